```python
import math
import jax
import jax.numpy as jnp
from jax import lax
import numpy as np

D_MODEL = 2048
BATCH = 4
SEQ = 4096
DEPTH = 4

GRID_W = 64
CTX_LEN = 256

D_MIX = D_MODEL
D_FOURIER = D_MIX // 2
FOURIER_HEADS = 4
FOURIER_HEAD_DIM = D_FOURIER // FOURIER_HEADS
D_SSM = D_MIX - D_FOURIER
SSM_GROUP = 16
SSM_GROUPS = D_SSM // SSM_GROUP
SSM_STATE = 64
DT_MIN = 0.001
DT_MAX = 0.1

N_MOD = 6

N_EXPERTS = 16
N_EXPERT_GROUPS = 4
EXPERTS_PER_GROUP = N_EXPERTS // N_EXPERT_GROUPS
TOP_K = 2
D_EXPERT = 3 * D_MODEL // 4
MOE_BLOCK = 128

EPS = 1e-6

kernel_name = "hymba_fnet_s5_moe_dit_prefix"

F32 = jnp.float32


def _rmsnorm(x, g):
    xf = x.astype(F32)
    y = xf * lax.rsqrt(jnp.mean(xf * xf, axis=-1, keepdims=True) + EPS)
    return (y * g.astype(F32)).astype(x.dtype)


def _unit_rms(y):
    yf = y.astype(F32)
    return yf * lax.rsqrt(jnp.mean(yf * yf, axis=-1, keepdims=True) + EPS)


def _fourier_mix(z, w_heads):
    bsz, length, _ = z.shape
    f = jnp.fft.fft2(z.astype(F32), axes=(1, 2), norm="ortho").real
    f = f.reshape(bsz, length, FOURIER_HEADS, FOURIER_HEAD_DIM)
    y = jnp.einsum("blhd,hde->blhe", f, w_heads.astype(F32))
    return y.reshape(bsz, length, D_FOURIER)


def _linear_scan(lbar, bu):
    a = jnp.broadcast_to(lbar, bu.shape[1:])[None]

    def combine(left, right):
        a_l, b_l = left
        a_r, b_r = right
        return a_l * a_r, a_r * b_l + b_r

    return lax.associative_scan(combine, (a, bu), axis=1)


def _s5_mixer(uc, ux, lam_re, lam_im, log_dt, b_re, b_im, c_re, c_im, d_skip, glu_w, glu_b, need_ctx):
    bsz, n_ctx, _ = uc.shape
    seq = ux.shape[1]
    ucg = uc.astype(F32).reshape(bsz, n_ctx, SSM_GROUPS, SSM_GROUP)
    uxg = ux.astype(F32).reshape(bsz, seq, SSM_GROUPS, SSM_GROUP)
    yx_dirs, yc_dirs = [], []
    for d in range(2):
        rev = d == 1
        lam = lax.complex(lam_re[d].astype(F32), lam_im[d].astype(F32))
        dt = jnp.exp(log_dt[d].astype(F32))[:, None]
        lbar = jnp.exp(lam * dt)
        bbar = ((lbar - 1.0) / lam)[..., None] * lax.complex(b_re[d].astype(F32), b_im[d].astype(F32))
        cmat = lax.complex(c_re[d].astype(F32), c_im[d].astype(F32))
        uc_d = jnp.flip(ucg, axis=1) if rev else ucg
        ux_d = jnp.flip(uxg, axis=1) if rev else uxg
        _, hc = _linear_scan(lbar, jnp.einsum("blgh,gph->blgp", uc_d, bbar))
        a_cum, hx = _linear_scan(lbar, jnp.einsum("blgh,gph->blgp", ux_d, bbar))
        hx = hx + a_cum * hc[:, -1:]
        y_x = jnp.einsum("blgp,ghp->blgh", hx, cmat).real
        yx_dirs.append(jnp.flip(y_x, axis=1) if rev else y_x)
        if need_ctx:
            y_c = jnp.einsum("blgp,ghp->blgh", hc, cmat).real
            yc_dirs.append(jnp.flip(y_c, axis=1) if rev else y_c)

    d_g = d_skip.astype(F32).reshape(SSM_GROUPS, SSM_GROUP)
    w_glu = glu_w.astype(F32)
    b_glu = glu_b.astype(F32)

    def out_glu(y, u, length):
        y = (y + d_g * u).reshape(bsz, length, D_SSM)
        g = jax.nn.gelu(y)
        return g * jax.nn.sigmoid(g @ w_glu + b_glu)

    out_x = out_glu(yx_dirs[0] + yx_dirs[1], uxg, seq)
    out_c = out_glu(yc_dirs[0] + yc_dirs[1], ucg, n_ctx) if need_ctx else None
    return out_c, out_x


def _merge_heads(y_f, y_s, gain, w_out, dtype):
    y = jnp.concatenate([_unit_rms(y_f), _unit_rms(y_s)], axis=-1) * gain.astype(F32)
    return y.astype(dtype) @ w_out


def _moe(h, router_w, router_b, w_gate, w_up, w_down):
    n = h.shape[0]
    scores = jax.nn.sigmoid(h.astype(F32) @ router_w.astype(F32))
    sel = (scores + router_b.astype(F32)).reshape(n, N_EXPERT_GROUPS, EXPERTS_PER_GROUP)
    grp_score = lax.top_k(sel, TOP_K)[0].sum(-1)
    g_star = jnp.argmax(grp_score, axis=-1)
    in_grp = jnp.take_along_axis(sel, g_star[:, None, None], axis=1)[:, 0]
    _, local = lax.top_k(in_grp, TOP_K)
    expert = g_star[:, None] * EXPERTS_PER_GROUP + local
    w = jnp.take_along_axis(scores, expert, axis=1)
    w = w / jnp.sum(w, axis=-1, keepdims=True)

    a = n * TOP_K
    flat_e = expert.reshape(a)
    flat_tok = jnp.repeat(jnp.arange(n, dtype=jnp.int32), TOP_K)
    flat_w = w.reshape(a)
    order = jnp.argsort(flat_e)
    se = flat_e[order]
    counts = jnp.bincount(flat_e, length=N_EXPERTS)
    padded = (counts + MOE_BLOCK - 1) // MOE_BLOCK * MOE_BLOCK
    pad_end = jnp.cumsum(padded)
    pad_start = pad_end - padded
    start = jnp.cumsum(counts) - counts
    dest = pad_start[se] + jnp.arange(a, dtype=jnp.int32) - start[se]
    n_blocks = -(-a // MOE_BLOCK) + N_EXPERTS
    buf_tok = jnp.zeros((n_blocks * MOE_BLOCK,), jnp.int32).at[dest].set(flat_tok[order])
    buf_w = jnp.zeros((n_blocks * MOE_BLOCK,), F32).at[dest].set(flat_w[order])
    blk_exp = jnp.minimum(
        jnp.searchsorted(pad_end, jnp.arange(n_blocks, dtype=jnp.int32) * MOE_BLOCK, side="right"),
        N_EXPERTS - 1)

    def expert_rows(args):
        tok, wt, e = args
        xb = h[tok]
        y = (jax.nn.silu(xb @ w_gate[e]) * (xb @ w_up[e])) @ w_down[e]
        return y * wt[:, None].astype(y.dtype)

    yb = lax.map(expert_rows, (buf_tok.reshape(n_blocks, MOE_BLOCK),
                               buf_w.reshape(n_blocks, MOE_BLOCK), blk_exp))
    return jnp.zeros_like(h).at[buf_tok].add(yb.reshape(n_blocks * MOE_BLOCK, -1))


def setup_inputs(seed: int = 0) -> dict:
    key = jax.random.key(seed)
    ks = jax.random.split(key, 32)
    D = D_MODEL

    def nrm(k, shape, scale):
        return jax.random.normal(k, shape, F32) * scale

    n_idx = jnp.arange(SSM_STATE, dtype=F32)
    ssm_shape = (DEPTH, 2, SSM_GROUPS, SSM_STATE)
    return {
        "x": nrm(ks[0], (BATCH, SEQ, D), 1.0),
        "c": nrm(ks[1], (BATCH, D), 1.0),
        "ctx": nrm(ks[2], (BATCH, CTX_LEN, D), 1.0),
        "c_ctx": nrm(ks[3], (D,), 1.0),
        "w_mod": nrm(ks[4], (DEPTH, D, N_MOD * D), 0.5 * D ** -0.5),
        "b_mod": nrm(ks[5], (DEPTH, N_MOD * D), 0.02),
        "norm1_g": 1.0 + nrm(ks[6], (DEPTH, D), 0.02),
        "norm2_g": 1.0 + nrm(ks[7], (DEPTH, D), 0.02),
        "w_in": nrm(ks[8], (DEPTH, D, D_MIX), D ** -0.5),
        "w_out": nrm(ks[9], (DEPTH, D_MIX, D), D_MIX ** -0.5),
        "fourier_w": nrm(ks[10], (DEPTH, FOURIER_HEADS, FOURIER_HEAD_DIM, FOURIER_HEAD_DIM), FOURIER_HEAD_DIM ** -0.5),
        "mix_norm_g": 1.0 + nrm(ks[11], (DEPTH, D_MIX), 0.02),
        "lam_re": -0.5 + nrm(ks[12], ssm_shape, 0.01),
        "lam_im": math.pi * n_idx + nrm(ks[13], ssm_shape, 0.01),
        "log_dt": jax.random.uniform(ks[14], (DEPTH, 2, SSM_GROUPS), F32, math.log(DT_MIN), math.log(DT_MAX)),
        "b_re": nrm(ks[15], (DEPTH, 2, SSM_GROUPS, SSM_STATE, SSM_GROUP), SSM_GROUP ** -0.5),
        "b_im": nrm(ks[16], (DEPTH, 2, SSM_GROUPS, SSM_STATE, SSM_GROUP), SSM_GROUP ** -0.5),
        "c_re": nrm(ks[17], (DEPTH, 2, SSM_GROUPS, SSM_GROUP, SSM_STATE), SSM_STATE ** -0.5),
        "c_im": nrm(ks[18], (DEPTH, 2, SSM_GROUPS, SSM_GROUP, SSM_STATE), SSM_STATE ** -0.5),
        "d_skip": nrm(ks[19], (DEPTH, D_SSM), 1.0),
        "glu_w": nrm(ks[20], (DEPTH, D_SSM, D_SSM), D_SSM ** -0.5),
        "glu_b": nrm(ks[21], (DEPTH, D_SSM), 0.02),
        "router_w": nrm(ks[22], (D, N_EXPERTS), D ** -0.5),
        "router_b": nrm(ks[23], (N_EXPERTS,), 0.01),
        "w_gate": nrm(ks[24], (DEPTH, N_EXPERTS, D, D_EXPERT), D ** -0.5),
        "w_up": nrm(ks[25], (DEPTH, N_EXPERTS, D, D_EXPERT), D ** -0.5),
        "w_down": nrm(ks[26], (DEPTH, N_EXPERTS, D_EXPERT, D), D_EXPERT ** -0.5),
        "final_g": 1.0 + nrm(ks[27], (D,), 0.02),
    }


def reference(x, c, ctx, c_ctx, w_mod, b_mod, norm1_g, norm2_g, w_in, w_out, fourier_w, mix_norm_g,
              lam_re, lam_im, log_dt, b_re, b_im, c_re, c_im, d_skip, glu_w, glu_b,
              router_w, router_b, w_gate, w_up, w_down, final_g):
    bsz, seq, dm = x.shape
    n_ctx = ctx.shape[1]
    for l in range(DEPTH):
        last = l == DEPTH - 1
        mod_x = jax.nn.silu(c) @ w_mod[l] + b_mod[l]
        mod_c = jax.nn.silu(c_ctx) @ w_mod[l] + b_mod[l]
        sh1x, sc1x, g1x, sh2x, sc2x, g2x = jnp.split(mod_x[:, None, :], N_MOD, axis=-1)
        sh1c, sc1c, g1c, sh2c, sc2c, g2c = jnp.split(mod_c, N_MOD)

        hx = _rmsnorm(x, norm1_g[l]) * (1.0 + sc1x) + sh1x
        hc = _rmsnorm(ctx, norm1_g[l]) * (1.0 + sc1c) + sh1c
        zx = hx @ w_in[l]
        col0 = D_FOURIER if last else 0
        zc = hc @ w_in[l][:, col0:]
        uc = zc[..., D_FOURIER - col0:]
        yfx = _fourier_mix(zx[..., :D_FOURIER], fourier_w[l])
        ysc, ysx = _s5_mixer(uc, zx[..., D_FOURIER:], lam_re[l], lam_im[l], log_dt[l], b_re[l], b_im[l],
                             c_re[l], c_im[l], d_skip[l], glu_w[l], glu_b[l], need_ctx=not last)
        x = x + g1x * _merge_heads(yfx, ysx, mix_norm_g[l], w_out[l], x.dtype)
        if not last:
            yfc = _fourier_mix(zc[..., :D_FOURIER], fourier_w[l])
            ctx = ctx + g1c * _merge_heads(yfc, ysc, mix_norm_g[l], w_out[l], ctx.dtype)

        hx2 = _rmsnorm(x, norm2_g[l]) * (1.0 + sc2x) + sh2x
        if last:
            yx = _moe(hx2.reshape(-1, dm), router_w, router_b, w_gate[l], w_up[l], w_down[l])
            x = x + g2x * yx.reshape(x.shape)
        else:
            hc2 = _rmsnorm(ctx, norm2_g[l]) * (1.0 + sc2c) + sh2c
            tokens = jnp.concatenate([hc2.reshape(-1, dm), hx2.reshape(-1, dm)], axis=0)
            y = _moe(tokens, router_w, router_b, w_gate[l], w_up[l], w_down[l])
            ctx = ctx + g2c * y[:bsz * n_ctx].reshape(ctx.shape)
            x = x + g2x * y[bsz * n_ctx:].reshape(x.shape)
    return _rmsnorm(x, final_g)
```

```python
import functools
import math

import jax
import jax.numpy as jnp
from jax import lax
from jax.experimental import pallas as pl
from jax.experimental.pallas import tpu as pltpu

F32 = jnp.float32
BF16 = jnp.bfloat16
EPS = 1e-6

FOURIER_HEADS = 4
SSM_GROUP = 16
N_EXPERT_GROUPS = 4
TOP_K = 2
N_MOD = 6

SSM_CHUNK = 16
S5_GROUPS_PER_STEP = 4
ROW_TILE = 256
MOE_ROWS = 512
MOE_FTILE = 256
LANES = 128
VMEM_LIMIT_BYTES = 56 * 1024 * 1024


def _cp(sems):
    return pltpu.CompilerParams(dimension_semantics=sems, vmem_limit_bytes=VMEM_LIMIT_BYTES)


def _mod_row(i, tm, n_ctx_rows, seq):
    r = i * tm
    return jnp.where(r < n_ctx_rows, 0, 1 + (r - n_ctx_rows) // seq)


def _mod_kernel(c_ref, w_ref, b_ref, o_ref):
    c = c_ref[...]
    s = c * jax.nn.sigmoid(c)
    o_ref[...] = jnp.dot(s.astype(BF16), w_ref[...].astype(BF16),
                         preferred_element_type=F32) + b_ref[...]


def _adaln(cvec, w_mod, b_mod):
    depth, d, n = w_mod.shape
    tn = 1024
    return pl.pallas_call(
        _mod_kernel,
        grid=(depth, n // tn),
        in_specs=[pl.BlockSpec((8, d), lambda l, j: (0, 0)),
                  pl.BlockSpec((None, d, tn), lambda l, j: (l, 0, j)),
                  pl.BlockSpec((None, 1, tn), lambda l, j: (l, 0, j))],
        out_specs=pl.BlockSpec((None, 8, tn), lambda l, j: (l, 0, j)),
        out_shape=jax.ShapeDtypeStruct((depth, 8, n), F32),
        compiler_params=_cp(("parallel", "parallel")),
        name="adaln",
    )(cvec, w_mod, b_mod.reshape(depth, 1, n))


def _inproj_kernel(t_ref, g_ref, sc_ref, sh_ref, w_ref, o_ref):
    x = t_ref[...]
    h = x * lax.rsqrt(jnp.mean(x * x, axis=-1, keepdims=True) + EPS) * g_ref[...]
    h = h * (1.0 + sc_ref[0]) + sh_ref[0]
    o_ref[...] = jnp.dot(h.astype(BF16), w_ref[...], preferred_element_type=F32).astype(o_ref.dtype)


def _inproj(t, g, sc, sh, w, n_ctx_rows, seq):
    nt, d = t.shape
    n = w.shape[1]
    tm = ROW_TILE
    mrow = functools.partial(_mod_row, tm=tm, n_ctx_rows=n_ctx_rows, seq=seq)
    return pl.pallas_call(
        _inproj_kernel,
        grid=(nt // tm,),
        in_specs=[pl.BlockSpec((tm, d), lambda i: (i, 0)),
                  pl.BlockSpec((1, d), lambda i: (0, 0)),
                  pl.BlockSpec((1, 1, d), lambda i: (mrow(i), 0, 0)),
                  pl.BlockSpec((1, 1, d), lambda i: (mrow(i), 0, 0)),
                  pl.BlockSpec((d, n), lambda i: (0, 0))],
        out_specs=pl.BlockSpec((tm, n), lambda i: (i, 0)),
        out_shape=jax.ShapeDtypeStruct((nt, n), BF16),
        compiler_params=_cp(("parallel",)),
        name="inproj",
    )(t, g, sc, sh, w)


def _dft_tables(n):
    r = 1
    while r * r < n:
        r *= 2
    q = n // r
    k = jnp.arange(n, dtype=jnp.int32)[:, None]
    step = 2.0 * math.pi / n
    pa = ((k * (jnp.arange(q, dtype=jnp.int32)[None, :] * r)) % n).astype(F32) * step
    pb = ((k * jnp.arange(r, dtype=jnp.int32)[None, :]) % n).astype(F32) * step
    ca, sa = jnp.cos(pa)[:, :, None], jnp.sin(pa)[:, :, None]
    cb, sb = jnp.cos(pb)[:, None, :], jnp.sin(pb)[:, None, :]
    c = (ca * cb - sa * sb).reshape(n, n)
    s = (sa * cb + ca * sb).reshape(n, n)
    return c, s


def _chan_dft_kernel(z_ref, w_ref, o_ref):
    o_ref[...] = jnp.dot(z_ref[...], w_ref[...], preferred_element_type=F32).astype(o_ref.dtype)


def _chan_dft(z, w):
    nt = z.shape[0]
    df, n = w.shape
    tm = 512
    return pl.pallas_call(
        _chan_dft_kernel,
        grid=(nt // tm,),
        in_specs=[pl.BlockSpec((tm, df), lambda i: (i, 0)),
                  pl.BlockSpec((df, n), lambda i: (0, 0))],
        out_specs=pl.BlockSpec((tm, n), lambda i: (i, 0)),
        out_shape=jax.ShapeDtypeStruct((nt, n), BF16),
        compiler_params=_cp(("parallel",)),
        name="chan_dft",
    )(z, w)


def _row_dft_kernel(cl_ref, sl_ref, a_ref, b_ref, hw_ref, gain_ref, o_ref, acc_ref, *, scale):
    k = pl.program_id(2)

    @pl.when(k == 0)
    def _():
        acc_ref[...] = jnp.zeros_like(acc_ref)

    acc_ref[...] += (jnp.dot(cl_ref[...], a_ref[...], preferred_element_type=F32)
                     + jnp.dot(sl_ref[...], b_ref[...], preferred_element_type=F32))

    @pl.when(k == pl.num_programs(2) - 1)
    def _():
        f = acc_ref[...] * scale
        nh, hd, _ = hw_ref.shape
        ys = [jnp.dot(f[:, h * hd:(h + 1) * hd].astype(BF16), hw_ref[h],
                      preferred_element_type=F32) for h in range(nh)]
        ss = ys[0] * ys[0]
        ssum = jnp.sum(ss, axis=-1, keepdims=True)
        for h in range(1, nh):
            ssum = ssum + jnp.sum(ys[h] * ys[h], axis=-1, keepdims=True)
        r = lax.rsqrt(ssum / (nh * hd) + EPS)
        for h in range(nh):
            o_ref[:, h * hd:(h + 1) * hd] = (ys[h] * r * gain_ref[:, h * hd:(h + 1) * hd]).astype(o_ref.dtype)


def _row_dft(ab, cl, sl_neg, hw, gain, row0, bsz, length, tm, tk):
    df = ab.shape[1] // 2
    kb = length // tk
    mb = length // tm
    off = row0 // tk
    scale = 1.0 / math.sqrt(length * df)
    return pl.pallas_call(
        functools.partial(_row_dft_kernel, scale=scale),
        grid=(bsz, mb, kb),
        in_specs=[pl.BlockSpec((tm, tk), lambda b, m, k: (m, k)),
                  pl.BlockSpec((tm, tk), lambda b, m, k: (m, k)),
                  pl.BlockSpec((tk, df), lambda b, m, k: (off + b * kb + k, 0)),
                  pl.BlockSpec((tk, df), lambda b, m, k: (off + b * kb + k, 1)),
                  pl.BlockSpec(hw.shape, lambda b, m, k: (0, 0, 0)),
                  pl.BlockSpec((1, df), lambda b, m, k: (0, 0))],
        out_specs=pl.BlockSpec((tm, df), lambda b, m, k: (b * mb + m, 0)),
        out_shape=jax.ShapeDtypeStruct((bsz * length, df), BF16),
        scratch_shapes=[pltpu.VMEM((tm, df), F32)],
        compiler_params=_cp(("parallel", "parallel", "arbitrary")),
        name="row_dft",
    )(cl, sl_neg, ab, ab, hw, gain)


def _s5_tables(lam_re, lam_im, log_dt, b_re, b_im, c_re, c_im, d_skip):
    tc = SSM_CHUNK
    g, p = lam_re.shape[1:]
    hch = b_re.shape[-1]
    lr, li = lam_re.astype(F32), lam_im.astype(F32)
    dt = jnp.exp(log_dt.astype(F32))[..., None]
    er, ei = lr * dt, li * dt

    def lpow(k):
        m = jnp.exp(er * k)
        return m * jnp.cos(ei * k), m * jnp.sin(ei * k)

    l1r, l1i = lpow(1.0)
    den = lr * lr + li * li
    qr = ((l1r - 1.0) * lr + l1i * li) / den
    qi = (l1i * lr - (l1r - 1.0) * li) / den
    br, bi = b_re.astype(F32), b_im.astype(F32)
    bbr = qr[..., None] * br - qi[..., None] * bi
    bbi = qr[..., None] * bi + qi[..., None] * br
    cr, ci = c_re.astype(F32), c_im.astype(F32)

    ks = jnp.arange(tc + 1, dtype=F32)
    pwr = jnp.stack([lpow(k)[0] for k in range(tc + 1)], axis=-1)
    pwi = jnp.stack([lpow(k)[1] for k in range(tc + 1)], axis=-1)
    del ks

    cbr = jnp.einsum("dgip,dgpj->dgpij", cr, bbr) - jnp.einsum("dgip,dgpj->dgpij", ci, bbi)
    cbi = jnp.einsum("dgip,dgpj->dgpij", cr, bbi) + jnp.einsum("dgip,dgpj->dgpij", ci, bbr)
    klag = (jnp.einsum("dgpij,dgpl->dglij", cbr, pwr[..., :tc])
            - jnp.einsum("dgpij,dgpl->dglij", cbi, pwi[..., :tc]))

    t_in = jnp.arange(tc)[:, None]
    t_out = jnp.arange(tc)[None, :]
    lag_f = t_out - t_in
    lag_b = t_in - t_out
    kf = klag[0][:, jnp.clip(lag_f, 0, tc - 1)] * (lag_f >= 0)[None, :, :, None, None]
    kb = klag[1][:, jnp.clip(lag_b, 0, tc - 1)] * (lag_b >= 0)[None, :, :, None, None]
    eye_t = (t_in == t_out).astype(F32)[None, :, :, None, None]
    dg = d_skip.astype(F32).reshape(g, hch)
    dmat = eye_t * (jnp.eye(hch, dtype=F32) * dg[:, None, :])[:, None, None, :, :]
    m = (kf + kb + dmat).transpose(0, 1, 4, 2, 3).reshape(g, tc * hch, tc * hch)

    idx_f = jnp.arange(tc - 1, -1, -1)
    idx_b = jnp.arange(tc)

    def st(d, idx):
        wr = pwr[d][..., idx][:, :, :, None] * bbr[d][:, :, None, :] - pwi[d][..., idx][:, :, :, None] * bbi[d][:, :, None, :]
        wi = pwr[d][..., idx][:, :, :, None] * bbi[d][:, :, None, :] + pwi[d][..., idx][:, :, :, None] * bbr[d][:, :, None, :]
        return (wr.transpose(0, 2, 3, 1).reshape(g, tc * hch, p),
                wi.transpose(0, 2, 3, 1).reshape(g, tc * hch, p))

    sfr, sfi = st(0, idx_f)
    sbr, sbi = st(1, idx_b)
    w_st = jnp.concatenate([sfr, sbr, sfi, sbi], axis=-1)

    def so(d, idx):
        wr = cr[d][:, None, :, :] * pwr[d][..., idx].transpose(0, 2, 1)[:, :, None, :] \
            - ci[d][:, None, :, :] * pwi[d][..., idx].transpose(0, 2, 1)[:, :, None, :]
        wi = cr[d][:, None, :, :] * pwi[d][..., idx].transpose(0, 2, 1)[:, :, None, :] \
            + ci[d][:, None, :, :] * pwr[d][..., idx].transpose(0, 2, 1)[:, :, None, :]
        return (wr.reshape(g, tc * hch, p).transpose(0, 2, 1),
                wi.reshape(g, tc * hch, p).transpose(0, 2, 1))

    ofr, ofi = so(0, jnp.arange(1, tc + 1))
    obr, obi = so(1, jnp.arange(tc, 0, -1))
    zero = jnp.zeros_like(ofr)
    w_of = jnp.concatenate([ofr, zero, -ofi, zero], axis=1)
    w_ob = jnp.concatenate([zero, obr, zero, -obi], axis=1)

    a_re = jnp.concatenate([pwr[0][..., tc], pwr[1][..., tc]], axis=-1)[:, None, :]
    a_im = jnp.concatenate([pwi[0][..., tc], pwi[1][..., tc]], axis=-1)[:, None, :]
    return w_st.astype(BF16), m.astype(BF16), w_of.astype(BF16), w_ob.astype(BF16), a_re, a_im


def _s5_kernel(u_ref, wst_ref, m_ref, wof_ref, wob_ref, are_ref, aim_ref, y_ref,
               s_ref, hf_ref, hb_ref, *, gb, ctx_pairs, x_pairs, half):
    for j in range(gb):
        s_ref[j] = jnp.dot(u_ref[j], wst_ref[j], preferred_element_type=F32)

    n_pairs = ctx_pairs + x_pairs
    w2 = are_ref.shape[-1]
    sub = lax.broadcasted_iota(jnp.int32, (2 * half, w2), 0)
    lane = lax.broadcasted_iota(jnp.int32, (2 * half, w2), 1)
    lo = sub < half
    fwd = lane < (w2 // 2)
    a_re = [jnp.broadcast_to(are_ref[j], (2 * half, w2)) for j in range(gb)]
    a_im = [jnp.broadcast_to(aim_ref[j], (2 * half, w2)) for j in range(gb)]

    def step(i, carry):
        pf = i
        pb = jnp.where(i < ctx_pairs, ctx_pairs - 1 - i, n_pairs - 1 + ctx_pairs - i)
        rf = pl.multiple_of(pf * (2 * half), 2 * half)
        rb = pl.multiple_of(pb * (2 * half), 2 * half)
        out = []
        for j in range(gb):
            hr, hi = carry[2 * j], carry[2 * j + 1]
            firsts, seconds = [], []
            for part in range(2):
                sf = s_ref[j, pl.ds(rf, 2 * half), part * w2:(part + 1) * w2]
                sb = s_ref[j, pl.ds(rb, 2 * half), part * w2:(part + 1) * w2]
                sf_sw = pltpu.roll(sf, half, 0)
                sb_sw = pltpu.roll(sb, half, 0)
                f_lo = jnp.where(lo, sf, sf_sw)
                f_hi = jnp.where(lo, sf_sw, sf)
                b_lo = jnp.where(lo, sb, sb_sw)
                b_hi = jnp.where(lo, sb_sw, sb)
                firsts.append(jnp.where(fwd, f_lo, b_hi))
                seconds.append(jnp.where(fwd, f_hi, b_lo))
            h1r = a_re[j] * hr - a_im[j] * hi + firsts[0]
            h1i = a_re[j] * hi + a_im[j] * hr + firsts[1]
            h2r = a_re[j] * h1r - a_im[j] * h1i + seconds[0]
            h2i = a_re[j] * h1i + a_im[j] * h1r + seconds[1]
            hf_ref[j, pl.ds(rf, 2 * half), 0:w2] = jnp.where(lo, hr, h1r)
            hf_ref[j, pl.ds(rf, 2 * half), w2:2 * w2] = jnp.where(lo, hi, h1i)
            hb_ref[j, pl.ds(rb, 2 * half), 0:w2] = jnp.where(lo, h1r, hr)
            hb_ref[j, pl.ds(rb, 2 * half), w2:2 * w2] = jnp.where(lo, h1i, hi)
            out += [h2r, h2i]
        return tuple(out)

    zero = jnp.zeros((2 * half, w2), F32)
    lax.fori_loop(0, n_pairs, step, tuple([zero] * (2 * gb)))

    for j in range(gb):
        y = (jnp.dot(u_ref[j], m_ref[j], preferred_element_type=F32)
             + jnp.dot(hf_ref[j].astype(BF16), wof_ref[j], preferred_element_type=F32)
             + jnp.dot(hb_ref[j].astype(BF16), wob_ref[j], preferred_element_type=F32))
        y_ref[j] = y.astype(y_ref.dtype)


def _s5_scan(u, tables, bsz, ctx_chunks, x_chunks):
    w_st, m, w_of, w_ob, a_re, a_im = tables
    g, r, kdim = u.shape
    gb = S5_GROUPS_PER_STEP
    ns = w_st.shape[-1]
    assert 2 * bsz == 8 and ctx_chunks % 2 == 0 and x_chunks % 2 == 0
    kern = functools.partial(_s5_kernel, gb=gb, ctx_pairs=ctx_chunks // 2, x_pairs=x_chunks // 2, half=bsz)
    blk = lambda shp: pl.BlockSpec((gb,) + shp, lambda i: (i, 0, 0))
    return pl.pallas_call(
        kern,
        grid=(g // gb,),
        in_specs=[blk((r, kdim)), blk((kdim, ns)), blk((kdim, kdim)), blk((ns, kdim)), blk((ns, kdim)),
                  blk((1, ns // 2)), blk((1, ns // 2))],
        out_specs=blk((r, kdim)),
        out_shape=jax.ShapeDtypeStruct((g, r, kdim), BF16),
        scratch_shapes=[pltpu.VMEM((gb, r, ns), F32), pltpu.VMEM((gb, r, ns), F32), pltpu.VMEM((gb, r, ns), F32)],
        compiler_params=_cp(("parallel",)),
        name="s5_scan",
    )(u, w_st, m, w_of, w_ob, a_re, a_im)


def _glu_kernel(y_ref, w_ref, b_ref, gain_ref, o_ref):
    y = y_ref[...].astype(F32)
    g = jax.nn.gelu(y)
    v = g * jax.nn.sigmoid(jnp.dot(g.astype(BF16), w_ref[...], preferred_element_type=F32) + b_ref[...])
    r = lax.rsqrt(jnp.mean(v * v, axis=-1, keepdims=True) + EPS)
    o_ref[...] = (v * r * gain_ref[...]).astype(o_ref.dtype)


def _glu(y, w, b, gain):
    nt, ds = y.shape
    tm = 512
    return pl.pallas_call(
        _glu_kernel,
        grid=(nt // tm,),
        in_specs=[pl.BlockSpec((tm, ds), lambda i: (i, 0)),
                  pl.BlockSpec((ds, ds), lambda i: (0, 0)),
                  pl.BlockSpec((1, ds), lambda i: (0, 0)),
                  pl.BlockSpec((1, ds), lambda i: (0, 0))],
        out_specs=pl.BlockSpec((tm, ds), lambda i: (i, 0)),
        out_shape=jax.ShapeDtypeStruct((nt, ds), BF16),
        compiler_params=_cp(("parallel",)),
        name="glu",
    )(y, w, b, gain)


def _route(scores, sel, n_groups):
    epg = len(sel) // n_groups
    gscore = []
    for q in range(n_groups):
        v = sel[q * epg:(q + 1) * epg]
        best = None
        for a in range(epg):
            for b in range(a + 1, epg):
                s = v[a] + v[b]
                best = s if best is None else jnp.maximum(best, s)
        gscore.append(best)
    gbest = gscore[0]
    gidx = jnp.zeros(gbest.shape, jnp.int32)
    for q in range(1, n_groups):
        upd = gscore[q] > gbest
        gbest = jnp.where(upd, gscore[q], gbest)
        gidx = jnp.where(upd, q, gidx)
    vin = list(sel[:epg])
    sin = list(scores[:epg])
    for q in range(1, n_groups):
        pick = gidx == q
        for j in range(epg):
            vin[j] = jnp.where(pick, sel[q * epg + j], vin[j])
            sin[j] = jnp.where(pick, scores[q * epg + j], sin[j])
    b1 = vin[0]
    i1 = jnp.zeros(gbest.shape, jnp.int32)
    for j in range(1, epg):
        upd = vin[j] > b1
        b1 = jnp.where(upd, vin[j], b1)
        i1 = jnp.where(upd, j, i1)
    b2 = vin[0]
    i2 = jnp.zeros(gbest.shape, jnp.int32)
    have = jnp.zeros(gbest.shape, jnp.bool_)
    for j in range(epg):
        cand = i1 != j
        upd = cand & (jnp.logical_not(have) | (vin[j] > b2))
        b2 = jnp.where(upd, vin[j], b2)
        i2 = jnp.where(upd, j, i2)
        have = have | cand
    s1 = sin[0]
    s2 = sin[0]
    for j in range(1, epg):
        s1 = jnp.where(i1 == j, sin[j], s1)
        s2 = jnp.where(i2 == j, sin[j], s2)
    tot = s1 + s2
    return (gidx * epg + i1, gidx * epg + i2), (s1 / tot, s2 / tot)


def _merge_kernel(t_ref, yf_ref, ys_ref, wo_ref, g1_ref, n2_ref, sc2_ref, sh2_ref, rwt_ref, rb_ref,
                  tn_ref, h2_ref, eidx_ref, ew_ref):
    df = yf_ref.shape[1]
    o = (jnp.dot(yf_ref[...], wo_ref[0:df, :], preferred_element_type=F32)
         + jnp.dot(ys_ref[...], wo_ref[df:, :], preferred_element_type=F32))
    tn = t_ref[...] + g1_ref[0] * o
    tn_ref[...] = tn
    h2 = tn * lax.rsqrt(jnp.mean(tn * tn, axis=-1, keepdims=True) + EPS) * n2_ref[...]
    h2 = h2 * (1.0 + sc2_ref[0]) + sh2_ref[0]
    h2_ref[...] = h2
    logits = lax.dot_general(rwt_ref[...], h2, (((1,), (1,)), ((), ())),
                             precision=lax.Precision.HIGHEST, preferred_element_type=F32)
    scores = jax.nn.sigmoid(logits)
    sel = scores + rb_ref[...]
    ne = scores.shape[0]
    srows = [scores[e:e + 1, :] for e in range(ne)]
    vrows = [sel[e:e + 1, :] for e in range(ne)]
    (e1, e2), (w1, w2) = _route(srows, vrows, N_EXPERT_GROUPS)
    eidx_ref[...] = jnp.zeros_like(eidx_ref)
    ew_ref[...] = jnp.zeros_like(ew_ref)
    eidx_ref[0:1, :] = e1
    eidx_ref[1:2, :] = e2
    ew_ref[0:1, :] = w1
    ew_ref[1:2, :] = w2


def _merge(t, yf, ys, wo, g1, n2, sc2, sh2, rwt, rb, n_ctx_rows, seq):
    nt, d = t.shape
    df = yf.shape[1]
    ne = rwt.shape[0]
    tm = ROW_TILE
    mrow = functools.partial(_mod_row, tm=tm, n_ctx_rows=n_ctx_rows, seq=seq)
    mspec = pl.BlockSpec((1, 1, d), lambda i: (mrow(i), 0, 0))
    return pl.pallas_call(
        _merge_kernel,
        grid=(nt // tm,),
        in_specs=[pl.BlockSpec((tm, d), lambda i: (i, 0)),
                  pl.BlockSpec((tm, df), lambda i: (i, 0)),
                  pl.BlockSpec((tm, d - df), lambda i: (i, 0)),
                  pl.BlockSpec((d, d), lambda i: (0, 0)),
                  mspec,
                  pl.BlockSpec((1, d), lambda i: (0, 0)),
                  mspec, mspec,
                  pl.BlockSpec((ne, d), lambda i: (0, 0)),
                  pl.BlockSpec((ne, 1), lambda i: (0, 0))],
        out_specs=[pl.BlockSpec((tm, d), lambda i: (i, 0)),
                   pl.BlockSpec((tm, d), lambda i: (i, 0)),
                   pl.BlockSpec((8, tm), lambda i: (0, i)),
                   pl.BlockSpec((8, tm), lambda i: (0, i))],
        out_shape=[jax.ShapeDtypeStruct((nt, d), F32),
                   jax.ShapeDtypeStruct((nt, d), F32),
                   jax.ShapeDtypeStruct((8, nt), jnp.int32),
                   jax.ShapeDtypeStruct((8, nt), F32)],
        compiler_params=_cp(("parallel",)),
        name="merge_route",
    )(t, yf, ys, wo, g1, n2, sc2, sh2, rwt, rb)


def _dispatch(eidx, ew, n_experts, rows, n_blocks):
    nt = eidx.shape[1]
    a = nt * TOP_K
    flat_e = eidx[:TOP_K].T.reshape(a)
    flat_w = ew[:TOP_K].T.reshape(a)
    onehot = (flat_e[:, None] == jnp.arange(n_experts, dtype=jnp.int32)[None, :]).astype(jnp.int32)
    csum = jnp.cumsum(onehot, axis=0)
    counts = csum[-1]
    padded = (counts + rows - 1) // rows * rows
    pad_end = jnp.cumsum(padded)
    pad_start = pad_end - padded
    dest = jnp.sum(onehot * (csum - 1 + pad_start[None, :]), axis=1)
    tok = jnp.arange(a, dtype=jnp.int32) // TOP_K
    buf_tok = jnp.zeros((n_blocks * rows,), jnp.int32).at[dest].set(tok)
    buf_w = jnp.zeros((n_blocks * rows,), F32).at[dest].set(flat_w)
    n_valid = (pad_end[-1] // rows).astype(jnp.int32)
    blk_start = jnp.arange(n_blocks, dtype=jnp.int32) * rows
    blk_exp = jnp.sum((pad_end[None, :] <= blk_start[:, None]).astype(jnp.int32), axis=1)
    blk_exp = jnp.minimum(blk_exp, n_experts - 1)
    last_exp = jnp.sum(jnp.where(jnp.arange(n_blocks) == n_valid - 1, blk_exp, 0))
    blk_exp = jnp.where(jnp.arange(n_blocks) < n_valid, blk_exp, last_exp).astype(jnp.int32)
    return buf_tok, buf_w, blk_exp, n_valid.reshape(1), dest.reshape(nt, TOP_K)


def _expert_kernel(be_ref, nv_ref, tok_ref, h_hbm, wg_ref, wu_ref, wd_ref, rw_ref, o_ref,
                   xg_ref, xb_ref, acc_ref, sem):
    i = pl.program_id(0)
    f = pl.program_id(1)
    nf = pl.num_programs(1)
    rows, nsub, _ = xg_ref.shape
    valid = i < nv_ref[0]

    @pl.when(valid & (f == 0))
    def _():
        def issue(r, c):
            pltpu.make_async_copy(h_hbm.at[tok_ref[0, r]], xg_ref.at[r], sem).start()
            return c

        lax.fori_loop(0, rows, issue, 0)
        pltpu.make_async_copy(h_hbm.at[pl.ds(0, rows)], xg_ref, sem).wait()
        for c in range(nsub):
            xb_ref[:, c * LANES:(c + 1) * LANES] = xg_ref[:, c, :].astype(BF16)

    @pl.when(valid)
    def _():
        x = xb_ref[...]
        g = jnp.dot(x, wg_ref[...].astype(BF16), preferred_element_type=F32)
        u = jnp.dot(x, wu_ref[...].astype(BF16), preferred_element_type=F32)
        hmid = (g * jax.nn.sigmoid(g)) * u
        contrib = jnp.dot(hmid.astype(BF16), wd_ref[...].astype(BF16), preferred_element_type=F32)

        @pl.when(f == 0)
        def _():
            acc_ref[...] = contrib

        @pl.when(f > 0)
        def _():
            acc_ref[...] += contrib

        @pl.when(f == nf - 1)
        def _():
            o_ref[...] = (acc_ref[...] * rw_ref[...]).astype(o_ref.dtype)

    @pl.when(jnp.logical_not(valid) & (f == nf - 1))
    def _():
        o_ref[...] = jnp.zeros_like(o_ref)


def _experts(h3, buf_tok, buf_w, blk_exp, n_valid, w_gate, w_up, w_down):
    nt, nsub, lanes = h3.shape
    ne, d, de = w_gate.shape
    rows, tf = MOE_ROWS, MOE_FTILE
    n_blocks = buf_tok.shape[0] // rows
    nf = de // tf

    def fidx(i, f, nv):
        return jnp.where(i < nv[0], f, nf - 1)

    grid_spec = pltpu.PrefetchScalarGridSpec(
        num_scalar_prefetch=2,
        grid=(n_blocks, nf),
        in_specs=[pl.BlockSpec((None, 1, rows), lambda i, f, be, nv: (i, 0, 0), memory_space=pltpu.SMEM),
                  pl.BlockSpec(memory_space=pl.ANY),
                  pl.BlockSpec((None, d, tf), lambda i, f, be, nv: (be[i], 0, fidx(i, f, nv))),
                  pl.BlockSpec((None, d, tf), lambda i, f, be, nv: (be[i], 0, fidx(i, f, nv))),
                  pl.BlockSpec((None, tf, d), lambda i, f, be, nv: (be[i], fidx(i, f, nv), 0)),
                  pl.BlockSpec((rows, 1), lambda i, f, be, nv: (i, 0))],
        out_specs=pl.BlockSpec((rows, d), lambda i, f, be, nv: (i, 0)),
        scratch_shapes=[pltpu.VMEM((rows, nsub, lanes), F32),
                        pltpu.VMEM((rows, d), BF16),
                        pltpu.VMEM((rows, d), F32),
                        pltpu.SemaphoreType.DMA],
    )
    return pl.pallas_call(
        _expert_kernel,
        grid_spec=grid_spec,
        out_shape=jax.ShapeDtypeStruct((n_blocks * rows, d), BF16),
        compiler_params=_cp(("arbitrary", "arbitrary")),
        name="experts",
    )(blk_exp, n_valid, buf_tok.reshape(n_blocks, 1, rows), h3, w_gate, w_up, w_down,
      buf_w.reshape(n_blocks * rows, 1))


def _combine_kernel(pos_ref, t_ref, yb_hbm, g2_ref, o_ref, ga_ref, gb_ref, sem):
    tm, nsub, _ = ga_ref.shape

    def issue(r, c):
        pltpu.make_async_copy(yb_hbm.at[pos_ref[0, 2 * r]], ga_ref.at[r], sem).start()
        pltpu.make_async_copy(yb_hbm.at[pos_ref[0, 2 * r + 1]], gb_ref.at[r], sem).start()
        return c

    lax.fori_loop(0, tm, issue, 0)
    pltpu.make_async_copy(yb_hbm.at[pl.ds(0, tm)], ga_ref, sem).wait()
    pltpu.make_async_copy(yb_hbm.at[pl.ds(0, tm)], gb_ref, sem).wait()
    for c in range(nsub):
        sl = slice(c * LANES, (c + 1) * LANES)
        y = ga_ref[:, c, :].astype(F32) + gb_ref[:, c, :].astype(F32)
        o_ref[:, sl] = t_ref[:, sl] + g2_ref[0][:, sl] * y


def _combine(t, yb3, pos, g2, n_ctx_rows, seq):
    nt, d = t.shape
    _, nsub, lanes = yb3.shape
    tm = ROW_TILE
    mrow = functools.partial(_mod_row, tm=tm, n_ctx_rows=n_ctx_rows, seq=seq)
    return pl.pallas_call(
        _combine_kernel,
        grid=(nt // tm,),
        in_specs=[pl.BlockSpec((None, 1, TOP_K * tm), lambda i: (i, 0, 0), memory_space=pltpu.SMEM),
                  pl.BlockSpec((tm, d), lambda i: (i, 0)),
                  pl.BlockSpec(memory_space=pl.ANY),
                  pl.BlockSpec((1, 1, d), lambda i: (mrow(i), 0, 0))],
        out_specs=pl.BlockSpec((tm, d), lambda i: (i, 0)),
        out_shape=jax.ShapeDtypeStruct((nt, d), F32),
        scratch_shapes=[pltpu.VMEM((tm, nsub, lanes), yb3.dtype),
                        pltpu.VMEM((tm, nsub, lanes), yb3.dtype),
                        pltpu.SemaphoreType.DMA],
        compiler_params=_cp(("arbitrary",)),
        name="combine",
    )(pos.reshape(nt // tm, 1, TOP_K * tm), t, yb3, g2)


def _final_kernel(x_ref, g_ref, o_ref):
    x = x_ref[...]
    o_ref[...] = x * lax.rsqrt(jnp.mean(x * x, axis=-1, keepdims=True) + EPS) * g_ref[...]


def _final_norm(t, g, row0):
    nt, d = t.shape
    tm = ROW_TILE
    off = row0 // tm
    return pl.pallas_call(
        _final_kernel,
        grid=((nt - row0) // tm,),
        in_specs=[pl.BlockSpec((tm, d), lambda i: (off + i, 0)),
                  pl.BlockSpec((1, d), lambda i: (0, 0))],
        out_specs=pl.BlockSpec((tm, d), lambda i: (i, 0)),
        out_shape=jax.ShapeDtypeStruct((nt - row0, d), F32),
        compiler_params=_cp(("parallel",)),
        name="final_norm",
    )(t, g)


def _to_groups(u, bsz, chunks, g, hch):
    tc = SSM_CHUNK
    return u.reshape(bsz, chunks, tc, g, hch).transpose(3, 1, 0, 2, 4).reshape(g, chunks, bsz, tc * hch)


def _from_groups(y, bsz, chunks, g, hch):
    tc = SSM_CHUNK
    return y.reshape(g, chunks, bsz, tc, hch).transpose(2, 1, 3, 0, 4).reshape(bsz * chunks * tc, g * hch)


def kernel(x, c, ctx, c_ctx, w_mod, b_mod, norm1_g, norm2_g, w_in, w_out, fourier_w, mix_norm_g,
           lam_re, lam_im, log_dt, b_re, b_im, c_re, c_im, d_skip, glu_w, glu_b,
           router_w, router_b, w_gate, w_up, w_down, final_g):
    bsz, seq, d = x.shape
    n_ctx = ctx.shape[1]
    depth = w_mod.shape[0]
    df = fourier_w.shape[1] * fourier_w.shape[2]
    ds = d_skip.shape[1]
    g = lam_re.shape[2]
    hch = ds // g
    ne = router_w.shape[1]
    nc_rows = bsz * n_ctx
    nt = nc_rows + bsz * seq
    tc = SSM_CHUNK
    ctx_chunks, x_chunks = n_ctx // tc, seq // tc

    t = jnp.concatenate([ctx.reshape(nc_rows, d), x.reshape(bsz * seq, d)], axis=0).astype(F32)

    cvec = jnp.concatenate([c_ctx[None, :], c, jnp.zeros((8 - 1 - bsz, d), c.dtype)], axis=0).astype(F32)
    mod = _adaln(cvec, w_mod, b_mod).reshape(depth, 8, N_MOD, 1, d)

    cc, sc_ = _dft_tables(df)
    w_chan = jnp.concatenate([cc, sc_], axis=1).astype(BF16)
    clx, slx = _dft_tables(seq)
    clx, slx = clx.astype(BF16), (-slx).astype(BF16)
    clc, slc = _dft_tables(n_ctx)
    clc, slc = clc.astype(BF16), (-slc).astype(BF16)

    rwt = router_w.astype(F32).T
    rb = router_b.astype(F32).reshape(ne, 1)
    n_blocks = -(-(nt * TOP_K) // MOE_ROWS) + ne

    for l in range(depth):
        sh1, sc1, g1, sh2, sc2, g2 = [mod[l, :, k] for k in range(N_MOD)]
        z = _inproj(t, norm1_g[l].reshape(1, d).astype(F32), sc1, sh1, w_in[l].astype(BF16), nc_rows, seq)

        ab = _chan_dft(z, w_chan)
        hw = fourier_w[l].astype(BF16)
        gain = mix_norm_g[l].astype(F32).reshape(1, -1)
        yf_c = _row_dft(ab, clc, slc, hw, gain[:, :df], 0, bsz, n_ctx, n_ctx, n_ctx)
        yf_x = _row_dft(ab, clx, slx, hw, gain[:, :df], nc_rows, bsz, seq, 512, 512)
        yf = jnp.concatenate([yf_c, yf_x], axis=0)

        tables = _s5_tables(lam_re[l], lam_im[l], log_dt[l], b_re[l], b_im[l], c_re[l], c_im[l], d_skip[l])
        us = z[:, df:]
        u = jnp.concatenate([_to_groups(us[:nc_rows], bsz, ctx_chunks, g, hch),
                             _to_groups(us[nc_rows:], bsz, x_chunks, g, hch)], axis=1)
        u = u.reshape(g, (ctx_chunks + x_chunks) * bsz, tc * hch)
        ysc = _s5_scan(u, tables, bsz, ctx_chunks, x_chunks).reshape(g, ctx_chunks + x_chunks, bsz, tc * hch)
        ys_tok = jnp.concatenate([_from_groups(ysc[:, :ctx_chunks], bsz, ctx_chunks, g, hch),
                                  _from_groups(ysc[:, ctx_chunks:], bsz, x_chunks, g, hch)], axis=0)
        ys = _glu(ys_tok, glu_w[l].astype(BF16), glu_b[l].astype(F32).reshape(1, ds), gain[:, df:])

        t, h2, eidx, ew = _merge(t, yf, ys, w_out[l].astype(BF16), g1,
                                 norm2_g[l].reshape(1, d).astype(F32), sc2, sh2, rwt, rb, nc_rows, seq)

        buf_tok, buf_w, blk_exp, n_valid, pos = _dispatch(eidx, ew, ne, MOE_ROWS, n_blocks)
        yb = _experts(h2.reshape(nt, d // LANES, LANES), buf_tok, buf_w, blk_exp, n_valid,
                      w_gate[l], w_up[l], w_down[l])
        t = _combine(t, yb.reshape(n_blocks * MOE_ROWS, d // LANES, LANES), pos, g2, nc_rows, seq)

    out = _final_norm(t, final_g.reshape(1, d).astype(F32), nc_rows)
    return out.reshape(bsz, seq, d).astype(x.dtype)
```

```python
import functools
import math

import jax
import jax.numpy as jnp
from jax import lax
from jax.experimental import pallas as pl
from jax.experimental.pallas import tpu as pltpu

F32 = jnp.float32
BF16 = jnp.bfloat16
EPS = 1e-6

FOURIER_HEADS = 4
SSM_GROUP = 16
N_EXPERT_GROUPS = 4
TOP_K = 2
N_MOD = 6

SSM_CHUNK = 16
S5_GROUPS_PER_STEP = 4
ROW_TILE = 256
MOE_ROWS = 512
MOE_FTILE = 512
LANES = 128
GATHER_PITCH = 24
VMEM_LIMIT_BYTES = 56 * 1024 * 1024


def _cp(sems):
    return pltpu.CompilerParams(dimension_semantics=sems, vmem_limit_bytes=VMEM_LIMIT_BYTES)


def _mod_row(i, tm, n_ctx_rows, seq):
    r = i * tm
    return jnp.where(r < n_ctx_rows, 0, 1 + (r - n_ctx_rows) // seq)


def _mod_kernel(c_ref, w_ref, b_ref, o_ref):
    c = c_ref[...]
    s = c * jax.nn.sigmoid(c)
    o_ref[...] = jnp.dot(s.astype(BF16), w_ref[...].astype(BF16),
                         preferred_element_type=F32) + b_ref[...]


def _adaln(cvec, w_mod, b_mod):
    depth, d, n = w_mod.shape
    tn = 1024
    return pl.pallas_call(
        _mod_kernel,
        grid=(depth, n // tn),
        in_specs=[pl.BlockSpec((8, d), lambda l, j: (0, 0)),
                  pl.BlockSpec((None, d, tn), lambda l, j: (l, 0, j)),
                  pl.BlockSpec((None, 1, tn), lambda l, j: (l, 0, j))],
        out_specs=pl.BlockSpec((None, 8, tn), lambda l, j: (l, 0, j)),
        out_shape=jax.ShapeDtypeStruct((depth, 8, n), F32),
        compiler_params=_cp(("parallel", "parallel")),
        name="adaln",
    )(cvec, w_mod, b_mod.reshape(depth, 1, n))


def _inproj_kernel(t_ref, g_ref, sc_ref, sh_ref, w_ref, o_ref):
    x = t_ref[...]
    h = x * lax.rsqrt(jnp.mean(x * x, axis=-1, keepdims=True) + EPS) * g_ref[...]
    h = h * (1.0 + sc_ref[0]) + sh_ref[0]
    o_ref[...] = jnp.dot(h.astype(BF16), w_ref[...], preferred_element_type=F32).astype(o_ref.dtype)


def _inproj(t, g, sc, sh, w, n_ctx_rows, seq):
    nt, d = t.shape
    n = w.shape[1]
    tm = ROW_TILE
    mrow = functools.partial(_mod_row, tm=tm, n_ctx_rows=n_ctx_rows, seq=seq)
    return pl.pallas_call(
        _inproj_kernel,
        grid=(nt // tm,),
        in_specs=[pl.BlockSpec((tm, d), lambda i: (i, 0)),
                  pl.BlockSpec((1, d), lambda i: (0, 0)),
                  pl.BlockSpec((1, 1, d), lambda i: (mrow(i), 0, 0)),
                  pl.BlockSpec((1, 1, d), lambda i: (mrow(i), 0, 0)),
                  pl.BlockSpec((d, n), lambda i: (0, 0))],
        out_specs=pl.BlockSpec((tm, n), lambda i: (i, 0)),
        out_shape=jax.ShapeDtypeStruct((nt, n), BF16),
        compiler_params=_cp(("parallel",)),
        name="inproj",
    )(t, g, sc, sh, w)


def _dft_tables(n):
    r = 1
    while r * r < n:
        r *= 2
    q = n // r
    k = jnp.arange(n, dtype=jnp.int32)[:, None]
    step = 2.0 * math.pi / n
    pa = ((k * (jnp.arange(q, dtype=jnp.int32)[None, :] * r)) % n).astype(F32) * step
    pb = ((k * jnp.arange(r, dtype=jnp.int32)[None, :]) % n).astype(F32) * step
    ca, sa = jnp.cos(pa)[:, :, None], jnp.sin(pa)[:, :, None]
    cb, sb = jnp.cos(pb)[:, None, :], jnp.sin(pb)[:, None, :]
    c = (ca * cb - sa * sb).reshape(n, n)
    s = (sa * cb + ca * sb).reshape(n, n)
    return c, s


def _chan_dft_kernel(z_ref, w_ref, o_ref):
    o_ref[...] = jnp.dot(z_ref[...], w_ref[...], preferred_element_type=F32).astype(o_ref.dtype)


def _chan_dft(z, w):
    nt = z.shape[0]
    df, n = w.shape
    tm = 512
    return pl.pallas_call(
        _chan_dft_kernel,
        grid=(nt // tm,),
        in_specs=[pl.BlockSpec((tm, df), lambda i: (i, 0)),
                  pl.BlockSpec((df, n), lambda i: (0, 0))],
        out_specs=pl.BlockSpec((tm, n), lambda i: (i, 0)),
        out_shape=jax.ShapeDtypeStruct((nt, n), BF16),
        compiler_params=_cp(("parallel",)),
        name="chan_dft",
    )(z, w)


def _row_dft_kernel(cl_ref, sl_ref, a_ref, b_ref, hw_ref, gain_ref, o_ref, acc_ref, *, scale):
    k = pl.program_id(2)

    @pl.when(k == 0)
    def _():
        acc_ref[...] = jnp.zeros_like(acc_ref)

    acc_ref[...] += (jnp.dot(cl_ref[...], a_ref[...], preferred_element_type=F32)
                     + jnp.dot(sl_ref[...], b_ref[...], preferred_element_type=F32))

    @pl.when(k == pl.num_programs(2) - 1)
    def _():
        f = acc_ref[...] * scale
        nh, hd, _ = hw_ref.shape
        ys = [jnp.dot(f[:, h * hd:(h + 1) * hd].astype(BF16), hw_ref[h],
                      preferred_element_type=F32) for h in range(nh)]
        ss = ys[0] * ys[0]
        ssum = jnp.sum(ss, axis=-1, keepdims=True)
        for h in range(1, nh):
            ssum = ssum + jnp.sum(ys[h] * ys[h], axis=-1, keepdims=True)
        r = lax.rsqrt(ssum / (nh * hd) + EPS)
        for h in range(nh):
            o_ref[:, h * hd:(h + 1) * hd] = (ys[h] * r * gain_ref[:, h * hd:(h + 1) * hd]).astype(o_ref.dtype)


def _row_dft(ab, cl, sl_neg, hw, gain, row0, bsz, length, tm, tk):
    df = ab.shape[1] // 2
    kb = length // tk
    mb = length // tm
    off = row0 // tk
    scale = 1.0 / math.sqrt(length * df)
    return pl.pallas_call(
        functools.partial(_row_dft_kernel, scale=scale),
        grid=(bsz, mb, kb),
        in_specs=[pl.BlockSpec((tm, tk), lambda b, m, k: (m, k)),
                  pl.BlockSpec((tm, tk), lambda b, m, k: (m, k)),
                  pl.BlockSpec((tk, df), lambda b, m, k: (off + b * kb + k, 0)),
                  pl.BlockSpec((tk, df), lambda b, m, k: (off + b * kb + k, 1)),
                  pl.BlockSpec(hw.shape, lambda b, m, k: (0, 0, 0)),
                  pl.BlockSpec((1, df), lambda b, m, k: (0, 0))],
        out_specs=pl.BlockSpec((tm, df), lambda b, m, k: (b * mb + m, 0)),
        out_shape=jax.ShapeDtypeStruct((bsz * length, df), BF16),
        scratch_shapes=[pltpu.VMEM((tm, df), F32)],
        compiler_params=_cp(("parallel", "parallel", "arbitrary")),
        name="row_dft",
    )(cl, sl_neg, ab, ab, hw, gain)


def _s5_tables(lam_re, lam_im, log_dt, b_re, b_im, c_re, c_im, d_skip):
    tc = SSM_CHUNK
    g, p = lam_re.shape[1:]
    hch = b_re.shape[-1]
    lr, li = lam_re.astype(F32), lam_im.astype(F32)
    dt = jnp.exp(log_dt.astype(F32))[..., None]
    er, ei = lr * dt, li * dt

    def lpow(k):
        m = jnp.exp(er * k)
        return m * jnp.cos(ei * k), m * jnp.sin(ei * k)

    l1r, l1i = lpow(1.0)
    den = lr * lr + li * li
    qr = ((l1r - 1.0) * lr + l1i * li) / den
    qi = (l1i * lr - (l1r - 1.0) * li) / den
    br, bi = b_re.astype(F32), b_im.astype(F32)
    bbr = qr[..., None] * br - qi[..., None] * bi
    bbi = qr[..., None] * bi + qi[..., None] * br
    cr, ci = c_re.astype(F32), c_im.astype(F32)

    ks = jnp.arange(tc + 1, dtype=F32)
    pwr = jnp.stack([lpow(k)[0] for k in range(tc + 1)], axis=-1)
    pwi = jnp.stack([lpow(k)[1] for k in range(tc + 1)], axis=-1)
    del ks

    cbr = jnp.einsum("dgip,dgpj->dgpij", cr, bbr) - jnp.einsum("dgip,dgpj->dgpij", ci, bbi)
    cbi = jnp.einsum("dgip,dgpj->dgpij", cr, bbi) + jnp.einsum("dgip,dgpj->dgpij", ci, bbr)
    klag = (jnp.einsum("dgpij,dgpl->dglij", cbr, pwr[..., :tc])
            - jnp.einsum("dgpij,dgpl->dglij", cbi, pwi[..., :tc]))

    t_in = jnp.arange(tc)[:, None]
    t_out = jnp.arange(tc)[None, :]
    lag_f = t_out - t_in
    lag_b = t_in - t_out
    kf = klag[0][:, jnp.clip(lag_f, 0, tc - 1)] * (lag_f >= 0)[None, :, :, None, None]
    kb = klag[1][:, jnp.clip(lag_b, 0, tc - 1)] * (lag_b >= 0)[None, :, :, None, None]
    eye_t = (t_in == t_out).astype(F32)[None, :, :, None, None]
    dg = d_skip.astype(F32).reshape(g, hch)
    dmat = eye_t * (jnp.eye(hch, dtype=F32) * dg[:, None, :])[:, None, None, :, :]
    m = (kf + kb + dmat).transpose(0, 1, 4, 2, 3).reshape(g, tc * hch, tc * hch)

    idx_f = jnp.arange(tc - 1, -1, -1)
    idx_b = jnp.arange(tc)

    def st(d, idx):
        wr = pwr[d][..., idx][:, :, :, None] * bbr[d][:, :, None, :] - pwi[d][..., idx][:, :, :, None] * bbi[d][:, :, None, :]
        wi = pwr[d][..., idx][:, :, :, None] * bbi[d][:, :, None, :] + pwi[d][..., idx][:, :, :, None] * bbr[d][:, :, None, :]
        return (wr.transpose(0, 2, 3, 1).reshape(g, tc * hch, p),
                wi.transpose(0, 2, 3, 1).reshape(g, tc * hch, p))

    sfr, sfi = st(0, idx_f)
    sbr, sbi = st(1, idx_b)
    w_st = jnp.concatenate([sfr, sbr, sfi, sbi], axis=-1)

    def so(d, idx):
        wr = cr[d][:, None, :, :] * pwr[d][..., idx].transpose(0, 2, 1)[:, :, None, :] \
            - ci[d][:, None, :, :] * pwi[d][..., idx].transpose(0, 2, 1)[:, :, None, :]
        wi = cr[d][:, None, :, :] * pwi[d][..., idx].transpose(0, 2, 1)[:, :, None, :] \
            + ci[d][:, None, :, :] * pwr[d][..., idx].transpose(0, 2, 1)[:, :, None, :]
        return (wr.reshape(g, tc * hch, p).transpose(0, 2, 1),
                wi.reshape(g, tc * hch, p).transpose(0, 2, 1))

    ofr, ofi = so(0, jnp.arange(1, tc + 1))
    obr, obi = so(1, jnp.arange(tc, 0, -1))
    zero = jnp.zeros_like(ofr)
    w_of = jnp.concatenate([ofr, zero, -ofi, zero], axis=1)
    w_ob = jnp.concatenate([zero, obr, zero, -obi], axis=1)

    a_re = jnp.concatenate([pwr[0][..., tc], pwr[1][..., tc]], axis=-1)[:, None, :]
    a_im = jnp.concatenate([pwi[0][..., tc], pwi[1][..., tc]], axis=-1)[:, None, :]
    return w_st.astype(BF16), m.astype(BF16), w_of.astype(BF16), w_ob.astype(BF16), a_re, a_im


def _s5_kernel(u_ref, wst_ref, m_ref, wof_ref, wob_ref, are_ref, aim_ref, y_ref,
               s_ref, hf_ref, hb_ref, *, gb, ctx_pairs, x_pairs, half):
    for j in range(gb):
        s_ref[j] = jnp.dot(u_ref[j], wst_ref[j], preferred_element_type=F32)

    n_pairs = ctx_pairs + x_pairs
    w2 = are_ref.shape[-1]
    sub = lax.broadcasted_iota(jnp.int32, (2 * half, w2), 0)
    lane = lax.broadcasted_iota(jnp.int32, (2 * half, w2), 1)
    lo = sub < half
    fwd = lane < (w2 // 2)
    a_re = [jnp.broadcast_to(are_ref[j], (2 * half, w2)) for j in range(gb)]
    a_im = [jnp.broadcast_to(aim_ref[j], (2 * half, w2)) for j in range(gb)]

    def step(i, carry):
        pf = i
        pb = jnp.where(i < ctx_pairs, ctx_pairs - 1 - i, n_pairs - 1 + ctx_pairs - i)
        rf = pl.multiple_of(pf * (2 * half), 2 * half)
        rb = pl.multiple_of(pb * (2 * half), 2 * half)
        out = []
        for j in range(gb):
            hr, hi = carry[2 * j], carry[2 * j + 1]
            firsts, seconds = [], []
            for part in range(2):
                sf = s_ref[j, pl.ds(rf, 2 * half), part * w2:(part + 1) * w2]
                sb = s_ref[j, pl.ds(rb, 2 * half), part * w2:(part + 1) * w2]
                sf_sw = pltpu.roll(sf, half, 0)
                sb_sw = pltpu.roll(sb, half, 0)
                f_lo = jnp.where(lo, sf, sf_sw)
                f_hi = jnp.where(lo, sf_sw, sf)
                b_lo = jnp.where(lo, sb, sb_sw)
                b_hi = jnp.where(lo, sb_sw, sb)
                firsts.append(jnp.where(fwd, f_lo, b_hi))
                seconds.append(jnp.where(fwd, f_hi, b_lo))
            h1r = a_re[j] * hr - a_im[j] * hi + firsts[0]
            h1i = a_re[j] * hi + a_im[j] * hr + firsts[1]
            h2r = a_re[j] * h1r - a_im[j] * h1i + seconds[0]
            h2i = a_re[j] * h1i + a_im[j] * h1r + seconds[1]
            hf_ref[j, pl.ds(rf, 2 * half), 0:w2] = jnp.where(lo, hr, h1r)
            hf_ref[j, pl.ds(rf, 2 * half), w2:2 * w2] = jnp.where(lo, hi, h1i)
            hb_ref[j, pl.ds(rb, 2 * half), 0:w2] = jnp.where(lo, h1r, hr)
            hb_ref[j, pl.ds(rb, 2 * half), w2:2 * w2] = jnp.where(lo, h1i, hi)
            out += [h2r, h2i]
        return tuple(out)

    zero = jnp.zeros((2 * half, w2), F32)
    lax.fori_loop(0, n_pairs, step, tuple([zero] * (2 * gb)))

    for j in range(gb):
        y = (jnp.dot(u_ref[j], m_ref[j], preferred_element_type=F32)
             + jnp.dot(hf_ref[j].astype(BF16), wof_ref[j], preferred_element_type=F32)
             + jnp.dot(hb_ref[j].astype(BF16), wob_ref[j], preferred_element_type=F32))
        y_ref[j] = y.astype(y_ref.dtype)


def _s5_scan(u, tables, bsz, ctx_chunks, x_chunks):
    w_st, m, w_of, w_ob, a_re, a_im = tables
    g, r, kdim = u.shape
    gb = S5_GROUPS_PER_STEP
    ns = w_st.shape[-1]
    assert 2 * bsz == 8 and ctx_chunks % 2 == 0 and x_chunks % 2 == 0
    kern = functools.partial(_s5_kernel, gb=gb, ctx_pairs=ctx_chunks // 2, x_pairs=x_chunks // 2, half=bsz)
    blk = lambda shp: pl.BlockSpec((gb,) + shp, lambda i: (i, 0, 0))
    return pl.pallas_call(
        kern,
        grid=(g // gb,),
        in_specs=[blk((r, kdim)), blk((kdim, ns)), blk((kdim, kdim)), blk((ns, kdim)), blk((ns, kdim)),
                  blk((1, ns // 2)), blk((1, ns // 2))],
        out_specs=blk((r, kdim)),
        out_shape=jax.ShapeDtypeStruct((g, r, kdim), BF16),
        scratch_shapes=[pltpu.VMEM((gb, r, ns), F32), pltpu.VMEM((gb, r, ns), F32), pltpu.VMEM((gb, r, ns), F32)],
        compiler_params=_cp(("parallel",)),
        name="s5_scan",
    )(u, w_st, m, w_of, w_ob, a_re, a_im)


def _glu_kernel(y_ref, w_ref, b_ref, gain_ref, o_ref):
    y = y_ref[...].astype(F32)
    g = jax.nn.gelu(y)
    v = g * jax.nn.sigmoid(jnp.dot(g.astype(BF16), w_ref[...], preferred_element_type=F32) + b_ref[...])
    r = lax.rsqrt(jnp.mean(v * v, axis=-1, keepdims=True) + EPS)
    o_ref[...] = (v * r * gain_ref[...]).astype(o_ref.dtype)


def _glu(y, w, b, gain):
    nt, ds = y.shape
    tm = 512
    return pl.pallas_call(
        _glu_kernel,
        grid=(nt // tm,),
        in_specs=[pl.BlockSpec((tm, ds), lambda i: (i, 0)),
                  pl.BlockSpec((ds, ds), lambda i: (0, 0)),
                  pl.BlockSpec((1, ds), lambda i: (0, 0)),
                  pl.BlockSpec((1, ds), lambda i: (0, 0))],
        out_specs=pl.BlockSpec((tm, ds), lambda i: (i, 0)),
        out_shape=jax.ShapeDtypeStruct((nt, ds), BF16),
        compiler_params=_cp(("parallel",)),
        name="glu",
    )(y, w, b, gain)


def _route(scores, sel, n_groups):
    epg = len(sel) // n_groups
    gscore = []
    for q in range(n_groups):
        v = sel[q * epg:(q + 1) * epg]
        best = None
        for a in range(epg):
            for b in range(a + 1, epg):
                s = v[a] + v[b]
                best = s if best is None else jnp.maximum(best, s)
        gscore.append(best)
    gbest = gscore[0]
    gidx = jnp.zeros(gbest.shape, jnp.int32)
    for q in range(1, n_groups):
        upd = gscore[q] > gbest
        gbest = jnp.where(upd, gscore[q], gbest)
        gidx = jnp.where(upd, q, gidx)
    vin = list(sel[:epg])
    sin = list(scores[:epg])
    for q in range(1, n_groups):
        pick = gidx == q
        for j in range(epg):
            vin[j] = jnp.where(pick, sel[q * epg + j], vin[j])
            sin[j] = jnp.where(pick, scores[q * epg + j], sin[j])
    b1 = vin[0]
    i1 = jnp.zeros(gbest.shape, jnp.int32)
    for j in range(1, epg):
        upd = vin[j] > b1
        b1 = jnp.where(upd, vin[j], b1)
        i1 = jnp.where(upd, j, i1)
    b2 = vin[0]
    i2 = jnp.zeros(gbest.shape, jnp.int32)
    have = jnp.zeros(gbest.shape, jnp.bool_)
    for j in range(epg):
        cand = i1 != j
        upd = cand & (jnp.logical_not(have) | (vin[j] > b2))
        b2 = jnp.where(upd, vin[j], b2)
        i2 = jnp.where(upd, j, i2)
        have = have | cand
    s1 = sin[0]
    s2 = sin[0]
    for j in range(1, epg):
        s1 = jnp.where(i1 == j, sin[j], s1)
        s2 = jnp.where(i2 == j, sin[j], s2)
    tot = s1 + s2
    return (gidx * epg + i1, gidx * epg + i2), (s1 / tot, s2 / tot)


def _merge_kernel(t_ref, yf_ref, ys_ref, wo_ref, g1_ref, n2_ref, sc2_ref, sh2_ref, rwh_ref, rwl_ref, rb_ref,
                  tn_ref, h3_ref, eidx_ref, ew_ref):
    df = yf_ref.shape[1]
    tm, d = t_ref.shape
    o = (jnp.dot(yf_ref[...], wo_ref[0:df, :], preferred_element_type=F32)
         + jnp.dot(ys_ref[...], wo_ref[df:, :], preferred_element_type=F32))
    tn = t_ref[...] + g1_ref[0] * o
    tn_ref[...] = tn
    h2 = tn * lax.rsqrt(jnp.mean(tn * tn, axis=-1, keepdims=True) + EPS) * n2_ref[...]
    h2 = h2 * (1.0 + sc2_ref[0]) + sh2_ref[0]
    nsub = d // LANES
    for c in range(nsub):
        h3_ref[pl.ds(c, tm, stride=nsub), :] = h2[:, c * LANES:(c + 1) * LANES]
    h_hi = h2.astype(BF16)
    h_lo = (h2 - h_hi.astype(F32)).astype(BF16)
    lg = (jnp.dot(h_hi, rwh_ref[...], preferred_element_type=F32)
          + jnp.dot(h_lo, rwh_ref[...], preferred_element_type=F32)
          + jnp.dot(h_hi, rwl_ref[...], preferred_element_type=F32))
    ne = rb_ref.shape[0]
    logits = lg.T[0:ne, :]
    scores = jax.nn.sigmoid(logits)
    sel = scores + rb_ref[...]
    srows = [scores[e:e + 1, :] for e in range(ne)]
    vrows = [sel[e:e + 1, :] for e in range(ne)]
    (e1, e2), (w1, w2) = _route(srows, vrows, N_EXPERT_GROUPS)
    eidx_ref[...] = jnp.zeros_like(eidx_ref)
    ew_ref[...] = jnp.zeros_like(ew_ref)
    eidx_ref[0:1, :] = e1
    eidx_ref[1:2, :] = e2
    ew_ref[0:1, :] = w1
    ew_ref[1:2, :] = w2


def _merge(t, yf, ys, wo, g1, n2, sc2, sh2, rwh, rwl, rb, n_ctx_rows, seq):
    nt, d = t.shape
    df = yf.shape[1]
    ne = rb.shape[0]
    nsub = d // LANES
    tm = ROW_TILE
    mrow = functools.partial(_mod_row, tm=tm, n_ctx_rows=n_ctx_rows, seq=seq)
    mspec = pl.BlockSpec((1, 1, d), lambda i: (mrow(i), 0, 0))
    return pl.pallas_call(
        _merge_kernel,
        grid=(nt // tm,),
        in_specs=[pl.BlockSpec((tm, d), lambda i: (i, 0)),
                  pl.BlockSpec((tm, df), lambda i: (i, 0)),
                  pl.BlockSpec((tm, d - df), lambda i: (i, 0)),
                  pl.BlockSpec((d, d), lambda i: (0, 0)),
                  mspec,
                  pl.BlockSpec((1, d), lambda i: (0, 0)),
                  mspec, mspec,
                  pl.BlockSpec((d, LANES), lambda i: (0, 0)),
                  pl.BlockSpec((d, LANES), lambda i: (0, 0)),
                  pl.BlockSpec((ne, 1), lambda i: (0, 0))],
        out_specs=[pl.BlockSpec((tm, d), lambda i: (i, 0)),
                   pl.BlockSpec((tm * nsub, LANES), lambda i: (i, 0)),
                   pl.BlockSpec((8, tm), lambda i: (0, i)),
                   pl.BlockSpec((8, tm), lambda i: (0, i))],
        out_shape=[jax.ShapeDtypeStruct((nt, d), F32),
                   jax.ShapeDtypeStruct((nt * nsub, LANES), F32),
                   jax.ShapeDtypeStruct((8, nt), jnp.int32),
                   jax.ShapeDtypeStruct((8, nt), F32)],
        compiler_params=_cp(("parallel",)),
        name="merge_route",
    )(t, yf, ys, wo, g1, n2, sc2, sh2, rwh, rwl, rb)


def _dispatch(eidx, ew, n_experts, rows, n_blocks):
    nt = eidx.shape[1]
    a = nt * TOP_K
    flat_e = eidx[:TOP_K].T.reshape(a)
    onehot = (flat_e[:, None] == jnp.arange(n_experts, dtype=jnp.int32)[None, :]).astype(jnp.int32)
    csum = jnp.cumsum(onehot, axis=0)
    counts = csum[-1]
    padded = (counts + rows - 1) // rows * rows
    pad_end = jnp.cumsum(padded)
    pad_start = pad_end - padded
    dest = jnp.sum(onehot * (csum - 1 + pad_start[None, :]), axis=1)
    tok = jnp.arange(a, dtype=jnp.int32) // TOP_K
    buf_tok = jnp.zeros((n_blocks * rows,), jnp.int32).at[dest].set(tok)
    n_valid = (pad_end[-1] // rows).astype(jnp.int32)
    blk_start = jnp.arange(n_blocks, dtype=jnp.int32) * rows
    blk_exp = jnp.sum((pad_end[None, :] <= blk_start[:, None]).astype(jnp.int32), axis=1)
    blk_exp = jnp.minimum(blk_exp, n_experts - 1)
    last_exp = jnp.sum(jnp.where(jnp.arange(n_blocks) == n_valid - 1, blk_exp, 0))
    blk_exp = jnp.where(jnp.arange(n_blocks) < n_valid, blk_exp, last_exp).astype(jnp.int32)
    return buf_tok, blk_exp, n_valid.reshape(1), dest.reshape(nt, TOP_K), ew[:TOP_K].T


def _expert_kernel(be_ref, nv_ref, tok_ref, h_hbm, wg_ref, wu_ref, wd_ref, o_ref,
                   xg_ref, xb_ref, acc_ref, sem, *, nsub):
    i = pl.program_id(0)
    f = pl.program_id(1)
    nf = pl.num_programs(1)
    rows = xb_ref.shape[0]
    valid = i < nv_ref[0]

    @pl.when(valid & (f == 0))
    def _():
        def issue(r, c):
            src = h_hbm.at[pl.ds(pl.multiple_of(tok_ref[0, r] * nsub, nsub), nsub)]
            dst = xg_ref.at[pl.ds(pl.multiple_of(r * GATHER_PITCH, 8), nsub)]
            pltpu.make_async_copy(src, dst, sem).start()
            return c

        lax.fori_loop(0, rows, issue, 0)
        pltpu.make_async_copy(h_hbm.at[pl.ds(0, rows * nsub)], xg_ref.at[pl.ds(0, rows * nsub)], sem).wait()
        for c in range(nsub):
            xb_ref[:, c * LANES:(c + 1) * LANES] = xg_ref[pl.ds(c, rows, stride=GATHER_PITCH), :].astype(BF16)

    @pl.when(valid)
    def _():
        x = xb_ref[...]
        g = jnp.dot(x, wg_ref[...].astype(BF16), preferred_element_type=F32)
        u = jnp.dot(x, wu_ref[...].astype(BF16), preferred_element_type=F32)
        hmid = (g * jax.nn.sigmoid(g)) * u
        contrib = jnp.dot(hmid.astype(BF16), wd_ref[...].astype(BF16), preferred_element_type=F32)

        @pl.when(f == 0)
        def _():
            acc_ref[...] = contrib

        @pl.when(f > 0)
        def _():
            acc_ref[...] += contrib

        @pl.when(f == nf - 1)
        def _():
            for c in range(nsub):
                o_ref[pl.ds(c, rows, stride=nsub), :] = acc_ref[:, c * LANES:(c + 1) * LANES]

    @pl.when(jnp.logical_not(valid) & (f == nf - 1))
    def _():
        o_ref[...] = jnp.zeros_like(o_ref)


def _experts(h3, buf_tok, blk_exp, n_valid, w_gate, w_up, w_down, layer):
    _, ne, d, de = w_gate.shape
    nsub = d // LANES
    rows, tf = MOE_ROWS, MOE_FTILE
    n_blocks = buf_tok.shape[0] // rows
    nf = de // tf

    def fidx(i, f, nv):
        return jnp.where(i < nv[0], f, nf - 1)

    grid_spec = pltpu.PrefetchScalarGridSpec(
        num_scalar_prefetch=2,
        grid=(n_blocks, nf),
        in_specs=[pl.BlockSpec((None, 1, rows), lambda i, f, be, nv: (i, 0, 0), memory_space=pltpu.SMEM),
                  pl.BlockSpec(memory_space=pl.ANY),
                  pl.BlockSpec((None, None, d, tf), lambda i, f, be, nv: (layer, be[i], 0, fidx(i, f, nv))),
                  pl.BlockSpec((None, None, d, tf), lambda i, f, be, nv: (layer, be[i], 0, fidx(i, f, nv))),
                  pl.BlockSpec((None, None, tf, d), lambda i, f, be, nv: (layer, be[i], fidx(i, f, nv), 0))],
        out_specs=pl.BlockSpec((rows * nsub, LANES), lambda i, f, be, nv: (i, 0)),
        scratch_shapes=[pltpu.VMEM((rows * GATHER_PITCH, LANES), F32),
                        pltpu.VMEM((rows, d), BF16),
                        pltpu.VMEM((rows, d), F32),
                        pltpu.SemaphoreType.DMA],
    )
    return pl.pallas_call(
        functools.partial(_expert_kernel, nsub=nsub),
        grid_spec=grid_spec,
        out_shape=jax.ShapeDtypeStruct((n_blocks * rows * nsub, LANES), F32),
        compiler_params=_cp(("arbitrary", "arbitrary")),
        name="experts",
    )(blk_exp, n_valid, buf_tok.reshape(n_blocks, 1, rows), h3, w_gate, w_up, w_down)


def _combine_kernel(pos_ref, t_ref, w_ref, yb_hbm, g2_ref, o_ref, ga_ref, gb_ref, sem, *, nsub):
    tm = t_ref.shape[0]

    def issue(r, c):
        dst = pl.ds(pl.multiple_of(r * GATHER_PITCH, 8), nsub)
        for k, buf in enumerate((ga_ref, gb_ref)):
            src = pl.ds(pl.multiple_of(pos_ref[0, TOP_K * r + k] * nsub, nsub), nsub)
            pltpu.make_async_copy(yb_hbm.at[src], buf.at[dst], sem).start()
        return c

    lax.fori_loop(0, tm, issue, 0)
    for buf in (ga_ref, gb_ref):
        pltpu.make_async_copy(yb_hbm.at[pl.ds(0, tm * nsub)], buf.at[pl.ds(0, tm * nsub)], sem).wait()
    w0 = w_ref[:, 0:1]
    w1 = w_ref[:, 1:2]
    for c in range(nsub):
        sl = slice(c * LANES, (c + 1) * LANES)
        y = (ga_ref[pl.ds(c, tm, stride=GATHER_PITCH), :] * w0
             + gb_ref[pl.ds(c, tm, stride=GATHER_PITCH), :] * w1)
        o_ref[:, sl] = t_ref[:, sl] + g2_ref[0][:, sl] * y


def _combine(t, yb3, pos, w, g2, n_ctx_rows, seq):
    nt, d = t.shape
    nsub = d // LANES
    tm = ROW_TILE
    mrow = functools.partial(_mod_row, tm=tm, n_ctx_rows=n_ctx_rows, seq=seq)
    return pl.pallas_call(
        functools.partial(_combine_kernel, nsub=nsub),
        grid=(nt // tm,),
        in_specs=[pl.BlockSpec((None, 1, TOP_K * tm), lambda i: (i, 0, 0), memory_space=pltpu.SMEM),
                  pl.BlockSpec((tm, d), lambda i: (i, 0)),
                  pl.BlockSpec((tm, TOP_K), lambda i: (i, 0)),
                  pl.BlockSpec(memory_space=pl.ANY),
                  pl.BlockSpec((1, 1, d), lambda i: (mrow(i), 0, 0))],
        out_specs=pl.BlockSpec((tm, d), lambda i: (i, 0)),
        out_shape=jax.ShapeDtypeStruct((nt, d), F32),
        scratch_shapes=[pltpu.VMEM((tm * GATHER_PITCH, LANES), F32),
                        pltpu.VMEM((tm * GATHER_PITCH, LANES), F32),
                        pltpu.SemaphoreType.DMA],
        compiler_params=_cp(("arbitrary",)),
        name="combine",
    )(pos.reshape(nt // tm, 1, TOP_K * tm), t, w, yb3, g2)


def _final_kernel(x_ref, g_ref, o_ref):
    x = x_ref[...]
    o_ref[...] = x * lax.rsqrt(jnp.mean(x * x, axis=-1, keepdims=True) + EPS) * g_ref[...]


def _final_norm(t, g, row0):
    nt, d = t.shape
    tm = ROW_TILE
    off = row0 // tm
    return pl.pallas_call(
        _final_kernel,
        grid=((nt - row0) // tm,),
        in_specs=[pl.BlockSpec((tm, d), lambda i: (off + i, 0)),
                  pl.BlockSpec((1, d), lambda i: (0, 0))],
        out_specs=pl.BlockSpec((tm, d), lambda i: (i, 0)),
        out_shape=jax.ShapeDtypeStruct((nt - row0, d), F32),
        compiler_params=_cp(("parallel",)),
        name="final_norm",
    )(t, g)


def _to_groups(u, bsz, chunks, g, hch):
    tc = SSM_CHUNK
    return u.reshape(bsz, chunks, tc, g, hch).transpose(3, 1, 0, 2, 4).reshape(g, chunks, bsz, tc * hch)


def _from_groups(y, bsz, chunks, g, hch):
    tc = SSM_CHUNK
    return y.reshape(g, chunks, bsz, tc, hch).transpose(2, 1, 3, 0, 4).reshape(bsz * chunks * tc, g * hch)


def kernel(x, c, ctx, c_ctx, w_mod, b_mod, norm1_g, norm2_g, w_in, w_out, fourier_w, mix_norm_g,
           lam_re, lam_im, log_dt, b_re, b_im, c_re, c_im, d_skip, glu_w, glu_b,
           router_w, router_b, w_gate, w_up, w_down, final_g):
    bsz, seq, d = x.shape
    n_ctx = ctx.shape[1]
    depth = w_mod.shape[0]
    df = fourier_w.shape[1] * fourier_w.shape[2]
    ds = d_skip.shape[1]
    g = lam_re.shape[2]
    hch = ds // g
    ne = router_w.shape[1]
    nc_rows = bsz * n_ctx
    nt = nc_rows + bsz * seq
    tc = SSM_CHUNK
    ctx_chunks, x_chunks = n_ctx // tc, seq // tc

    t = jnp.concatenate([ctx.reshape(nc_rows, d), x.reshape(bsz * seq, d)], axis=0).astype(F32)

    cvec = jnp.concatenate([c_ctx[None, :], c, jnp.zeros((8 - 1 - bsz, d), c.dtype)], axis=0).astype(F32)
    mod = _adaln(cvec, w_mod, b_mod).reshape(depth, 8, N_MOD, 1, d)

    cc, sc_ = _dft_tables(df)
    w_chan = jnp.concatenate([cc, sc_], axis=1).astype(BF16)
    clx, slx = _dft_tables(seq)
    clx, slx = clx.astype(BF16), (-slx).astype(BF16)
    clc, slc = _dft_tables(n_ctx)
    clc, slc = clc.astype(BF16), (-slc).astype(BF16)

    rw = jnp.pad(router_w.astype(F32), ((0, 0), (0, LANES - ne)))
    rwh = rw.astype(BF16)
    rwl = (rw - rwh.astype(F32)).astype(BF16)
    rb = router_b.astype(F32).reshape(ne, 1)
    n_blocks = -(-(nt * TOP_K) // MOE_ROWS) + ne

    for l in range(depth):
        sh1, sc1, g1, sh2, sc2, g2 = [mod[l, :, k] for k in range(N_MOD)]
        z = _inproj(t, norm1_g[l].reshape(1, d).astype(F32), sc1, sh1, w_in[l].astype(BF16), nc_rows, seq)

        ab = _chan_dft(z, w_chan)
        hw = fourier_w[l].astype(BF16)
        gain = mix_norm_g[l].astype(F32).reshape(1, -1)
        yf_c = _row_dft(ab, clc, slc, hw, gain[:, :df], 0, bsz, n_ctx, n_ctx, n_ctx)
        yf_x = _row_dft(ab, clx, slx, hw, gain[:, :df], nc_rows, bsz, seq, 512, 512)
        yf = jnp.concatenate([yf_c, yf_x], axis=0)

        tables = _s5_tables(lam_re[l], lam_im[l], log_dt[l], b_re[l], b_im[l], c_re[l], c_im[l], d_skip[l])
        us = z[:, df:]
        u = jnp.concatenate([_to_groups(us[:nc_rows], bsz, ctx_chunks, g, hch),
                             _to_groups(us[nc_rows:], bsz, x_chunks, g, hch)], axis=1)
        u = u.reshape(g, (ctx_chunks + x_chunks) * bsz, tc * hch)
        ysc = _s5_scan(u, tables, bsz, ctx_chunks, x_chunks).reshape(g, ctx_chunks + x_chunks, bsz, tc * hch)
        ys_tok = jnp.concatenate([_from_groups(ysc[:, :ctx_chunks], bsz, ctx_chunks, g, hch),
                                  _from_groups(ysc[:, ctx_chunks:], bsz, x_chunks, g, hch)], axis=0)
        ys = _glu(ys_tok, glu_w[l].astype(BF16), glu_b[l].astype(F32).reshape(1, ds), gain[:, df:])

        t, h3, eidx, ew = _merge(t, yf, ys, w_out[l].astype(BF16), g1,
                                 norm2_g[l].reshape(1, d).astype(F32), sc2, sh2, rwh, rwl, rb, nc_rows, seq)

        buf_tok, blk_exp, n_valid, pos, wtok = _dispatch(eidx, ew, ne, MOE_ROWS, n_blocks)
        yb = _experts(h3, buf_tok, blk_exp, n_valid, w_gate, w_up, w_down, l)
        t = _combine(t, yb, pos, wtok, g2, nc_rows, seq)

    out = _final_norm(t, final_g.reshape(1, d).astype(F32), nc_rows)
    return out.reshape(bsz, seq, d).astype(x.dtype)
```

```python
import functools
import math

import jax
import jax.numpy as jnp
from jax import lax
from jax.experimental import pallas as pl
from jax.experimental.pallas import tpu as pltpu

F32 = jnp.float32
BF16 = jnp.bfloat16
EPS = 1e-6

FOURIER_HEADS = 4
SSM_GROUP = 16
N_EXPERT_GROUPS = 4
TOP_K = 2
N_MOD = 6

SSM_CHUNK = 16
ROW_TILE = 256
MOE_ROWS = 512
MOE_FTILE = 512
LANES = 128
GATHER_PITCH = 24
S5_SET = LANES
VMEM_LIMIT_BYTES = 56 * 1024 * 1024


def _cp(sems):
    return pltpu.CompilerParams(dimension_semantics=sems, vmem_limit_bytes=VMEM_LIMIT_BYTES)


def _mod_row(i, tm, n_x_rows, seq):
    r = i * tm
    return jnp.where(r < n_x_rows, 1 + r // seq, 0)


def _mod_kernel(c_ref, w_ref, b_ref, o_ref):
    c = c_ref[...]
    s = c * jax.nn.sigmoid(c)
    o_ref[...] = jnp.dot(s.astype(BF16), w_ref[...].astype(BF16),
                         preferred_element_type=F32) + b_ref[...]


def _adaln(cvec, w_mod, b_mod):
    depth, d, n = w_mod.shape
    tn = 1024
    return pl.pallas_call(
        _mod_kernel,
        grid=(depth, n // tn),
        in_specs=[pl.BlockSpec((8, d), lambda l, j: (0, 0)),
                  pl.BlockSpec((None, d, tn), lambda l, j: (l, 0, j)),
                  pl.BlockSpec((None, 1, tn), lambda l, j: (l, 0, j))],
        out_specs=pl.BlockSpec((None, 8, tn), lambda l, j: (l, 0, j)),
        out_shape=jax.ShapeDtypeStruct((depth, 8, n), F32),
        compiler_params=_cp(("parallel", "parallel")),
        name="adaln",
    )(cvec, w_mod, b_mod.reshape(depth, 1, n))


def _inproj_kernel(t_ref, g_ref, sc_ref, sh_ref, w_ref, zf_ref, zs_ref):
    x = t_ref[...]
    h = x * lax.rsqrt(jnp.mean(x * x, axis=-1, keepdims=True) + EPS) * g_ref[...]
    h = h * (1.0 + sc_ref[0]) + sh_ref[0]
    z = jnp.dot(h.astype(BF16), w_ref[...], preferred_element_type=F32)
    df = zf_ref.shape[1]
    zf_ref[...] = z[:, :df].astype(zf_ref.dtype)
    zs_ref[...] = z[:, df:]


def _inproj(t, g, sc, sh, w, df, n_x_rows, seq):
    nt, d = t.shape
    n = w.shape[1]
    tm = ROW_TILE
    mrow = functools.partial(_mod_row, tm=tm, n_x_rows=n_x_rows, seq=seq)
    return pl.pallas_call(
        _inproj_kernel,
        grid=(nt // tm,),
        in_specs=[pl.BlockSpec((tm, d), lambda i: (i, 0)),
                  pl.BlockSpec((1, d), lambda i: (0, 0)),
                  pl.BlockSpec((1, 1, d), lambda i: (mrow(i), 0, 0)),
                  pl.BlockSpec((1, 1, d), lambda i: (mrow(i), 0, 0)),
                  pl.BlockSpec((d, n), lambda i: (0, 0))],
        out_specs=[pl.BlockSpec((tm, df), lambda i: (i, 0)),
                   pl.BlockSpec((tm, n - df), lambda i: (i, 0))],
        out_shape=[jax.ShapeDtypeStruct((nt, df), BF16),
                   jax.ShapeDtypeStruct((nt, n - df), F32)],
        compiler_params=_cp(("parallel",)),
        name="inproj",
    )(t, g, sc, sh, w)


def _dft_tables(n):
    r = 1
    while r * r < n:
        r *= 2
    q = n // r
    k = jnp.arange(n, dtype=jnp.int32)[:, None]
    step = 2.0 * math.pi / n
    pa = ((k * (jnp.arange(q, dtype=jnp.int32)[None, :] * r)) % n).astype(F32) * step
    pb = ((k * jnp.arange(r, dtype=jnp.int32)[None, :]) % n).astype(F32) * step
    ca, sa = jnp.cos(pa)[:, :, None], jnp.sin(pa)[:, :, None]
    cb, sb = jnp.cos(pb)[:, None, :], jnp.sin(pb)[:, None, :]
    c = (ca * cb - sa * sb).reshape(n, n)
    s = (sa * cb + ca * sb).reshape(n, n)
    return c, s


def _chan_dft_kernel(z_ref, w_ref, o_ref):
    o_ref[...] = jnp.dot(z_ref[...], w_ref[...], preferred_element_type=F32).astype(o_ref.dtype)


def _chan_dft(z, w):
    nt = z.shape[0]
    df, n = w.shape
    tm = 512
    return pl.pallas_call(
        _chan_dft_kernel,
        grid=(nt // tm,),
        in_specs=[pl.BlockSpec((tm, df), lambda i: (i, 0)),
                  pl.BlockSpec((df, n), lambda i: (0, 0))],
        out_specs=pl.BlockSpec((tm, n), lambda i: (i, 0)),
        out_shape=jax.ShapeDtypeStruct((nt, n), BF16),
        compiler_params=_cp(("parallel",)),
        name="chan_dft",
    )(z, w)


def _row_dft_kernel(cl_ref, sl_ref, a_ref, b_ref, hw_ref, gain_ref, o_ref, acc_ref, *, scale):
    k = pl.program_id(2)

    @pl.when(k == 0)
    def _():
        acc_ref[...] = jnp.zeros_like(acc_ref)

    acc_ref[...] += (jnp.dot(cl_ref[...], a_ref[...], preferred_element_type=F32)
                     + jnp.dot(sl_ref[...], b_ref[...], preferred_element_type=F32))

    @pl.when(k == pl.num_programs(2) - 1)
    def _():
        f = acc_ref[...] * scale
        nh, hd, _ = hw_ref.shape
        ys = [jnp.dot(f[:, h * hd:(h + 1) * hd].astype(BF16), hw_ref[h],
                      preferred_element_type=F32) for h in range(nh)]
        ss = ys[0] * ys[0]
        ssum = jnp.sum(ss, axis=-1, keepdims=True)
        for h in range(1, nh):
            ssum = ssum + jnp.sum(ys[h] * ys[h], axis=-1, keepdims=True)
        r = lax.rsqrt(ssum / (nh * hd) + EPS)
        for h in range(nh):
            o_ref[:, h * hd:(h + 1) * hd] = (ys[h] * r * gain_ref[:, h * hd:(h + 1) * hd]).astype(o_ref.dtype)


def _row_dft(ab, cl, sl_neg, hw, gain, row0, bsz, length, tm, tk):
    df = ab.shape[1] // 2
    kb = length // tk
    mb = length // tm
    off = row0 // tk
    scale = 1.0 / math.sqrt(length * df)
    return pl.pallas_call(
        functools.partial(_row_dft_kernel, scale=scale),
        grid=(bsz, mb, kb),
        in_specs=[pl.BlockSpec((tm, tk), lambda b, m, k: (m, k)),
                  pl.BlockSpec((tm, tk), lambda b, m, k: (m, k)),
                  pl.BlockSpec((tk, df), lambda b, m, k: (off + b * kb + k, 0)),
                  pl.BlockSpec((tk, df), lambda b, m, k: (off + b * kb + k, 1)),
                  pl.BlockSpec(hw.shape, lambda b, m, k: (0, 0, 0)),
                  pl.BlockSpec((1, df), lambda b, m, k: (0, 0))],
        out_specs=pl.BlockSpec((tm, df), lambda b, m, k: (b * mb + m, 0)),
        out_shape=jax.ShapeDtypeStruct((bsz * length, df), BF16),
        scratch_shapes=[pltpu.VMEM((tm, df), F32)],
        compiler_params=_cp(("parallel", "parallel", "arbitrary")),
        name="row_dft",
    )(cl, sl_neg, ab, ab, hw, gain)


def _s5_tables(lam_re, lam_im, log_dt, b_re, b_im, c_re, c_im, d_skip):
    tc = SSM_CHUNK
    g, p = lam_re.shape[1:]
    hch = b_re.shape[-1]
    lr, li = lam_re.astype(F32), lam_im.astype(F32)
    dt = jnp.exp(log_dt.astype(F32))[..., None]
    er, ei = lr * dt, li * dt

    def lpow(k):
        m = jnp.exp(er * k)
        return m * jnp.cos(ei * k), m * jnp.sin(ei * k)

    l1r, l1i = lpow(1.0)
    den = lr * lr + li * li
    qr = ((l1r - 1.0) * lr + l1i * li) / den
    qi = (l1i * lr - (l1r - 1.0) * li) / den
    br, bi = b_re.astype(F32), b_im.astype(F32)
    bbr = qr[..., None] * br - qi[..., None] * bi
    bbi = qr[..., None] * bi + qi[..., None] * br
    cr, ci = c_re.astype(F32), c_im.astype(F32)

    ks = jnp.arange(tc + 1, dtype=F32)
    pwr = jnp.stack([lpow(k)[0] for k in range(tc + 1)], axis=-1)
    pwi = jnp.stack([lpow(k)[1] for k in range(tc + 1)], axis=-1)
    del ks

    cbr = jnp.einsum("dgip,dgpj->dgpij", cr, bbr) - jnp.einsum("dgip,dgpj->dgpij", ci, bbi)
    cbi = jnp.einsum("dgip,dgpj->dgpij", cr, bbi) + jnp.einsum("dgip,dgpj->dgpij", ci, bbr)
    klag = (jnp.einsum("dgpij,dgpl->dglij", cbr, pwr[..., :tc])
            - jnp.einsum("dgpij,dgpl->dglij", cbi, pwi[..., :tc]))

    t_in = jnp.arange(tc)[:, None]
    t_out = jnp.arange(tc)[None, :]
    lag_f = t_out - t_in
    lag_b = t_in - t_out
    kf = klag[0][:, jnp.clip(lag_f, 0, tc - 1)] * (lag_f >= 0)[None, :, :, None, None]
    kb = klag[1][:, jnp.clip(lag_b, 0, tc - 1)] * (lag_b >= 0)[None, :, :, None, None]
    eye_t = (t_in == t_out).astype(F32)[None, :, :, None, None]
    dg = d_skip.astype(F32).reshape(g, hch)
    dmat = eye_t * (jnp.eye(hch, dtype=F32) * dg[:, None, :])[:, None, None, :, :]
    m = (kf + kb + dmat).transpose(0, 1, 4, 2, 3).reshape(g, tc * hch, tc * hch)

    idx_f = jnp.arange(tc - 1, -1, -1)
    idx_b = jnp.arange(tc)

    def st(d, idx):
        wr = pwr[d][..., idx][:, :, :, None] * bbr[d][:, :, None, :] - pwi[d][..., idx][:, :, :, None] * bbi[d][:, :, None, :]
        wi = pwr[d][..., idx][:, :, :, None] * bbi[d][:, :, None, :] + pwi[d][..., idx][:, :, :, None] * bbr[d][:, :, None, :]
        return (wr.transpose(0, 2, 3, 1).reshape(g, tc * hch, p),
                wi.transpose(0, 2, 3, 1).reshape(g, tc * hch, p))

    sfr, sfi = st(0, idx_f)
    sbr, sbi = st(1, idx_b)
    w_st = jnp.concatenate([sfr, sbr, sfi, sbi], axis=-1)

    def so(d, idx):
        wr = cr[d][:, None, :, :] * pwr[d][..., idx].transpose(0, 2, 1)[:, :, None, :] \
            - ci[d][:, None, :, :] * pwi[d][..., idx].transpose(0, 2, 1)[:, :, None, :]
        wi = cr[d][:, None, :, :] * pwi[d][..., idx].transpose(0, 2, 1)[:, :, None, :] \
            + ci[d][:, None, :, :] * pwr[d][..., idx].transpose(0, 2, 1)[:, :, None, :]
        return (wr.reshape(g, tc * hch, p).transpose(0, 2, 1),
                wi.reshape(g, tc * hch, p).transpose(0, 2, 1))

    ofr, ofi = so(0, jnp.arange(1, tc + 1))
    obr, obi = so(1, jnp.arange(tc, 0, -1))
    zero = jnp.zeros_like(ofr)
    w_of = jnp.concatenate([ofr, zero, -ofi, zero], axis=1)
    w_ob = jnp.concatenate([zero, obr, zero, -obi], axis=1)

    a_re = jnp.concatenate([pwr[0][..., tc], pwr[1][..., tc]], axis=-1)
    a_im = jnp.concatenate([pwi[0][..., tc], pwi[1][..., tc]], axis=-1)

    gs = S5_SET // hch
    ns = g // gs
    eye = jnp.eye(gs, dtype=F32)
    kd = tc * hch
    m_set = jnp.einsum("sgtjui,gh->stgjuhi", m.reshape(ns, gs, tc, hch, tc, hch), eye)
    w_o = w_of + w_ob
    st_set = jnp.einsum("sgtjpl,gh->stgjphl", w_st.reshape(ns, gs, tc, hch, 2, 2 * p), eye)
    o_set = jnp.einsum("sgpluj,gh->spgluhj", w_o.reshape(ns, gs, 2, 2 * p, tc, hch), eye)
    kset = gs * kd
    return (m_set.reshape(ns, kset, kset).astype(BF16),
            st_set.reshape(ns, kset, gs * 4 * p).astype(BF16),
            o_set.reshape(ns, gs * 4 * p, kset).astype(BF16),
            a_re.reshape(ns, gs, 2 * p), a_im.reshape(ns, gs, 2 * p))


def _s5_kernel(zx_ref, zc_ref, m_ref, wst_ref, wo_ref, are_ref, aim_ref, ox_ref, oc_ref,
               sh_ref, xb_ref, hb_ref, *, ctx_chunks, x_chunks, pitch):
    tc = SSM_CHUNK
    nch = ctx_chunks + x_chunks
    lanes = zx_ref.shape[1]
    for t in range(tc):
        xb_ref[0:ctx_chunks, t * lanes:(t + 1) * lanes] = zc_ref[pl.ds(t, ctx_chunks, stride=tc), :].astype(BF16)
        xb_ref[ctx_chunks:nch, t * lanes:(t + 1) * lanes] = zx_ref[pl.ds(t, x_chunks, stride=tc), :].astype(BF16)

    s = jnp.dot(xb_ref[...], wst_ref[...], preferred_element_type=F32)
    nslab = s.shape[1] // lanes
    for k in range(nslab):
        sh_ref[k * pitch:k * pitch + nch, :] = s[:, k * lanes:(k + 1) * lanes]

    half = nslab // 2
    fwd = lax.broadcasted_iota(jnp.int32, (half, lanes), 1) < (lanes // 2)
    a_re = are_ref[...]
    a_im = aim_ref[...]

    def rows(c, part):
        return pl.ds(part * half * pitch + c, half, stride=pitch)

    def step(i, carry):
        hr, hi = carry
        cf = i
        cb = jnp.where(i < ctx_chunks, ctx_chunks - 1 - i, nch - 1 + ctx_chunks - i)
        same = cf == cb
        sfr, sfi = sh_ref[rows(cf, 0), :], sh_ref[rows(cf, 1), :]
        sbr, sbi = sh_ref[rows(cb, 0), :], sh_ref[rows(cb, 1), :]
        sh_ref[rows(cf, 0), :] = jnp.where(fwd, hr, sfr)
        sh_ref[rows(cf, 1), :] = jnp.where(fwd, hi, sfi)
        sh_ref[rows(cb, 0), :] = jnp.where(fwd, jnp.where(same, hr, sbr), hr)
        sh_ref[rows(cb, 1), :] = jnp.where(fwd, jnp.where(same, hi, sbi), hi)
        sr = jnp.where(fwd, sfr, sbr)
        si = jnp.where(fwd, sfi, sbi)
        return a_re * hr - a_im * hi + sr, a_re * hi + a_im * hr + si

    zero = jnp.zeros((half, lanes), F32)
    lax.fori_loop(0, nch, step, (zero, zero))

    for k in range(nslab):
        hb_ref[:, k * lanes:(k + 1) * lanes] = sh_ref[k * pitch:k * pitch + nch, :].astype(BF16)
    y = (jnp.dot(xb_ref[...], m_ref[...], preferred_element_type=F32)
         + jnp.dot(hb_ref[...], wo_ref[...], preferred_element_type=F32))
    for t in range(tc):
        oc_ref[pl.ds(t, ctx_chunks, stride=tc), :] = y[0:ctx_chunks, t * lanes:(t + 1) * lanes]
        ox_ref[pl.ds(t, x_chunks, stride=tc), :] = y[ctx_chunks:nch, t * lanes:(t + 1) * lanes]


def _s5_scan(zs, tables, bsz, seq, n_ctx):
    m_set, st_set, o_set, a_re, a_im = tables
    ns, kset, nstate = st_set.shape
    tc = SSM_CHUNK
    lanes = S5_SET
    ctx_chunks, x_chunks = n_ctx // tc, seq // tc
    nch = ctx_chunks + x_chunks
    pitch = -(-nch // 8) * 8
    if (pitch // 8) % 2 == 0:
        pitch += 8
    ctx0 = bsz * seq // n_ctx
    once = pl.Buffered(1)
    kern = functools.partial(_s5_kernel, ctx_chunks=ctx_chunks, x_chunks=x_chunks, pitch=pitch)
    return pl.pallas_call(
        kern,
        grid=(ns, bsz),
        in_specs=[pl.BlockSpec((seq, lanes), lambda s, b: (b, s)),
                  pl.BlockSpec((n_ctx, lanes), lambda s, b: (ctx0 + b, s)),
                  pl.BlockSpec((None, kset, kset), lambda s, b: (s, 0, 0), pipeline_mode=once),
                  pl.BlockSpec((None, kset, nstate), lambda s, b: (s, 0, 0), pipeline_mode=once),
                  pl.BlockSpec((None, nstate, kset), lambda s, b: (s, 0, 0), pipeline_mode=once),
                  pl.BlockSpec((None,) + a_re.shape[1:], lambda s, b: (s, 0, 0)),
                  pl.BlockSpec((None,) + a_im.shape[1:], lambda s, b: (s, 0, 0))],
        out_specs=[pl.BlockSpec((seq, lanes), lambda s, b: (b, s)),
                   pl.BlockSpec((n_ctx, lanes), lambda s, b: (b, s))],
        out_shape=[jax.ShapeDtypeStruct((bsz * seq, zs.shape[1]), F32),
                   jax.ShapeDtypeStruct((bsz * n_ctx, zs.shape[1]), F32)],
        scratch_shapes=[pltpu.VMEM((nstate // lanes * pitch, lanes), F32),
                        pltpu.VMEM((nch, kset), BF16),
                        pltpu.VMEM((nch, nstate), BF16)],
        compiler_params=_cp(("parallel", "parallel")),
        name="s5_scan",
    )(zs, zs, m_set, st_set, o_set, a_re, a_im)


def _glu_kernel(yx_ref, yc_ref, w_ref, b_ref, gain_ref, o_ref, *, nx_tiles):
    y = jnp.where(pl.program_id(0) < nx_tiles, yx_ref[...], yc_ref[...])
    g = jax.nn.gelu(y)
    v = g * jax.nn.sigmoid(jnp.dot(g.astype(BF16), w_ref[...], preferred_element_type=F32) + b_ref[...])
    r = lax.rsqrt(jnp.mean(v * v, axis=-1, keepdims=True) + EPS)
    o_ref[...] = (v * r * gain_ref[...]).astype(o_ref.dtype)


def _glu(yx, yc, w, b, gain):
    ds = yx.shape[1]
    tm = 512
    nx_tiles, nc_tiles = yx.shape[0] // tm, yc.shape[0] // tm
    return pl.pallas_call(
        functools.partial(_glu_kernel, nx_tiles=nx_tiles),
        grid=(nx_tiles + nc_tiles,),
        in_specs=[pl.BlockSpec((tm, ds), lambda i: (jnp.minimum(i, nx_tiles - 1), 0)),
                  pl.BlockSpec((tm, ds), lambda i: (jnp.maximum(i - nx_tiles, 0), 0)),
                  pl.BlockSpec((ds, ds), lambda i: (0, 0)),
                  pl.BlockSpec((1, ds), lambda i: (0, 0)),
                  pl.BlockSpec((1, ds), lambda i: (0, 0))],
        out_specs=pl.BlockSpec((tm, ds), lambda i: (i, 0)),
        out_shape=jax.ShapeDtypeStruct((yx.shape[0] + yc.shape[0], ds), BF16),
        compiler_params=_cp(("parallel",)),
        name="glu",
    )(yx, yc, w, b, gain)


def _route(scores, sel, n_groups):
    epg = len(sel) // n_groups
    gscore = []
    for q in range(n_groups):
        v = sel[q * epg:(q + 1) * epg]
        best = None
        for a in range(epg):
            for b in range(a + 1, epg):
                s = v[a] + v[b]
                best = s if best is None else jnp.maximum(best, s)
        gscore.append(best)
    gbest = gscore[0]
    gidx = jnp.zeros(gbest.shape, jnp.int32)
    for q in range(1, n_groups):
        upd = gscore[q] > gbest
        gbest = jnp.where(upd, gscore[q], gbest)
        gidx = jnp.where(upd, q, gidx)
    vin = list(sel[:epg])
    sin = list(scores[:epg])
    for q in range(1, n_groups):
        pick = gidx == q
        for j in range(epg):
            vin[j] = jnp.where(pick, sel[q * epg + j], vin[j])
            sin[j] = jnp.where(pick, scores[q * epg + j], sin[j])
    b1 = vin[0]
    i1 = jnp.zeros(gbest.shape, jnp.int32)
    for j in range(1, epg):
        upd = vin[j] > b1
        b1 = jnp.where(upd, vin[j], b1)
        i1 = jnp.where(upd, j, i1)
    b2 = vin[0]
    i2 = jnp.zeros(gbest.shape, jnp.int32)
    have = jnp.zeros(gbest.shape, jnp.bool_)
    for j in range(epg):
        cand = i1 != j
        upd = cand & (jnp.logical_not(have) | (vin[j] > b2))
        b2 = jnp.where(upd, vin[j], b2)
        i2 = jnp.where(upd, j, i2)
        have = have | cand
    s1 = sin[0]
    s2 = sin[0]
    for j in range(1, epg):
        s1 = jnp.where(i1 == j, sin[j], s1)
        s2 = jnp.where(i2 == j, sin[j], s2)
    tot = s1 + s2
    return (gidx * epg + i1, gidx * epg + i2), (s1 / tot, s2 / tot)


def _merge_kernel(t_ref, yf_ref, ys_ref, wo_ref, g1_ref, n2_ref, sc2_ref, sh2_ref, rwh_ref, rwl_ref, rb_ref,
                  tn_ref, h3_ref, eidx_ref, ew_ref):
    df = yf_ref.shape[1]
    tm, d = t_ref.shape
    o = (jnp.dot(yf_ref[...], wo_ref[0:df, :], preferred_element_type=F32)
         + jnp.dot(ys_ref[...], wo_ref[df:, :], preferred_element_type=F32))
    tn = t_ref[...] + g1_ref[0] * o
    tn_ref[...] = tn
    h2 = tn * lax.rsqrt(jnp.mean(tn * tn, axis=-1, keepdims=True) + EPS) * n2_ref[...]
    h2 = h2 * (1.0 + sc2_ref[0]) + sh2_ref[0]
    nsub = d // LANES
    for c in range(nsub):
        h3_ref[pl.ds(c, tm, stride=nsub), :] = h2[:, c * LANES:(c + 1) * LANES]
    h_hi = h2.astype(BF16)
    h_lo = (h2 - h_hi.astype(F32)).astype(BF16)
    lg = (jnp.dot(h_hi, rwh_ref[...], preferred_element_type=F32)
          + jnp.dot(h_lo, rwh_ref[...], preferred_element_type=F32)
          + jnp.dot(h_hi, rwl_ref[...], preferred_element_type=F32))
    ne = rb_ref.shape[0]
    logits = lg.T[0:ne, :]
    scores = jax.nn.sigmoid(logits)
    sel = scores + rb_ref[...]
    srows = [scores[e:e + 1, :] for e in range(ne)]
    vrows = [sel[e:e + 1, :] for e in range(ne)]
    (e1, e2), (w1, w2) = _route(srows, vrows, N_EXPERT_GROUPS)
    eidx_ref[...] = jnp.zeros_like(eidx_ref)
    ew_ref[...] = jnp.zeros_like(ew_ref)
    eidx_ref[0:1, :] = e1
    eidx_ref[1:2, :] = e2
    ew_ref[0:1, :] = w1
    ew_ref[1:2, :] = w2


def _merge(t, yf, ys, wo, g1, n2, sc2, sh2, rwh, rwl, rb, n_x_rows, seq):
    nt, d = t.shape
    df = yf.shape[1]
    ne = rb.shape[0]
    nsub = d // LANES
    tm = ROW_TILE
    mrow = functools.partial(_mod_row, tm=tm, n_x_rows=n_x_rows, seq=seq)
    mspec = pl.BlockSpec((1, 1, d), lambda i: (mrow(i), 0, 0))
    return pl.pallas_call(
        _merge_kernel,
        grid=(nt // tm,),
        in_specs=[pl.BlockSpec((tm, d), lambda i: (i, 0)),
                  pl.BlockSpec((tm, df), lambda i: (i, 0)),
                  pl.BlockSpec((tm, d - df), lambda i: (i, 0)),
                  pl.BlockSpec((d, d), lambda i: (0, 0)),
                  mspec,
                  pl.BlockSpec((1, d), lambda i: (0, 0)),
                  mspec, mspec,
                  pl.BlockSpec((d, LANES), lambda i: (0, 0)),
                  pl.BlockSpec((d, LANES), lambda i: (0, 0)),
                  pl.BlockSpec((ne, 1), lambda i: (0, 0))],
        out_specs=[pl.BlockSpec((tm, d), lambda i: (i, 0)),
                   pl.BlockSpec((tm * nsub, LANES), lambda i: (i, 0)),
                   pl.BlockSpec((8, tm), lambda i: (0, i)),
                   pl.BlockSpec((8, tm), lambda i: (0, i))],
        out_shape=[jax.ShapeDtypeStruct((nt, d), F32),
                   jax.ShapeDtypeStruct((nt * nsub, LANES), F32),
                   jax.ShapeDtypeStruct((8, nt), jnp.int32),
                   jax.ShapeDtypeStruct((8, nt), F32)],
        compiler_params=_cp(("parallel",)),
        name="merge_route",
    )(t, yf, ys, wo, g1, n2, sc2, sh2, rwh, rwl, rb)


def _dispatch(eidx, ew, n_experts, rows, n_blocks):
    nt = eidx.shape[1]
    a = nt * TOP_K
    flat_e = eidx[:TOP_K].T.reshape(a)
    onehot = (flat_e[:, None] == jnp.arange(n_experts, dtype=jnp.int32)[None, :]).astype(jnp.int32)
    csum = jnp.cumsum(onehot, axis=0)
    counts = csum[-1]
    padded = (counts + rows - 1) // rows * rows
    pad_end = jnp.cumsum(padded)
    pad_start = pad_end - padded
    dest = jnp.sum(onehot * (csum - 1 + pad_start[None, :]), axis=1)
    tok = jnp.arange(a, dtype=jnp.int32) // TOP_K
    buf_tok = jnp.zeros((n_blocks * rows,), jnp.int32).at[dest].set(tok)
    n_valid = (pad_end[-1] // rows).astype(jnp.int32)
    blk_start = jnp.arange(n_blocks, dtype=jnp.int32) * rows
    blk_exp = jnp.sum((pad_end[None, :] <= blk_start[:, None]).astype(jnp.int32), axis=1)
    blk_exp = jnp.minimum(blk_exp, n_experts - 1)
    last_exp = jnp.sum(jnp.where(jnp.arange(n_blocks) == n_valid - 1, blk_exp, 0))
    blk_exp = jnp.where(jnp.arange(n_blocks) < n_valid, blk_exp, last_exp).astype(jnp.int32)
    return buf_tok, blk_exp, n_valid.reshape(1), dest.reshape(nt, TOP_K), ew[:TOP_K].T


def _expert_kernel(be_ref, nv_ref, tok_ref, h_hbm, wg_ref, wu_ref, wd_ref, o_ref,
                   xg_ref, xb_ref, acc_ref, sem, *, nsub):
    i = pl.program_id(0)
    f = pl.program_id(1)
    nf = pl.num_programs(1)
    rows = xb_ref.shape[0]
    valid = i < nv_ref[0]

    @pl.when(valid & (f == 0))
    def _():
        def issue(r, c):
            src = h_hbm.at[pl.ds(pl.multiple_of(tok_ref[0, r] * nsub, nsub), nsub)]
            dst = xg_ref.at[pl.ds(pl.multiple_of(r * GATHER_PITCH, 8), nsub)]
            pltpu.make_async_copy(src, dst, sem).start()
            return c

        lax.fori_loop(0, rows, issue, 0)
        pltpu.make_async_copy(h_hbm.at[pl.ds(0, rows * nsub)], xg_ref.at[pl.ds(0, rows * nsub)], sem).wait()
        for c in range(nsub):
            xb_ref[:, c * LANES:(c + 1) * LANES] = xg_ref[pl.ds(c, rows, stride=GATHER_PITCH), :].astype(BF16)

    @pl.when(valid)
    def _():
        x = xb_ref[...]
        g = jnp.dot(x, wg_ref[...].astype(BF16), preferred_element_type=F32)
        u = jnp.dot(x, wu_ref[...].astype(BF16), preferred_element_type=F32)
        hmid = (g * jax.nn.sigmoid(g)) * u
        contrib = jnp.dot(hmid.astype(BF16), wd_ref[...].astype(BF16), preferred_element_type=F32)

        @pl.when(f == 0)
        def _():
            acc_ref[...] = contrib

        @pl.when(f > 0)
        def _():
            acc_ref[...] += contrib

        @pl.when(f == nf - 1)
        def _():
            for c in range(nsub):
                o_ref[pl.ds(c, rows, stride=nsub), :] = acc_ref[:, c * LANES:(c + 1) * LANES]

    @pl.when(jnp.logical_not(valid) & (f == nf - 1))
    def _():
        o_ref[...] = jnp.zeros_like(o_ref)


def _experts(h3, buf_tok, blk_exp, n_valid, w_gate, w_up, w_down, layer):
    _, ne, d, de = w_gate.shape
    nsub = d // LANES
    rows, tf = MOE_ROWS, MOE_FTILE
    n_blocks = buf_tok.shape[0] // rows
    nf = de // tf

    def fidx(i, f, nv):
        return jnp.where(i < nv[0], f, nf - 1)

    grid_spec = pltpu.PrefetchScalarGridSpec(
        num_scalar_prefetch=2,
        grid=(n_blocks, nf),
        in_specs=[pl.BlockSpec((None, 1, rows), lambda i, f, be, nv: (i, 0, 0), memory_space=pltpu.SMEM),
                  pl.BlockSpec(memory_space=pl.ANY),
                  pl.BlockSpec((None, None, d, tf), lambda i, f, be, nv: (layer, be[i], 0, fidx(i, f, nv))),
                  pl.BlockSpec((None, None, d, tf), lambda i, f, be, nv: (layer, be[i], 0, fidx(i, f, nv))),
                  pl.BlockSpec((None, None, tf, d), lambda i, f, be, nv: (layer, be[i], fidx(i, f, nv), 0))],
        out_specs=pl.BlockSpec((rows * nsub, LANES), lambda i, f, be, nv: (i, 0)),
        scratch_shapes=[pltpu.VMEM((rows * GATHER_PITCH, LANES), F32),
                        pltpu.VMEM((rows, d), BF16),
                        pltpu.VMEM((rows, d), F32),
                        pltpu.SemaphoreType.DMA],
    )
    return pl.pallas_call(
        functools.partial(_expert_kernel, nsub=nsub),
        grid_spec=grid_spec,
        out_shape=jax.ShapeDtypeStruct((n_blocks * rows * nsub, LANES), F32),
        compiler_params=_cp(("arbitrary", "arbitrary")),
        name="experts",
    )(blk_exp, n_valid, buf_tok.reshape(n_blocks, 1, rows), h3, w_gate, w_up, w_down)


def _combine_kernel(pos_ref, t_ref, w_ref, yb_hbm, g2_ref, o_ref, ga_ref, gb_ref, sem, *, nsub):
    tm = t_ref.shape[0]

    def issue(r, c):
        dst = pl.ds(pl.multiple_of(r * GATHER_PITCH, 8), nsub)
        for k, buf in enumerate((ga_ref, gb_ref)):
            src = pl.ds(pl.multiple_of(pos_ref[0, TOP_K * r + k] * nsub, nsub), nsub)
            pltpu.make_async_copy(yb_hbm.at[src], buf.at[dst], sem).start()
        return c

    lax.fori_loop(0, tm, issue, 0)
    for buf in (ga_ref, gb_ref):
        pltpu.make_async_copy(yb_hbm.at[pl.ds(0, tm * nsub)], buf.at[pl.ds(0, tm * nsub)], sem).wait()
    w0 = w_ref[:, 0:1]
    w1 = w_ref[:, 1:2]
    for c in range(nsub):
        sl = slice(c * LANES, (c + 1) * LANES)
        y = (ga_ref[pl.ds(c, tm, stride=GATHER_PITCH), :] * w0
             + gb_ref[pl.ds(c, tm, stride=GATHER_PITCH), :] * w1)
        o_ref[:, sl] = t_ref[:, sl] + g2_ref[0][:, sl] * y


def _combine(t, yb3, pos, w, g2, n_x_rows, seq):
    nt, d = t.shape
    nsub = d // LANES
    tm = ROW_TILE
    mrow = functools.partial(_mod_row, tm=tm, n_x_rows=n_x_rows, seq=seq)
    return pl.pallas_call(
        functools.partial(_combine_kernel, nsub=nsub),
        grid=(nt // tm,),
        in_specs=[pl.BlockSpec((None, 1, TOP_K * tm), lambda i: (i, 0, 0), memory_space=pltpu.SMEM),
                  pl.BlockSpec((tm, d), lambda i: (i, 0)),
                  pl.BlockSpec((tm, TOP_K), lambda i: (i, 0)),
                  pl.BlockSpec(memory_space=pl.ANY),
                  pl.BlockSpec((1, 1, d), lambda i: (mrow(i), 0, 0))],
        out_specs=pl.BlockSpec((tm, d), lambda i: (i, 0)),
        out_shape=jax.ShapeDtypeStruct((nt, d), F32),
        scratch_shapes=[pltpu.VMEM((tm * GATHER_PITCH, LANES), F32),
                        pltpu.VMEM((tm * GATHER_PITCH, LANES), F32),
                        pltpu.SemaphoreType.DMA],
        compiler_params=_cp(("arbitrary",)),
        name="combine",
    )(pos.reshape(nt // tm, 1, TOP_K * tm), t, w, yb3, g2)


def _final_kernel(x_ref, g_ref, o_ref):
    x = x_ref[...]
    o_ref[...] = x * lax.rsqrt(jnp.mean(x * x, axis=-1, keepdims=True) + EPS) * g_ref[...]


def _final_norm(t, g, n_rows):
    d = t.shape[1]
    tm = ROW_TILE
    return pl.pallas_call(
        _final_kernel,
        grid=(n_rows // tm,),
        in_specs=[pl.BlockSpec((tm, d), lambda i: (i, 0)),
                  pl.BlockSpec((1, d), lambda i: (0, 0))],
        out_specs=pl.BlockSpec((tm, d), lambda i: (i, 0)),
        out_shape=jax.ShapeDtypeStruct((n_rows, d), F32),
        compiler_params=_cp(("parallel",)),
        name="final_norm",
    )(t, g)


def kernel(x, c, ctx, c_ctx, w_mod, b_mod, norm1_g, norm2_g, w_in, w_out, fourier_w, mix_norm_g,
           lam_re, lam_im, log_dt, b_re, b_im, c_re, c_im, d_skip, glu_w, glu_b,
           router_w, router_b, w_gate, w_up, w_down, final_g):
    bsz, seq, d = x.shape
    n_ctx = ctx.shape[1]
    depth = w_mod.shape[0]
    df = fourier_w.shape[1] * fourier_w.shape[2]
    ds = d_skip.shape[1]
    ne = router_w.shape[1]
    nx_rows = bsz * seq
    nt = nx_rows + bsz * n_ctx

    t = jnp.concatenate([x.reshape(nx_rows, d), ctx.reshape(bsz * n_ctx, d)], axis=0).astype(F32)

    cvec = jnp.concatenate([c_ctx[None, :], c, jnp.zeros((8 - 1 - bsz, d), c.dtype)], axis=0).astype(F32)
    mod = _adaln(cvec, w_mod, b_mod).reshape(depth, 8, N_MOD, 1, d)

    cc, sc_ = _dft_tables(df)
    w_chan = jnp.concatenate([cc, sc_], axis=1).astype(BF16)
    clx, slx = _dft_tables(seq)
    clx, slx = clx.astype(BF16), (-slx).astype(BF16)
    clc, slc = _dft_tables(n_ctx)
    clc, slc = clc.astype(BF16), (-slc).astype(BF16)

    rw = jnp.pad(router_w.astype(F32), ((0, 0), (0, LANES - ne)))
    rwh = rw.astype(BF16)
    rwl = (rw - rwh.astype(F32)).astype(BF16)
    rb = router_b.astype(F32).reshape(ne, 1)
    n_blocks = -(-(nt * TOP_K) // MOE_ROWS) + ne
    wg16, wu16, wd16 = w_gate.astype(BF16), w_up.astype(BF16), w_down.astype(BF16)

    for l in range(depth):
        sh1, sc1, g1, sh2, sc2, g2 = [mod[l, :, k] for k in range(N_MOD)]
        zf, zs = _inproj(t, norm1_g[l].reshape(1, d).astype(F32), sc1, sh1, w_in[l].astype(BF16),
                         df, nx_rows, seq)

        ab = _chan_dft(zf, w_chan)
        hw = fourier_w[l].astype(BF16)
        gain = mix_norm_g[l].astype(F32).reshape(1, -1)
        yf_x = _row_dft(ab, clx, slx, hw, gain[:, :df], 0, bsz, seq, 512, 512)
        yf_c = _row_dft(ab, clc, slc, hw, gain[:, :df], nx_rows, bsz, n_ctx, n_ctx, n_ctx)
        yf = jnp.concatenate([yf_x, yf_c], axis=0)

        tables = _s5_tables(lam_re[l], lam_im[l], log_dt[l], b_re[l], b_im[l], c_re[l], c_im[l], d_skip[l])
        ysx, ysc = _s5_scan(zs, tables, bsz, seq, n_ctx)
        ys = _glu(ysx, ysc, glu_w[l].astype(BF16), glu_b[l].astype(F32).reshape(1, ds), gain[:, df:])

        t, h3, eidx, ew = _merge(t, yf, ys, w_out[l].astype(BF16), g1,
                                 norm2_g[l].reshape(1, d).astype(F32), sc2, sh2, rwh, rwl, rb, nx_rows, seq)

        buf_tok, blk_exp, n_valid, pos, wtok = _dispatch(eidx, ew, ne, MOE_ROWS, n_blocks)
        yb = _experts(h3, buf_tok, blk_exp, n_valid, wg16, wu16, wd16, l)
        t = _combine(t, yb, pos, wtok, g2, nx_rows, seq)

    out = _final_norm(t, final_g.reshape(1, d).astype(F32), nx_rows)
    return out.reshape(bsz, seq, d).astype(x.dtype)
```

```python
import functools
import math

import jax
import jax.numpy as jnp
from jax import lax
from jax.experimental import pallas as pl
from jax.experimental.pallas import tpu as pltpu

F32 = jnp.float32
BF16 = jnp.bfloat16
EPS = 1e-6

FOURIER_HEADS = 4
SSM_GROUP = 16
N_EXPERT_GROUPS = 4
TOP_K = 2
N_MOD = 6

SSM_CHUNK = 16
ROW_TILE = 256
MOE_ROWS = 512
MOE_FTILE = 256
LANES = 128
GATHER_PITCH = 24
S5_SET = LANES
VMEM_LIMIT_BYTES = 56 * 1024 * 1024


def _cp(sems):
    return pltpu.CompilerParams(dimension_semantics=sems, vmem_limit_bytes=VMEM_LIMIT_BYTES)


def _mod_row(i, tm, n_x_rows, seq):
    r = i * tm
    return jnp.where(r < n_x_rows, 1 + r // seq, 0)


def _mod_kernel(c_ref, w_ref, b_ref, o_ref):
    c = c_ref[...]
    s = c * jax.nn.sigmoid(c)
    o_ref[...] = jnp.dot(s.astype(BF16), w_ref[...].astype(BF16),
                         preferred_element_type=F32) + b_ref[...]


def _adaln(cvec, w_mod, b_mod):
    depth, d, n = w_mod.shape
    tn = 1024
    return pl.pallas_call(
        _mod_kernel,
        grid=(depth, n // tn),
        in_specs=[pl.BlockSpec((8, d), lambda l, j: (0, 0)),
                  pl.BlockSpec((None, d, tn), lambda l, j: (l, 0, j)),
                  pl.BlockSpec((None, 1, tn), lambda l, j: (l, 0, j))],
        out_specs=pl.BlockSpec((None, 8, tn), lambda l, j: (l, 0, j)),
        out_shape=jax.ShapeDtypeStruct((depth, 8, n), F32),
        compiler_params=_cp(("parallel", "parallel")),
        name="adaln",
    )(cvec, w_mod, b_mod.reshape(depth, 1, n))


def _inproj_kernel(t_ref, g_ref, sc_ref, sh_ref, w_ref, zf_ref, zs_ref):
    x = t_ref[...]
    h = x * lax.rsqrt(jnp.mean(x * x, axis=-1, keepdims=True) + EPS) * g_ref[...]
    h = h * (1.0 + sc_ref[0]) + sh_ref[0]
    z = jnp.dot(h.astype(BF16), w_ref[...], preferred_element_type=F32)
    df = zf_ref.shape[1]
    zf_ref[...] = z[:, :df].astype(zf_ref.dtype)
    zs_ref[...] = z[:, df:]


def _inproj(t, g, sc, sh, w, df, n_x_rows, seq):
    nt, d = t.shape
    n = w.shape[1]
    tm = ROW_TILE
    mrow = functools.partial(_mod_row, tm=tm, n_x_rows=n_x_rows, seq=seq)
    return pl.pallas_call(
        _inproj_kernel,
        grid=(nt // tm,),
        in_specs=[pl.BlockSpec((tm, d), lambda i: (i, 0)),
                  pl.BlockSpec((1, d), lambda i: (0, 0)),
                  pl.BlockSpec((1, 1, d), lambda i: (mrow(i), 0, 0)),
                  pl.BlockSpec((1, 1, d), lambda i: (mrow(i), 0, 0)),
                  pl.BlockSpec((d, n), lambda i: (0, 0))],
        out_specs=[pl.BlockSpec((tm, df), lambda i: (i, 0)),
                   pl.BlockSpec((tm, n - df), lambda i: (i, 0))],
        out_shape=[jax.ShapeDtypeStruct((nt, df), BF16),
                   jax.ShapeDtypeStruct((nt, n - df), F32)],
        compiler_params=_cp(("parallel",)),
        name="inproj",
    )(t, g, sc, sh, w)


def _dft_tables(n):
    r = 1
    while r * r < n:
        r *= 2
    q = n // r
    k = jnp.arange(n, dtype=jnp.int32)[:, None]
    step = 2.0 * math.pi / n
    pa = ((k * (jnp.arange(q, dtype=jnp.int32)[None, :] * r)) % n).astype(F32) * step
    pb = ((k * jnp.arange(r, dtype=jnp.int32)[None, :]) % n).astype(F32) * step
    ca, sa = jnp.cos(pa)[:, :, None], jnp.sin(pa)[:, :, None]
    cb, sb = jnp.cos(pb)[:, None, :], jnp.sin(pb)[:, None, :]
    c = (ca * cb - sa * sb).reshape(n, n)
    s = (sa * cb + ca * sb).reshape(n, n)
    return c, s


def _chan_dft_kernel(z_ref, w_ref, o_ref):
    o_ref[...] = jnp.dot(z_ref[...], w_ref[...], preferred_element_type=F32).astype(o_ref.dtype)


def _chan_dft(z, w):
    nt = z.shape[0]
    df, n = w.shape
    tm = 512
    return pl.pallas_call(
        _chan_dft_kernel,
        grid=(nt // tm,),
        in_specs=[pl.BlockSpec((tm, df), lambda i: (i, 0)),
                  pl.BlockSpec((df, n), lambda i: (0, 0))],
        out_specs=pl.BlockSpec((tm, n), lambda i: (i, 0)),
        out_shape=jax.ShapeDtypeStruct((nt, n), BF16),
        compiler_params=_cp(("parallel",)),
        name="chan_dft",
    )(z, w)


def _row_dft_kernel(cl_ref, sl_ref, a_ref, b_ref, hw_ref, gain_ref, o_ref, acc_ref, *, scale):
    k = pl.program_id(2)

    @pl.when(k == 0)
    def _():
        acc_ref[...] = jnp.zeros_like(acc_ref)

    acc_ref[...] += (jnp.dot(cl_ref[...], a_ref[...], preferred_element_type=F32)
                     + jnp.dot(sl_ref[...], b_ref[...], preferred_element_type=F32))

    @pl.when(k == pl.num_programs(2) - 1)
    def _():
        f = acc_ref[...] * scale
        nh, hd, _ = hw_ref.shape
        ys = [jnp.dot(f[:, h * hd:(h + 1) * hd].astype(BF16), hw_ref[h],
                      preferred_element_type=F32) for h in range(nh)]
        ss = ys[0] * ys[0]
        ssum = jnp.sum(ss, axis=-1, keepdims=True)
        for h in range(1, nh):
            ssum = ssum + jnp.sum(ys[h] * ys[h], axis=-1, keepdims=True)
        r = lax.rsqrt(ssum / (nh * hd) + EPS)
        for h in range(nh):
            o_ref[:, h * hd:(h + 1) * hd] = (ys[h] * r * gain_ref[:, h * hd:(h + 1) * hd]).astype(o_ref.dtype)


def _row_dft(ab, cl, sl_neg, hw, gain, row0, bsz, length, tm, tk):
    df = ab.shape[1] // 2
    kb = length // tk
    mb = length // tm
    off = row0 // tk
    scale = 1.0 / math.sqrt(length * df)
    return pl.pallas_call(
        functools.partial(_row_dft_kernel, scale=scale),
        grid=(bsz, mb, kb),
        in_specs=[pl.BlockSpec((tm, tk), lambda b, m, k: (m, k)),
                  pl.BlockSpec((tm, tk), lambda b, m, k: (m, k)),
                  pl.BlockSpec((tk, df), lambda b, m, k: (off + b * kb + k, 0)),
                  pl.BlockSpec((tk, df), lambda b, m, k: (off + b * kb + k, 1)),
                  pl.BlockSpec(hw.shape, lambda b, m, k: (0, 0, 0)),
                  pl.BlockSpec((1, df), lambda b, m, k: (0, 0))],
        out_specs=pl.BlockSpec((tm, df), lambda b, m, k: (b * mb + m, 0)),
        out_shape=jax.ShapeDtypeStruct((bsz * length, df), BF16),
        scratch_shapes=[pltpu.VMEM((tm, df), F32)],
        compiler_params=_cp(("parallel", "parallel", "arbitrary")),
        name="row_dft",
    )(cl, sl_neg, ab, ab, hw, gain)


def _s5_tables(lam_re, lam_im, log_dt, b_re, b_im, c_re, c_im, d_skip):
    tc = SSM_CHUNK
    g, p = lam_re.shape[1:]
    hch = b_re.shape[-1]
    lr, li = lam_re.astype(F32), lam_im.astype(F32)
    dt = jnp.exp(log_dt.astype(F32))[..., None]
    er, ei = lr * dt, li * dt

    def lpow(k):
        m = jnp.exp(er * k)
        return m * jnp.cos(ei * k), m * jnp.sin(ei * k)

    l1r, l1i = lpow(1.0)
    den = lr * lr + li * li
    qr = ((l1r - 1.0) * lr + l1i * li) / den
    qi = (l1i * lr - (l1r - 1.0) * li) / den
    br, bi = b_re.astype(F32), b_im.astype(F32)
    bbr = qr[..., None] * br - qi[..., None] * bi
    bbi = qr[..., None] * bi + qi[..., None] * br
    cr, ci = c_re.astype(F32), c_im.astype(F32)

    ks = jnp.arange(tc + 1, dtype=F32)
    pwr = jnp.stack([lpow(k)[0] for k in range(tc + 1)], axis=-1)
    pwi = jnp.stack([lpow(k)[1] for k in range(tc + 1)], axis=-1)
    del ks

    cbr = jnp.einsum("dgip,dgpj->dgpij", cr, bbr) - jnp.einsum("dgip,dgpj->dgpij", ci, bbi)
    cbi = jnp.einsum("dgip,dgpj->dgpij", cr, bbi) + jnp.einsum("dgip,dgpj->dgpij", ci, bbr)
    klag = (jnp.einsum("dgpij,dgpl->dglij", cbr, pwr[..., :tc])
            - jnp.einsum("dgpij,dgpl->dglij", cbi, pwi[..., :tc]))

    t_in = jnp.arange(tc)[:, None]
    t_out = jnp.arange(tc)[None, :]
    lag_f = t_out - t_in
    lag_b = t_in - t_out
    kf = klag[0][:, jnp.clip(lag_f, 0, tc - 1)] * (lag_f >= 0)[None, :, :, None, None]
    kb = klag[1][:, jnp.clip(lag_b, 0, tc - 1)] * (lag_b >= 0)[None, :, :, None, None]
    eye_t = (t_in == t_out).astype(F32)[None, :, :, None, None]
    dg = d_skip.astype(F32).reshape(g, hch)
    dmat = eye_t * (jnp.eye(hch, dtype=F32) * dg[:, None, :])[:, None, None, :, :]
    m = (kf + kb + dmat).transpose(0, 1, 4, 2, 3).reshape(g, tc * hch, tc * hch)

    idx_f = jnp.arange(tc - 1, -1, -1)
    idx_b = jnp.arange(tc)

    def st(d, idx):
        wr = pwr[d][..., idx][:, :, :, None] * bbr[d][:, :, None, :] - pwi[d][..., idx][:, :, :, None] * bbi[d][:, :, None, :]
        wi = pwr[d][..., idx][:, :, :, None] * bbi[d][:, :, None, :] + pwi[d][..., idx][:, :, :, None] * bbr[d][:, :, None, :]
        return (wr.transpose(0, 2, 3, 1).reshape(g, tc * hch, p),
                wi.transpose(0, 2, 3, 1).reshape(g, tc * hch, p))

    sfr, sfi = st(0, idx_f)
    sbr, sbi = st(1, idx_b)
    w_st = jnp.concatenate([sfr, sbr, sfi, sbi], axis=-1)

    def so(d, idx):
        wr = cr[d][:, None, :, :] * pwr[d][..., idx].transpose(0, 2, 1)[:, :, None, :] \
            - ci[d][:, None, :, :] * pwi[d][..., idx].transpose(0, 2, 1)[:, :, None, :]
        wi = cr[d][:, None, :, :] * pwi[d][..., idx].transpose(0, 2, 1)[:, :, None, :] \
            + ci[d][:, None, :, :] * pwr[d][..., idx].transpose(0, 2, 1)[:, :, None, :]
        return (wr.reshape(g, tc * hch, p).transpose(0, 2, 1),
                wi.reshape(g, tc * hch, p).transpose(0, 2, 1))

    ofr, ofi = so(0, jnp.arange(1, tc + 1))
    obr, obi = so(1, jnp.arange(tc, 0, -1))
    zero = jnp.zeros_like(ofr)
    w_of = jnp.concatenate([ofr, zero, -ofi, zero], axis=1)
    w_ob = jnp.concatenate([zero, obr, zero, -obi], axis=1)

    a_re = jnp.concatenate([pwr[0][..., tc], pwr[1][..., tc]], axis=-1)
    a_im = jnp.concatenate([pwi[0][..., tc], pwi[1][..., tc]], axis=-1)

    gs = S5_SET // hch
    ns = g // gs
    kd = tc * hch
    w_o = w_of + w_ob
    return (m.astype(BF16).reshape(ns, gs, kd, kd),
            w_st.astype(BF16).reshape(ns, gs, kd, 4 * p),
            w_o.astype(BF16).reshape(ns, gs, 4 * p, kd),
            a_re.reshape(ns, gs, 2 * p), a_im.reshape(ns, gs, 2 * p))


def _chunk_perm(tc, gs, hch):
    n = tc * gs * hch
    r = jnp.arange(n, dtype=jnp.int32)
    t, g, j = r // (gs * hch), (r // hch) % gs, r % hch
    dst = g * (tc * hch) + t * hch + j
    fwd = (dst[:, None] == r[None, :]).astype(BF16)
    return fwd, fwd.T


def _s5_kernel(zx_ref, zc_ref, pf_ref, pb_ref, m_ref, wst_ref, wo_ref, are_ref, aim_ref, ox_ref, oc_ref,
               sh_ref, xb_ref, yb_ref, *, ctx_chunks, x_chunks, pitch):
    tc = SSM_CHUNK
    nch = ctx_chunks + x_chunks
    lanes = zx_ref.shape[1]
    gs, kd, nst = wst_ref.shape
    for t in range(tc):
        xb_ref[0:ctx_chunks, t * lanes:(t + 1) * lanes] = zc_ref[pl.ds(t, ctx_chunks, stride=tc), :].astype(BF16)
        xb_ref[ctx_chunks:nch, t * lanes:(t + 1) * lanes] = zx_ref[pl.ds(t, x_chunks, stride=tc), :].astype(BF16)
    xb_ref[...] = jnp.dot(xb_ref[...], pf_ref[...], preferred_element_type=F32).astype(BF16)

    nslab = gs * nst // lanes
    half = nslab // 2
    for j in range(gs):
        s = jnp.dot(xb_ref[:, j * kd:(j + 1) * kd], wst_ref[j], preferred_element_type=F32)
        sh_ref[j * pitch:j * pitch + nch, :] = s[:, 0:lanes]
        sh_ref[(half + j) * pitch:(half + j) * pitch + nch, :] = s[:, lanes:2 * lanes]

    fwd = lax.broadcasted_iota(jnp.int32, (half, lanes), 1) < (lanes // 2)
    a_re = are_ref[...]
    a_im = aim_ref[...]

    def rows(c, part):
        return pl.ds(part * half * pitch + c, half, stride=pitch)

    def step(i, carry):
        hr, hi = carry
        cf = i
        cb = jnp.where(i < ctx_chunks, ctx_chunks - 1 - i, nch - 1 + ctx_chunks - i)
        same = cf == cb
        sfr, sfi = sh_ref[rows(cf, 0), :], sh_ref[rows(cf, 1), :]
        sbr, sbi = sh_ref[rows(cb, 0), :], sh_ref[rows(cb, 1), :]
        sh_ref[rows(cf, 0), :] = jnp.where(fwd, hr, sfr)
        sh_ref[rows(cf, 1), :] = jnp.where(fwd, hi, sfi)
        sh_ref[rows(cb, 0), :] = jnp.where(fwd, jnp.where(same, hr, sbr), hr)
        sh_ref[rows(cb, 1), :] = jnp.where(fwd, jnp.where(same, hi, sbi), hi)
        sr = jnp.where(fwd, sfr, sbr)
        si = jnp.where(fwd, sfi, sbi)
        return a_re * hr - a_im * hi + sr, a_re * hi + a_im * hr + si

    zero = jnp.zeros((half, lanes), F32)
    lax.fori_loop(0, nch, step, (zero, zero))

    for j in range(gs):
        hb = jnp.concatenate([sh_ref[j * pitch:j * pitch + nch, :],
                              sh_ref[(half + j) * pitch:(half + j) * pitch + nch, :]], axis=1).astype(BF16)
        yj = (jnp.dot(xb_ref[:, j * kd:(j + 1) * kd], m_ref[j], preferred_element_type=F32)
              + jnp.dot(hb, wo_ref[j], preferred_element_type=F32))
        yb_ref[:, j * kd:(j + 1) * kd] = yj.astype(BF16)
    y = jnp.dot(yb_ref[...], pb_ref[...], preferred_element_type=F32)
    for t in range(tc):
        oc_ref[pl.ds(t, ctx_chunks, stride=tc), :] = y[0:ctx_chunks, t * lanes:(t + 1) * lanes]
        ox_ref[pl.ds(t, x_chunks, stride=tc), :] = y[ctx_chunks:nch, t * lanes:(t + 1) * lanes]


def _s5_scan(zs, tables, bsz, seq, n_ctx):
    m, w_st, w_o, a_re, a_im = tables
    ns, gs, kd, nst = w_st.shape
    tc = SSM_CHUNK
    lanes = S5_SET
    kset = gs * kd
    ctx_chunks, x_chunks = n_ctx // tc, seq // tc
    nch = ctx_chunks + x_chunks
    pitch = -(-nch // 8) * 8
    if (pitch // 8) % 2 == 0:
        pitch += 8
    ctx0 = bsz * seq // n_ctx
    perm_f, perm_b = _chunk_perm(tc, gs, kd // tc)
    kern = functools.partial(_s5_kernel, ctx_chunks=ctx_chunks, x_chunks=x_chunks, pitch=pitch)
    per_set = lambda arr: pl.BlockSpec((None,) + arr.shape[1:], lambda s, b: (s,) + (0,) * (arr.ndim - 1))
    return pl.pallas_call(
        kern,
        grid=(ns, bsz),
        in_specs=[pl.BlockSpec((seq, lanes), lambda s, b: (b, s)),
                  pl.BlockSpec((n_ctx, lanes), lambda s, b: (ctx0 + b, s)),
                  pl.BlockSpec((kset, kset), lambda s, b: (0, 0)),
                  pl.BlockSpec((kset, kset), lambda s, b: (0, 0)),
                  per_set(m), per_set(w_st), per_set(w_o), per_set(a_re), per_set(a_im)],
        out_specs=[pl.BlockSpec((seq, lanes), lambda s, b: (b, s)),
                   pl.BlockSpec((n_ctx, lanes), lambda s, b: (b, s))],
        out_shape=[jax.ShapeDtypeStruct((bsz * seq, zs.shape[1]), F32),
                   jax.ShapeDtypeStruct((bsz * n_ctx, zs.shape[1]), F32)],
        scratch_shapes=[pltpu.VMEM((gs * nst // lanes * pitch, lanes), F32),
                        pltpu.VMEM((nch, kset), BF16),
                        pltpu.VMEM((nch, kset), BF16)],
        compiler_params=_cp(("parallel", "parallel")),
        name="s5_scan",
    )(zs, zs, perm_f, perm_b, m, w_st, w_o, a_re, a_im)


def _glu_kernel(yx_ref, yc_ref, w_ref, b_ref, gain_ref, o_ref, *, nx_tiles):
    y = jnp.where(pl.program_id(0) < nx_tiles, yx_ref[...], yc_ref[...])
    g = jax.nn.gelu(y)
    v = g * jax.nn.sigmoid(jnp.dot(g.astype(BF16), w_ref[...], preferred_element_type=F32) + b_ref[...])
    r = lax.rsqrt(jnp.mean(v * v, axis=-1, keepdims=True) + EPS)
    o_ref[...] = (v * r * gain_ref[...]).astype(o_ref.dtype)


def _glu(yx, yc, w, b, gain):
    ds = yx.shape[1]
    tm = 512
    nx_tiles, nc_tiles = yx.shape[0] // tm, yc.shape[0] // tm
    return pl.pallas_call(
        functools.partial(_glu_kernel, nx_tiles=nx_tiles),
        grid=(nx_tiles + nc_tiles,),
        in_specs=[pl.BlockSpec((tm, ds), lambda i: (jnp.minimum(i, nx_tiles - 1), 0)),
                  pl.BlockSpec((tm, ds), lambda i: (jnp.maximum(i - nx_tiles, 0), 0)),
                  pl.BlockSpec((ds, ds), lambda i: (0, 0)),
                  pl.BlockSpec((1, ds), lambda i: (0, 0)),
                  pl.BlockSpec((1, ds), lambda i: (0, 0))],
        out_specs=pl.BlockSpec((tm, ds), lambda i: (i, 0)),
        out_shape=jax.ShapeDtypeStruct((yx.shape[0] + yc.shape[0], ds), BF16),
        compiler_params=_cp(("parallel",)),
        name="glu",
    )(yx, yc, w, b, gain)


def _route(scores, sel, n_groups):
    epg = len(sel) // n_groups
    gscore = []
    for q in range(n_groups):
        v = sel[q * epg:(q + 1) * epg]
        best = None
        for a in range(epg):
            for b in range(a + 1, epg):
                s = v[a] + v[b]
                best = s if best is None else jnp.maximum(best, s)
        gscore.append(best)
    gbest = gscore[0]
    gidx = jnp.zeros(gbest.shape, jnp.int32)
    for q in range(1, n_groups):
        upd = gscore[q] > gbest
        gbest = jnp.where(upd, gscore[q], gbest)
        gidx = jnp.where(upd, q, gidx)
    vin = list(sel[:epg])
    sin = list(scores[:epg])
    for q in range(1, n_groups):
        pick = gidx == q
        for j in range(epg):
            vin[j] = jnp.where(pick, sel[q * epg + j], vin[j])
            sin[j] = jnp.where(pick, scores[q * epg + j], sin[j])
    b1 = vin[0]
    i1 = jnp.zeros(gbest.shape, jnp.int32)
    for j in range(1, epg):
        upd = vin[j] > b1
        b1 = jnp.where(upd, vin[j], b1)
        i1 = jnp.where(upd, j, i1)
    b2 = vin[0]
    i2 = jnp.zeros(gbest.shape, jnp.int32)
    have = jnp.zeros(gbest.shape, jnp.bool_)
    for j in range(epg):
        cand = i1 != j
        upd = cand & (jnp.logical_not(have) | (vin[j] > b2))
        b2 = jnp.where(upd, vin[j], b2)
        i2 = jnp.where(upd, j, i2)
        have = have | cand
    s1 = sin[0]
    s2 = sin[0]
    for j in range(1, epg):
        s1 = jnp.where(i1 == j, sin[j], s1)
        s2 = jnp.where(i2 == j, sin[j], s2)
    tot = s1 + s2
    return (gidx * epg + i1, gidx * epg + i2), (s1 / tot, s2 / tot)


def _merge_kernel(t_ref, yf_ref, ys_ref, wo_ref, g1_ref, n2_ref, sc2_ref, sh2_ref, rwh_ref, rwl_ref, rb_ref,
                  tn_ref, h3_ref, eidx_ref, ew_ref):
    df = yf_ref.shape[1]
    tm, d = t_ref.shape
    o = (jnp.dot(yf_ref[...], wo_ref[0:df, :], preferred_element_type=F32)
         + jnp.dot(ys_ref[...], wo_ref[df:, :], preferred_element_type=F32))
    tn = t_ref[...] + g1_ref[0] * o
    tn_ref[...] = tn
    h2 = tn * lax.rsqrt(jnp.mean(tn * tn, axis=-1, keepdims=True) + EPS) * n2_ref[...]
    h2 = h2 * (1.0 + sc2_ref[0]) + sh2_ref[0]
    nsub = d // LANES
    for c in range(nsub):
        h3_ref[pl.ds(c, tm, stride=nsub), :] = h2[:, c * LANES:(c + 1) * LANES]
    h_hi = h2.astype(BF16)
    h_lo = (h2 - h_hi.astype(F32)).astype(BF16)
    lg = (jnp.dot(h_hi, rwh_ref[...], preferred_element_type=F32)
          + jnp.dot(h_lo, rwh_ref[...], preferred_element_type=F32)
          + jnp.dot(h_hi, rwl_ref[...], preferred_element_type=F32))
    ne = rb_ref.shape[0]
    logits = lg.T[0:ne, :]
    scores = jax.nn.sigmoid(logits)
    sel = scores + rb_ref[...]
    srows = [scores[e:e + 1, :] for e in range(ne)]
    vrows = [sel[e:e + 1, :] for e in range(ne)]
    (e1, e2), (w1, w2) = _route(srows, vrows, N_EXPERT_GROUPS)
    eidx_ref[...] = jnp.zeros_like(eidx_ref)
    ew_ref[...] = jnp.zeros_like(ew_ref)
    eidx_ref[0:1, :] = e1
    eidx_ref[1:2, :] = e2
    ew_ref[0:1, :] = w1
    ew_ref[1:2, :] = w2


def _merge(t, yf, ys, wo, g1, n2, sc2, sh2, rwh, rwl, rb, n_x_rows, seq):
    nt, d = t.shape
    df = yf.shape[1]
    ne = rb.shape[0]
    nsub = d // LANES
    tm = ROW_TILE
    mrow = functools.partial(_mod_row, tm=tm, n_x_rows=n_x_rows, seq=seq)
    mspec = pl.BlockSpec((1, 1, d), lambda i: (mrow(i), 0, 0))
    return pl.pallas_call(
        _merge_kernel,
        grid=(nt // tm,),
        in_specs=[pl.BlockSpec((tm, d), lambda i: (i, 0)),
                  pl.BlockSpec((tm, df), lambda i: (i, 0)),
                  pl.BlockSpec((tm, d - df), lambda i: (i, 0)),
                  pl.BlockSpec((d, d), lambda i: (0, 0)),
                  mspec,
                  pl.BlockSpec((1, d), lambda i: (0, 0)),
                  mspec, mspec,
                  pl.BlockSpec((d, LANES), lambda i: (0, 0)),
                  pl.BlockSpec((d, LANES), lambda i: (0, 0)),
                  pl.BlockSpec((ne, 1), lambda i: (0, 0))],
        out_specs=[pl.BlockSpec((tm, d), lambda i: (i, 0)),
                   pl.BlockSpec((tm * nsub, LANES), lambda i: (i, 0)),
                   pl.BlockSpec((8, tm), lambda i: (0, i)),
                   pl.BlockSpec((8, tm), lambda i: (0, i))],
        out_shape=[jax.ShapeDtypeStruct((nt, d), F32),
                   jax.ShapeDtypeStruct((nt * nsub, LANES), F32),
                   jax.ShapeDtypeStruct((8, nt), jnp.int32),
                   jax.ShapeDtypeStruct((8, nt), F32)],
        compiler_params=_cp(("parallel",)),
        name="merge_route",
    )(t, yf, ys, wo, g1, n2, sc2, sh2, rwh, rwl, rb)


def _dispatch(eidx, ew, n_experts, rows, n_blocks):
    nt = eidx.shape[1]
    a = nt * TOP_K
    flat_e = eidx[:TOP_K].T.reshape(a)
    onehot = (flat_e[:, None] == jnp.arange(n_experts, dtype=jnp.int32)[None, :]).astype(jnp.int32)
    csum = jnp.cumsum(onehot, axis=0)
    counts = csum[-1]
    padded = (counts + rows - 1) // rows * rows
    pad_end = jnp.cumsum(padded)
    pad_start = pad_end - padded
    dest = jnp.sum(onehot * (csum - 1 + pad_start[None, :]), axis=1)
    tok = jnp.arange(a, dtype=jnp.int32) // TOP_K
    buf_tok = jnp.zeros((n_blocks * rows,), jnp.int32).at[dest].set(tok)
    n_valid = (pad_end[-1] // rows).astype(jnp.int32)
    blk_start = jnp.arange(n_blocks, dtype=jnp.int32) * rows
    blk_exp = jnp.sum((pad_end[None, :] <= blk_start[:, None]).astype(jnp.int32), axis=1)
    blk_exp = jnp.minimum(blk_exp, n_experts - 1)
    last_exp = jnp.sum(jnp.where(jnp.arange(n_blocks) == n_valid - 1, blk_exp, 0))
    blk_exp = jnp.where(jnp.arange(n_blocks) < n_valid, blk_exp, last_exp).astype(jnp.int32)
    return buf_tok, blk_exp, n_valid.reshape(1), dest.reshape(nt, TOP_K), ew[:TOP_K].T


def _expert_kernel(be_ref, nv_ref, tok_ref, tokn_ref, h_hbm, wg_ref, wu_ref, wd_ref, o_ref,
                   xg_ref, xb_ref, acc_ref, sem, *, nsub):
    i = pl.program_id(0)
    f = pl.program_id(1)
    nf = pl.num_programs(1)
    rows = xb_ref.shape[0]
    valid = i < nv_ref[0]
    slot = i % 2

    def gather(toks, dst_slot):
        def issue(r, c):
            src = h_hbm.at[pl.ds(pl.multiple_of(toks[0, r] * nsub, nsub), nsub)]
            dst = xg_ref.at[dst_slot, pl.ds(pl.multiple_of(r * GATHER_PITCH, 8), nsub)]
            pltpu.make_async_copy(src, dst, sem.at[dst_slot]).start()
            return c

        lax.fori_loop(0, rows, issue, 0)

    @pl.when((f == 0) & (i == 0))
    def _():
        gather(tok_ref, 0)

    @pl.when((f == 0) & (i + 1 < nv_ref[0]) & (i + 1 < pl.num_programs(0)))
    def _():
        gather(tokn_ref, 1 - slot)

    @pl.when(valid & (f == 0))
    def _():
        pltpu.make_async_copy(h_hbm.at[pl.ds(0, rows * nsub)], xg_ref.at[slot, pl.ds(0, rows * nsub)],
                              sem.at[slot]).wait()
        for s in range(2):
            @pl.when(slot == s)
            def _():
                for c in range(nsub):
                    xb_ref[:, c * LANES:(c + 1) * LANES] = (
                        xg_ref[s, pl.ds(c, rows, stride=GATHER_PITCH), :].astype(BF16))

    @pl.when(valid)
    def _():
        x = xb_ref[...]
        g = jnp.dot(x, wg_ref[...].astype(BF16), preferred_element_type=F32)
        u = jnp.dot(x, wu_ref[...].astype(BF16), preferred_element_type=F32)
        hmid = (g * jax.nn.sigmoid(g)) * u
        contrib = jnp.dot(hmid.astype(BF16), wd_ref[...].astype(BF16), preferred_element_type=F32)

        @pl.when(f == 0)
        def _():
            acc_ref[...] = contrib

        @pl.when(f > 0)
        def _():
            acc_ref[...] += contrib

        @pl.when(f == nf - 1)
        def _():
            for c in range(nsub):
                o_ref[pl.ds(c, rows, stride=nsub), :] = acc_ref[:, c * LANES:(c + 1) * LANES]

    @pl.when(jnp.logical_not(valid) & (f == nf - 1))
    def _():
        o_ref[...] = jnp.zeros_like(o_ref)


def _experts(h3, buf_tok, blk_exp, n_valid, w_gate, w_up, w_down, layer):
    _, ne, d, de = w_gate.shape
    nsub = d // LANES
    rows, tf = MOE_ROWS, MOE_FTILE
    n_blocks = buf_tok.shape[0] // rows
    nf = de // tf

    def fidx(i, f, nv):
        return jnp.where(i < nv[0], f, nf - 1)

    grid_spec = pltpu.PrefetchScalarGridSpec(
        num_scalar_prefetch=2,
        grid=(n_blocks, nf),
        in_specs=[pl.BlockSpec((None, 1, rows), lambda i, f, be, nv: (i, 0, 0), memory_space=pltpu.SMEM),
                  pl.BlockSpec((None, 1, rows), lambda i, f, be, nv: (jnp.minimum(i + 1, n_blocks - 1), 0, 0),
                               memory_space=pltpu.SMEM),
                  pl.BlockSpec(memory_space=pl.ANY),
                  pl.BlockSpec((None, None, d, tf), lambda i, f, be, nv: (layer, be[i], 0, fidx(i, f, nv))),
                  pl.BlockSpec((None, None, d, tf), lambda i, f, be, nv: (layer, be[i], 0, fidx(i, f, nv))),
                  pl.BlockSpec((None, None, tf, d), lambda i, f, be, nv: (layer, be[i], fidx(i, f, nv), 0))],
        out_specs=pl.BlockSpec((rows * nsub, LANES), lambda i, f, be, nv: (i, 0)),
        scratch_shapes=[pltpu.VMEM((2, rows * GATHER_PITCH, LANES), F32),
                        pltpu.VMEM((rows, d), BF16),
                        pltpu.VMEM((rows, d), F32),
                        pltpu.SemaphoreType.DMA((2,))],
    )
    toks = buf_tok.reshape(n_blocks, 1, rows)
    return pl.pallas_call(
        functools.partial(_expert_kernel, nsub=nsub),
        grid_spec=grid_spec,
        out_shape=jax.ShapeDtypeStruct((n_blocks * rows * nsub, LANES), F32),
        compiler_params=_cp(("arbitrary", "arbitrary")),
        name="experts",
    )(blk_exp, n_valid, toks, toks, h3, w_gate, w_up, w_down)


def _combine_kernel(pos_ref, t_ref, w_ref, yb_hbm, g2_ref, o_ref, ga_ref, gb_ref, sem, *, nsub):
    tm = t_ref.shape[0]

    def issue(r, c):
        dst = pl.ds(pl.multiple_of(r * GATHER_PITCH, 8), nsub)
        for k, buf in enumerate((ga_ref, gb_ref)):
            src = pl.ds(pl.multiple_of(pos_ref[0, TOP_K * r + k] * nsub, nsub), nsub)
            pltpu.make_async_copy(yb_hbm.at[src], buf.at[dst], sem).start()
        return c

    lax.fori_loop(0, tm, issue, 0)
    for buf in (ga_ref, gb_ref):
        pltpu.make_async_copy(yb_hbm.at[pl.ds(0, tm * nsub)], buf.at[pl.ds(0, tm * nsub)], sem).wait()
    w0 = w_ref[:, 0:1]
    w1 = w_ref[:, 1:2]
    for c in range(nsub):
        sl = slice(c * LANES, (c + 1) * LANES)
        y = (ga_ref[pl.ds(c, tm, stride=GATHER_PITCH), :] * w0
             + gb_ref[pl.ds(c, tm, stride=GATHER_PITCH), :] * w1)
        o_ref[:, sl] = t_ref[:, sl] + g2_ref[0][:, sl] * y


def _combine(t, yb3, pos, w, g2, n_x_rows, seq):
    nt, d = t.shape
    nsub = d // LANES
    tm = ROW_TILE
    mrow = functools.partial(_mod_row, tm=tm, n_x_rows=n_x_rows, seq=seq)
    return pl.pallas_call(
        functools.partial(_combine_kernel, nsub=nsub),
        grid=(nt // tm,),
        in_specs=[pl.BlockSpec((None, 1, TOP_K * tm), lambda i: (i, 0, 0), memory_space=pltpu.SMEM),
                  pl.BlockSpec((tm, d), lambda i: (i, 0)),
                  pl.BlockSpec((tm, TOP_K), lambda i: (i, 0)),
                  pl.BlockSpec(memory_space=pl.ANY),
                  pl.BlockSpec((1, 1, d), lambda i: (mrow(i), 0, 0))],
        out_specs=pl.BlockSpec((tm, d), lambda i: (i, 0)),
        out_shape=jax.ShapeDtypeStruct((nt, d), F32),
        scratch_shapes=[pltpu.VMEM((tm * GATHER_PITCH, LANES), F32),
                        pltpu.VMEM((tm * GATHER_PITCH, LANES), F32),
                        pltpu.SemaphoreType.DMA],
        compiler_params=_cp(("arbitrary",)),
        name="combine",
    )(pos.reshape(nt // tm, 1, TOP_K * tm), t, w, yb3, g2)


def _final_kernel(x_ref, g_ref, o_ref):
    x = x_ref[...]
    o_ref[...] = x * lax.rsqrt(jnp.mean(x * x, axis=-1, keepdims=True) + EPS) * g_ref[...]


def _final_norm(t, g, n_rows):
    d = t.shape[1]
    tm = ROW_TILE
    return pl.pallas_call(
        _final_kernel,
        grid=(n_rows // tm,),
        in_specs=[pl.BlockSpec((tm, d), lambda i: (i, 0)),
                  pl.BlockSpec((1, d), lambda i: (0, 0))],
        out_specs=pl.BlockSpec((tm, d), lambda i: (i, 0)),
        out_shape=jax.ShapeDtypeStruct((n_rows, d), F32),
        compiler_params=_cp(("parallel",)),
        name="final_norm",
    )(t, g)


def kernel(x, c, ctx, c_ctx, w_mod, b_mod, norm1_g, norm2_g, w_in, w_out, fourier_w, mix_norm_g,
           lam_re, lam_im, log_dt, b_re, b_im, c_re, c_im, d_skip, glu_w, glu_b,
           router_w, router_b, w_gate, w_up, w_down, final_g):
    bsz, seq, d = x.shape
    n_ctx = ctx.shape[1]
    depth = w_mod.shape[0]
    df = fourier_w.shape[1] * fourier_w.shape[2]
    ds = d_skip.shape[1]
    ne = router_w.shape[1]
    nx_rows = bsz * seq
    nt = nx_rows + bsz * n_ctx

    t = jnp.concatenate([x.reshape(nx_rows, d), ctx.reshape(bsz * n_ctx, d)], axis=0).astype(F32)

    cvec = jnp.concatenate([c_ctx[None, :], c, jnp.zeros((8 - 1 - bsz, d), c.dtype)], axis=0).astype(F32)
    mod = _adaln(cvec, w_mod, b_mod).reshape(depth, 8, N_MOD, 1, d)

    cc, sc_ = _dft_tables(df)
    w_chan = jnp.concatenate([cc, sc_], axis=1).astype(BF16)
    clx, slx = _dft_tables(seq)
    clx, slx = clx.astype(BF16), (-slx).astype(BF16)
    clc, slc = _dft_tables(n_ctx)
    clc, slc = clc.astype(BF16), (-slc).astype(BF16)

    rw = jnp.pad(router_w.astype(F32), ((0, 0), (0, LANES - ne)))
    rwh = rw.astype(BF16)
    rwl = (rw - rwh.astype(F32)).astype(BF16)
    rb = router_b.astype(F32).reshape(ne, 1)
    n_blocks = -(-(nt * TOP_K) // MOE_ROWS) + ne

    for l in range(depth):
        sh1, sc1, g1, sh2, sc2, g2 = [mod[l, :, k] for k in range(N_MOD)]
        zf, zs = _inproj(t, norm1_g[l].reshape(1, d).astype(F32), sc1, sh1, w_in[l].astype(BF16),
                         df, nx_rows, seq)

        ab = _chan_dft(zf, w_chan)
        hw = fourier_w[l].astype(BF16)
        gain = mix_norm_g[l].astype(F32).reshape(1, -1)
        yf_x = _row_dft(ab, clx, slx, hw, gain[:, :df], 0, bsz, seq, 512, 512)
        yf_c = _row_dft(ab, clc, slc, hw, gain[:, :df], nx_rows, bsz, n_ctx, n_ctx, n_ctx)
        yf = jnp.concatenate([yf_x, yf_c], axis=0)

        tables = _s5_tables(lam_re[l], lam_im[l], log_dt[l], b_re[l], b_im[l], c_re[l], c_im[l], d_skip[l])
        ysx, ysc = _s5_scan(zs, tables, bsz, seq, n_ctx)
        ys = _glu(ysx, ysc, glu_w[l].astype(BF16), glu_b[l].astype(F32).reshape(1, ds), gain[:, df:])

        t, h3, eidx, ew = _merge(t, yf, ys, w_out[l].astype(BF16), g1,
                                 norm2_g[l].reshape(1, d).astype(F32), sc2, sh2, rwh, rwl, rb, nx_rows, seq)

        buf_tok, blk_exp, n_valid, pos, wtok = _dispatch(eidx, ew, ne, MOE_ROWS, n_blocks)
        yb = _experts(h3, buf_tok, blk_exp, n_valid, w_gate, w_up, w_down, l)
        t = _combine(t, yb, pos, wtok, g2, nx_rows, seq)

    out = _final_norm(t, final_g.reshape(1, d).astype(F32), nx_rows)
    return out.reshape(bsz, seq, d).astype(x.dtype)
```

```python
import functools
import math

import jax
import jax.numpy as jnp
from jax import lax
from jax.experimental import pallas as pl
from jax.experimental.pallas import tpu as pltpu

F32 = jnp.float32
BF16 = jnp.bfloat16
EPS = 1e-6

FOURIER_HEADS = 4
SSM_GROUP = 16
N_EXPERT_GROUPS = 4
TOP_K = 2
N_MOD = 6

SSM_CHUNK = 16
ROW_TILE = 256
MOE_ROWS = 256
MOE_WCHUNKS = 8
LANES = 128
GATHER_PITCH = 24
S5_SET = LANES
VMEM_LIMIT_BYTES = 56 * 1024 * 1024


def _cp(sems):
    return pltpu.CompilerParams(dimension_semantics=sems, vmem_limit_bytes=VMEM_LIMIT_BYTES)


def _mod_row(i, tm, n_x_rows, seq):
    r = i * tm
    return jnp.where(r < n_x_rows, 1 + r // seq, 0)


def _mod_kernel(c_ref, w_ref, b_ref, o_ref):
    c = c_ref[...]
    s = c * jax.nn.sigmoid(c)
    o_ref[...] = jnp.dot(s.astype(BF16), w_ref[...].astype(BF16),
                         preferred_element_type=F32) + b_ref[...]


def _adaln(cvec, w_mod, b_mod):
    depth, d, n = w_mod.shape
    tn = 1024
    return pl.pallas_call(
        _mod_kernel,
        grid=(depth, n // tn),
        in_specs=[pl.BlockSpec((8, d), lambda l, j: (0, 0)),
                  pl.BlockSpec((None, d, tn), lambda l, j: (l, 0, j)),
                  pl.BlockSpec((None, 1, tn), lambda l, j: (l, 0, j))],
        out_specs=pl.BlockSpec((None, 8, tn), lambda l, j: (l, 0, j)),
        out_shape=jax.ShapeDtypeStruct((depth, 8, n), F32),
        compiler_params=_cp(("parallel", "parallel")),
        name="adaln",
    )(cvec, w_mod, b_mod.reshape(depth, 1, n))


def _inproj_kernel(t_ref, g_ref, sc_ref, sh_ref, w_ref, zf_ref, zs_ref):
    x = t_ref[...]
    h = x * lax.rsqrt(jnp.mean(x * x, axis=-1, keepdims=True) + EPS) * g_ref[...]
    h = h * (1.0 + sc_ref[0]) + sh_ref[0]
    z = jnp.dot(h.astype(BF16), w_ref[...], preferred_element_type=F32)
    df = zf_ref.shape[1]
    zf_ref[...] = z[:, :df].astype(zf_ref.dtype)
    zs_ref[...] = z[:, df:]


def _inproj(t, g, sc, sh, w, df, n_x_rows, seq):
    nt, d = t.shape
    n = w.shape[1]
    tm = ROW_TILE
    mrow = functools.partial(_mod_row, tm=tm, n_x_rows=n_x_rows, seq=seq)
    return pl.pallas_call(
        _inproj_kernel,
        grid=(nt // tm,),
        in_specs=[pl.BlockSpec((tm, d), lambda i: (i, 0)),
                  pl.BlockSpec((1, d), lambda i: (0, 0)),
                  pl.BlockSpec((1, 1, d), lambda i: (mrow(i), 0, 0)),
                  pl.BlockSpec((1, 1, d), lambda i: (mrow(i), 0, 0)),
                  pl.BlockSpec((d, n), lambda i: (0, 0))],
        out_specs=[pl.BlockSpec((tm, df), lambda i: (i, 0)),
                   pl.BlockSpec((tm, n - df), lambda i: (i, 0))],
        out_shape=[jax.ShapeDtypeStruct((nt, df), BF16),
                   jax.ShapeDtypeStruct((nt, n - df), F32)],
        compiler_params=_cp(("parallel",)),
        name="inproj",
    )(t, g, sc, sh, w)


def _dft_tables(n):
    r = 1
    while r * r < n:
        r *= 2
    q = n // r
    k = jnp.arange(n, dtype=jnp.int32)[:, None]
    step = 2.0 * math.pi / n
    pa = ((k * (jnp.arange(q, dtype=jnp.int32)[None, :] * r)) % n).astype(F32) * step
    pb = ((k * jnp.arange(r, dtype=jnp.int32)[None, :]) % n).astype(F32) * step
    ca, sa = jnp.cos(pa)[:, :, None], jnp.sin(pa)[:, :, None]
    cb, sb = jnp.cos(pb)[:, None, :], jnp.sin(pb)[:, None, :]
    c = (ca * cb - sa * sb).reshape(n, n)
    s = (sa * cb + ca * sb).reshape(n, n)
    return c, s


def _chan_dft_kernel(z_ref, w_ref, o_ref):
    o_ref[...] = jnp.dot(z_ref[...], w_ref[...], preferred_element_type=F32).astype(o_ref.dtype)


def _chan_dft(z, w):
    nt = z.shape[0]
    df, n = w.shape
    tm = 512
    return pl.pallas_call(
        _chan_dft_kernel,
        grid=(nt // tm,),
        in_specs=[pl.BlockSpec((tm, df), lambda i: (i, 0)),
                  pl.BlockSpec((df, n), lambda i: (0, 0))],
        out_specs=pl.BlockSpec((tm, n), lambda i: (i, 0)),
        out_shape=jax.ShapeDtypeStruct((nt, n), BF16),
        compiler_params=_cp(("parallel",)),
        name="chan_dft",
    )(z, w)


def _row_dft_kernel(cl_ref, sl_ref, a_ref, b_ref, hw_ref, gain_ref, o_ref, acc_ref, *, scale):
    k = pl.program_id(2)

    @pl.when(k == 0)
    def _():
        acc_ref[...] = jnp.zeros_like(acc_ref)

    acc_ref[...] += (jnp.dot(cl_ref[...], a_ref[...], preferred_element_type=F32)
                     + jnp.dot(sl_ref[...], b_ref[...], preferred_element_type=F32))

    @pl.when(k == pl.num_programs(2) - 1)
    def _():
        f = acc_ref[...] * scale
        nh, hd, _ = hw_ref.shape
        ys = [jnp.dot(f[:, h * hd:(h + 1) * hd].astype(BF16), hw_ref[h],
                      preferred_element_type=F32) for h in range(nh)]
        ss = ys[0] * ys[0]
        ssum = jnp.sum(ss, axis=-1, keepdims=True)
        for h in range(1, nh):
            ssum = ssum + jnp.sum(ys[h] * ys[h], axis=-1, keepdims=True)
        r = lax.rsqrt(ssum / (nh * hd) + EPS)
        for h in range(nh):
            o_ref[:, h * hd:(h + 1) * hd] = (ys[h] * r * gain_ref[:, h * hd:(h + 1) * hd]).astype(o_ref.dtype)


def _row_dft(ab, cl, sl_neg, hw, gain, row0, bsz, length, tm, tk):
    df = ab.shape[1] // 2
    kb = length // tk
    mb = length // tm
    off = row0 // tk
    scale = 1.0 / math.sqrt(length * df)
    return pl.pallas_call(
        functools.partial(_row_dft_kernel, scale=scale),
        grid=(bsz, mb, kb),
        in_specs=[pl.BlockSpec((tm, tk), lambda b, m, k: (m, k)),
                  pl.BlockSpec((tm, tk), lambda b, m, k: (m, k)),
                  pl.BlockSpec((tk, df), lambda b, m, k: (off + b * kb + k, 0)),
                  pl.BlockSpec((tk, df), lambda b, m, k: (off + b * kb + k, 1)),
                  pl.BlockSpec(hw.shape, lambda b, m, k: (0, 0, 0)),
                  pl.BlockSpec((1, df), lambda b, m, k: (0, 0))],
        out_specs=pl.BlockSpec((tm, df), lambda b, m, k: (b * mb + m, 0)),
        out_shape=jax.ShapeDtypeStruct((bsz * length, df), BF16),
        scratch_shapes=[pltpu.VMEM((tm, df), F32)],
        compiler_params=_cp(("parallel", "parallel", "arbitrary")),
        name="row_dft",
    )(cl, sl_neg, ab, ab, hw, gain)


def _s5_tables(lam_re, lam_im, log_dt, b_re, b_im, c_re, c_im, d_skip):
    tc = SSM_CHUNK
    g, p = lam_re.shape[1:]
    hch = b_re.shape[-1]
    lr, li = lam_re.astype(F32), lam_im.astype(F32)
    dt = jnp.exp(log_dt.astype(F32))[..., None]
    er, ei = lr * dt, li * dt

    def lpow(k):
        m = jnp.exp(er * k)
        return m * jnp.cos(ei * k), m * jnp.sin(ei * k)

    l1r, l1i = lpow(1.0)
    den = lr * lr + li * li
    qr = ((l1r - 1.0) * lr + l1i * li) / den
    qi = (l1i * lr - (l1r - 1.0) * li) / den
    br, bi = b_re.astype(F32), b_im.astype(F32)
    bbr = qr[..., None] * br - qi[..., None] * bi
    bbi = qr[..., None] * bi + qi[..., None] * br
    cr, ci = c_re.astype(F32), c_im.astype(F32)

    ks = jnp.arange(tc + 1, dtype=F32)
    pwr = jnp.stack([lpow(k)[0] for k in range(tc + 1)], axis=-1)
    pwi = jnp.stack([lpow(k)[1] for k in range(tc + 1)], axis=-1)
    del ks

    cbr = jnp.einsum("dgip,dgpj->dgpij", cr, bbr) - jnp.einsum("dgip,dgpj->dgpij", ci, bbi)
    cbi = jnp.einsum("dgip,dgpj->dgpij", cr, bbi) + jnp.einsum("dgip,dgpj->dgpij", ci, bbr)
    klag = (jnp.einsum("dgpij,dgpl->dglij", cbr, pwr[..., :tc])
            - jnp.einsum("dgpij,dgpl->dglij", cbi, pwi[..., :tc]))

    t_in = jnp.arange(tc)[:, None]
    t_out = jnp.arange(tc)[None, :]
    lag_f = t_out - t_in
    lag_b = t_in - t_out
    kf = klag[0][:, jnp.clip(lag_f, 0, tc - 1)] * (lag_f >= 0)[None, :, :, None, None]
    kb = klag[1][:, jnp.clip(lag_b, 0, tc - 1)] * (lag_b >= 0)[None, :, :, None, None]
    eye_t = (t_in == t_out).astype(F32)[None, :, :, None, None]
    dg = d_skip.astype(F32).reshape(g, hch)
    dmat = eye_t * (jnp.eye(hch, dtype=F32) * dg[:, None, :])[:, None, None, :, :]
    m = (kf + kb + dmat).transpose(0, 1, 4, 2, 3).reshape(g, tc * hch, tc * hch)

    idx_f = jnp.arange(tc - 1, -1, -1)
    idx_b = jnp.arange(tc)

    def st(d, idx):
        wr = pwr[d][..., idx][:, :, :, None] * bbr[d][:, :, None, :] - pwi[d][..., idx][:, :, :, None] * bbi[d][:, :, None, :]
        wi = pwr[d][..., idx][:, :, :, None] * bbi[d][:, :, None, :] + pwi[d][..., idx][:, :, :, None] * bbr[d][:, :, None, :]
        return (wr.transpose(0, 2, 3, 1).reshape(g, tc * hch, p),
                wi.transpose(0, 2, 3, 1).reshape(g, tc * hch, p))

    sfr, sfi = st(0, idx_f)
    sbr, sbi = st(1, idx_b)
    w_st = jnp.concatenate([sfr, sbr, sfi, sbi], axis=-1)

    def so(d, idx):
        wr = cr[d][:, None, :, :] * pwr[d][..., idx].transpose(0, 2, 1)[:, :, None, :] \
            - ci[d][:, None, :, :] * pwi[d][..., idx].transpose(0, 2, 1)[:, :, None, :]
        wi = cr[d][:, None, :, :] * pwi[d][..., idx].transpose(0, 2, 1)[:, :, None, :] \
            + ci[d][:, None, :, :] * pwr[d][..., idx].transpose(0, 2, 1)[:, :, None, :]
        return (wr.reshape(g, tc * hch, p).transpose(0, 2, 1),
                wi.reshape(g, tc * hch, p).transpose(0, 2, 1))

    ofr, ofi = so(0, jnp.arange(1, tc + 1))
    obr, obi = so(1, jnp.arange(tc, 0, -1))
    zero = jnp.zeros_like(ofr)
    w_of = jnp.concatenate([ofr, zero, -ofi, zero], axis=1)
    w_ob = jnp.concatenate([zero, obr, zero, -obi], axis=1)

    a_re = jnp.concatenate([pwr[0][..., tc], pwr[1][..., tc]], axis=-1)
    a_im = jnp.concatenate([pwi[0][..., tc], pwi[1][..., tc]], axis=-1)

    gs = S5_SET // hch
    ns = g // gs
    kd = tc * hch
    w_o = w_of + w_ob
    return (m.astype(BF16).reshape(ns, gs, kd, kd),
            w_st.astype(BF16).reshape(ns, gs, kd, 4 * p),
            w_o.astype(BF16).reshape(ns, gs, 4 * p, kd),
            a_re.reshape(ns, gs, 2 * p), a_im.reshape(ns, gs, 2 * p))


def _chunk_perm(tc, gs, hch):
    n = tc * gs * hch
    r = jnp.arange(n, dtype=jnp.int32)
    t, g, j = r // (gs * hch), (r // hch) % gs, r % hch
    dst = g * (tc * hch) + t * hch + j
    fwd = (dst[:, None] == r[None, :]).astype(BF16)
    return fwd, fwd.T


def _s5_kernel(zx_ref, zc_ref, pf_ref, pb_ref, m_ref, wst_ref, wo_ref, are_ref, aim_ref, ox_ref, oc_ref,
               sh_ref, xb_ref, yb_ref, *, ctx_chunks, x_chunks, pitch):
    tc = SSM_CHUNK
    nch = ctx_chunks + x_chunks
    lanes = zx_ref.shape[1]
    gs, kd, nst = wst_ref.shape
    for t in range(tc):
        xb_ref[0:ctx_chunks, t * lanes:(t + 1) * lanes] = zc_ref[pl.ds(t, ctx_chunks, stride=tc), :].astype(BF16)
        xb_ref[ctx_chunks:nch, t * lanes:(t + 1) * lanes] = zx_ref[pl.ds(t, x_chunks, stride=tc), :].astype(BF16)
    xb_ref[...] = jnp.dot(xb_ref[...], pf_ref[...], preferred_element_type=F32).astype(BF16)

    nslab = gs * nst // lanes
    half = nslab // 2
    for j in range(gs):
        s = jnp.dot(xb_ref[:, j * kd:(j + 1) * kd], wst_ref[j], preferred_element_type=F32)
        sh_ref[j * pitch:j * pitch + nch, :] = s[:, 0:lanes]
        sh_ref[(half + j) * pitch:(half + j) * pitch + nch, :] = s[:, lanes:2 * lanes]

    fwd = lax.broadcasted_iota(jnp.int32, (half, lanes), 1) < (lanes // 2)
    a_re = are_ref[...]
    a_im = aim_ref[...]

    def rows(c, part):
        return pl.ds(part * half * pitch + c, half, stride=pitch)

    def step(i, carry):
        hr, hi = carry
        cf = i
        cb = jnp.where(i < ctx_chunks, ctx_chunks - 1 - i, nch - 1 + ctx_chunks - i)
        same = cf == cb
        sfr, sfi = sh_ref[rows(cf, 0), :], sh_ref[rows(cf, 1), :]
        sbr, sbi = sh_ref[rows(cb, 0), :], sh_ref[rows(cb, 1), :]
        sh_ref[rows(cf, 0), :] = jnp.where(fwd, hr, sfr)
        sh_ref[rows(cf, 1), :] = jnp.where(fwd, hi, sfi)
        sh_ref[rows(cb, 0), :] = jnp.where(fwd, jnp.where(same, hr, sbr), hr)
        sh_ref[rows(cb, 1), :] = jnp.where(fwd, jnp.where(same, hi, sbi), hi)
        sr = jnp.where(fwd, sfr, sbr)
        si = jnp.where(fwd, sfi, sbi)
        return a_re * hr - a_im * hi + sr, a_re * hi + a_im * hr + si

    zero = jnp.zeros((half, lanes), F32)
    lax.fori_loop(0, nch, step, (zero, zero))

    for j in range(gs):
        hb = jnp.concatenate([sh_ref[j * pitch:j * pitch + nch, :],
                              sh_ref[(half + j) * pitch:(half + j) * pitch + nch, :]], axis=1).astype(BF16)
        yj = (jnp.dot(xb_ref[:, j * kd:(j + 1) * kd], m_ref[j], preferred_element_type=F32)
              + jnp.dot(hb, wo_ref[j], preferred_element_type=F32))
        yb_ref[:, j * kd:(j + 1) * kd] = yj.astype(BF16)
    y = jnp.dot(yb_ref[...], pb_ref[...], preferred_element_type=F32)
    for t in range(tc):
        oc_ref[pl.ds(t, ctx_chunks, stride=tc), :] = y[0:ctx_chunks, t * lanes:(t + 1) * lanes]
        ox_ref[pl.ds(t, x_chunks, stride=tc), :] = y[ctx_chunks:nch, t * lanes:(t + 1) * lanes]


def _s5_scan(zs, tables, bsz, seq, n_ctx):
    m, w_st, w_o, a_re, a_im = tables
    ns, gs, kd, nst = w_st.shape
    tc = SSM_CHUNK
    lanes = S5_SET
    kset = gs * kd
    ctx_chunks, x_chunks = n_ctx // tc, seq // tc
    nch = ctx_chunks + x_chunks
    pitch = -(-nch // 8) * 8
    if (pitch // 8) % 2 == 0:
        pitch += 8
    ctx0 = bsz * seq // n_ctx
    perm_f, perm_b = _chunk_perm(tc, gs, kd // tc)
    kern = functools.partial(_s5_kernel, ctx_chunks=ctx_chunks, x_chunks=x_chunks, pitch=pitch)
    per_set = lambda arr: pl.BlockSpec((None,) + arr.shape[1:], lambda s, b: (s,) + (0,) * (arr.ndim - 1))
    return pl.pallas_call(
        kern,
        grid=(ns, bsz),
        in_specs=[pl.BlockSpec((seq, lanes), lambda s, b: (b, s)),
                  pl.BlockSpec((n_ctx, lanes), lambda s, b: (ctx0 + b, s)),
                  pl.BlockSpec((kset, kset), lambda s, b: (0, 0)),
                  pl.BlockSpec((kset, kset), lambda s, b: (0, 0)),
                  per_set(m), per_set(w_st), per_set(w_o), per_set(a_re), per_set(a_im)],
        out_specs=[pl.BlockSpec((seq, lanes), lambda s, b: (b, s)),
                   pl.BlockSpec((n_ctx, lanes), lambda s, b: (b, s))],
        out_shape=[jax.ShapeDtypeStruct((bsz * seq, zs.shape[1]), F32),
                   jax.ShapeDtypeStruct((bsz * n_ctx, zs.shape[1]), F32)],
        scratch_shapes=[pltpu.VMEM((gs * nst // lanes * pitch, lanes), F32),
                        pltpu.VMEM((nch, kset), BF16),
                        pltpu.VMEM((nch, kset), BF16)],
        compiler_params=_cp(("parallel", "parallel")),
        name="s5_scan",
    )(zs, zs, perm_f, perm_b, m, w_st, w_o, a_re, a_im)


def _glu_kernel(yx_ref, yc_ref, w_ref, b_ref, gain_ref, o_ref, *, nx_tiles):
    y = jnp.where(pl.program_id(0) < nx_tiles, yx_ref[...], yc_ref[...])
    g = jax.nn.gelu(y)
    v = g * jax.nn.sigmoid(jnp.dot(g.astype(BF16), w_ref[...], preferred_element_type=F32) + b_ref[...])
    r = lax.rsqrt(jnp.mean(v * v, axis=-1, keepdims=True) + EPS)
    o_ref[...] = (v * r * gain_ref[...]).astype(o_ref.dtype)


def _glu(yx, yc, w, b, gain):
    ds = yx.shape[1]
    tm = 512
    nx_tiles, nc_tiles = yx.shape[0] // tm, yc.shape[0] // tm
    return pl.pallas_call(
        functools.partial(_glu_kernel, nx_tiles=nx_tiles),
        grid=(nx_tiles + nc_tiles,),
        in_specs=[pl.BlockSpec((tm, ds), lambda i: (jnp.minimum(i, nx_tiles - 1), 0)),
                  pl.BlockSpec((tm, ds), lambda i: (jnp.maximum(i - nx_tiles, 0), 0)),
                  pl.BlockSpec((ds, ds), lambda i: (0, 0)),
                  pl.BlockSpec((1, ds), lambda i: (0, 0)),
                  pl.BlockSpec((1, ds), lambda i: (0, 0))],
        out_specs=pl.BlockSpec((tm, ds), lambda i: (i, 0)),
        out_shape=jax.ShapeDtypeStruct((yx.shape[0] + yc.shape[0], ds), BF16),
        compiler_params=_cp(("parallel",)),
        name="glu",
    )(yx, yc, w, b, gain)


def _route(scores, sel, n_groups):
    epg = len(sel) // n_groups
    gscore = []
    for q in range(n_groups):
        v = sel[q * epg:(q + 1) * epg]
        best = None
        for a in range(epg):
            for b in range(a + 1, epg):
                s = v[a] + v[b]
                best = s if best is None else jnp.maximum(best, s)
        gscore.append(best)
    gbest = gscore[0]
    gidx = jnp.zeros(gbest.shape, jnp.int32)
    for q in range(1, n_groups):
        upd = gscore[q] > gbest
        gbest = jnp.where(upd, gscore[q], gbest)
        gidx = jnp.where(upd, q, gidx)
    vin = list(sel[:epg])
    sin = list(scores[:epg])
    for q in range(1, n_groups):
        pick = gidx == q
        for j in range(epg):
            vin[j] = jnp.where(pick, sel[q * epg + j], vin[j])
            sin[j] = jnp.where(pick, scores[q * epg + j], sin[j])
    b1 = vin[0]
    i1 = jnp.zeros(gbest.shape, jnp.int32)
    for j in range(1, epg):
        upd = vin[j] > b1
        b1 = jnp.where(upd, vin[j], b1)
        i1 = jnp.where(upd, j, i1)
    b2 = vin[0]
    i2 = jnp.zeros(gbest.shape, jnp.int32)
    have = jnp.zeros(gbest.shape, jnp.bool_)
    for j in range(epg):
        cand = i1 != j
        upd = cand & (jnp.logical_not(have) | (vin[j] > b2))
        b2 = jnp.where(upd, vin[j], b2)
        i2 = jnp.where(upd, j, i2)
        have = have | cand
    s1 = sin[0]
    s2 = sin[0]
    for j in range(1, epg):
        s1 = jnp.where(i1 == j, sin[j], s1)
        s2 = jnp.where(i2 == j, sin[j], s2)
    tot = s1 + s2
    return (gidx * epg + i1, gidx * epg + i2), (s1 / tot, s2 / tot)


def _merge_kernel(t_ref, yf_ref, ys_ref, wo_ref, g1_ref, n2_ref, sc2_ref, sh2_ref, rwh_ref, rwl_ref, rb_ref,
                  tn_ref, h3_ref, eidx_ref, ew_ref):
    df = yf_ref.shape[1]
    tm, d = t_ref.shape
    o = (jnp.dot(yf_ref[...], wo_ref[0:df, :], preferred_element_type=F32)
         + jnp.dot(ys_ref[...], wo_ref[df:, :], preferred_element_type=F32))
    tn = t_ref[...] + g1_ref[0] * o
    tn_ref[...] = tn
    h2 = tn * lax.rsqrt(jnp.mean(tn * tn, axis=-1, keepdims=True) + EPS) * n2_ref[...]
    h2 = h2 * (1.0 + sc2_ref[0]) + sh2_ref[0]
    nsub = d // LANES
    for c in range(nsub):
        h3_ref[pl.ds(c, tm, stride=nsub), :] = h2[:, c * LANES:(c + 1) * LANES]
    h_hi = h2.astype(BF16)
    h_lo = (h2 - h_hi.astype(F32)).astype(BF16)
    lg = (jnp.dot(h_hi, rwh_ref[...], preferred_element_type=F32)
          + jnp.dot(h_lo, rwh_ref[...], preferred_element_type=F32)
          + jnp.dot(h_hi, rwl_ref[...], preferred_element_type=F32))
    ne = rb_ref.shape[0]
    logits = lg.T[0:ne, :]
    scores = jax.nn.sigmoid(logits)
    sel = scores + rb_ref[...]
    srows = [scores[e:e + 1, :] for e in range(ne)]
    vrows = [sel[e:e + 1, :] for e in range(ne)]
    (e1, e2), (w1, w2) = _route(srows, vrows, N_EXPERT_GROUPS)
    eidx_ref[...] = jnp.zeros_like(eidx_ref)
    ew_ref[...] = jnp.zeros_like(ew_ref)
    eidx_ref[0:1, :] = e1
    eidx_ref[1:2, :] = e2
    ew_ref[0:1, :] = w1
    ew_ref[1:2, :] = w2


def _merge(t, yf, ys, wo, g1, n2, sc2, sh2, rwh, rwl, rb, n_x_rows, seq):
    nt, d = t.shape
    df = yf.shape[1]
    ne = rb.shape[0]
    nsub = d // LANES
    tm = ROW_TILE
    mrow = functools.partial(_mod_row, tm=tm, n_x_rows=n_x_rows, seq=seq)
    mspec = pl.BlockSpec((1, 1, d), lambda i: (mrow(i), 0, 0))
    return pl.pallas_call(
        _merge_kernel,
        grid=(nt // tm,),
        in_specs=[pl.BlockSpec((tm, d), lambda i: (i, 0)),
                  pl.BlockSpec((tm, df), lambda i: (i, 0)),
                  pl.BlockSpec((tm, d - df), lambda i: (i, 0)),
                  pl.BlockSpec((d, d), lambda i: (0, 0)),
                  mspec,
                  pl.BlockSpec((1, d), lambda i: (0, 0)),
                  mspec, mspec,
                  pl.BlockSpec((d, LANES), lambda i: (0, 0)),
                  pl.BlockSpec((d, LANES), lambda i: (0, 0)),
                  pl.BlockSpec((ne, 1), lambda i: (0, 0))],
        out_specs=[pl.BlockSpec((tm, d), lambda i: (i, 0)),
                   pl.BlockSpec((tm * nsub, LANES), lambda i: (i, 0)),
                   pl.BlockSpec((8, tm), lambda i: (0, i)),
                   pl.BlockSpec((8, tm), lambda i: (0, i))],
        out_shape=[jax.ShapeDtypeStruct((nt, d), F32),
                   jax.ShapeDtypeStruct((nt * nsub, LANES), F32),
                   jax.ShapeDtypeStruct((8, nt), jnp.int32),
                   jax.ShapeDtypeStruct((8, nt), F32)],
        compiler_params=_cp(("parallel",)),
        name="merge_route",
    )(t, yf, ys, wo, g1, n2, sc2, sh2, rwh, rwl, rb)


def _dispatch(eidx, ew, n_experts, rows, n_blocks):
    nt = eidx.shape[1]
    a = nt * TOP_K
    flat_e = eidx[:TOP_K].T.reshape(a)
    onehot = (flat_e[:, None] == jnp.arange(n_experts, dtype=jnp.int32)[None, :]).astype(jnp.int32)
    csum = jnp.cumsum(onehot, axis=0)
    counts = csum[-1]
    padded = (counts + rows - 1) // rows * rows
    pad_end = jnp.cumsum(padded)
    pad_start = pad_end - padded
    dest = jnp.sum(onehot * (csum - 1 + pad_start[None, :]), axis=1)
    tok = jnp.arange(a, dtype=jnp.int32) // TOP_K
    buf_tok = jnp.zeros((n_blocks * rows,), jnp.int32).at[dest].set(tok)
    n_valid = (pad_end[-1] // rows).astype(jnp.int32)
    blk_start = jnp.arange(n_blocks, dtype=jnp.int32) * rows
    blk_exp = jnp.sum((pad_end[None, :] <= blk_start[:, None]).astype(jnp.int32), axis=1)
    blk_exp = jnp.minimum(blk_exp, n_experts - 1)
    last_exp = jnp.sum(jnp.where(jnp.arange(n_blocks) == n_valid - 1, blk_exp, 0))
    blk_exp = jnp.where(jnp.arange(n_blocks) < n_valid, blk_exp, last_exp).astype(jnp.int32)
    return buf_tok, blk_exp, n_valid.reshape(1), dest.reshape(nt, TOP_K), ew[:TOP_K].T


def _expert_kernel(be_ref, nv_ref, tok_ref, tokn_ref, h_hbm, wg_hbm, wu_hbm, wd_hbm, o_ref,
                   wgb_ref, wub_ref, wdb_ref, stg_ref, std_ref, xg_ref, xb_ref, sem, wsem, *, nsub, layer):
    i = pl.program_id(0)
    rows = xb_ref.shape[0]
    valid = i < nv_ref[0]
    slot = i % 2
    e = be_ref[i]
    first = (i == 0) | (be_ref[jnp.maximum(i - 1, 0)] != e)

    def gather(toks, dst_slot):
        def issue(r, c):
            src = h_hbm.at[pl.ds(pl.multiple_of(toks[0, r] * nsub, nsub), nsub)]
            dst = xg_ref.at[dst_slot, pl.ds(pl.multiple_of(r * GATHER_PITCH, 8), nsub)]
            pltpu.make_async_copy(src, dst, sem.at[dst_slot]).start()
            return c

        lax.fori_loop(0, rows, issue, 0)

    @pl.when(i == 0)
    def _():
        gather(tok_ref, 0)

    @pl.when((i + 1 < nv_ref[0]) & (i + 1 < pl.num_programs(0)))
    def _():
        gather(tokn_ref, 1 - slot)

    @pl.when(valid & first)
    def _():
        chunks = []
        for src, dst, stg in ((wg_hbm, wgb_ref, stg_ref), (wu_hbm, wub_ref, stg_ref), (wd_hbm, wdb_ref, std_ref)):
            nr = stg.shape[1]
            chunks += [(src, dst, stg, k * nr, nr) for k in range(dst.shape[0] // nr)]

        def copy(k):
            src, _, stg, r0, nr = chunks[k]
            return pltpu.make_async_copy(src.at[layer, e, pl.ds(r0, nr), :], stg.at[k % 2], wsem.at[k % 2])

        copy(0).start()
        for k, (_, dst, stg, r0, nr) in enumerate(chunks):
            if k + 1 < len(chunks):
                copy(k + 1).start()
            copy(k).wait()
            dst[r0:r0 + nr, :] = stg[k % 2].astype(BF16)

    @pl.when(valid)
    def _():
        pltpu.make_async_copy(h_hbm.at[pl.ds(0, rows * nsub)], xg_ref.at[slot, pl.ds(0, rows * nsub)],
                              sem.at[slot]).wait()
        for s in range(2):
            @pl.when(slot == s)
            def _():
                for c in range(nsub):
                    xb_ref[:, c * LANES:(c + 1) * LANES] = (
                        xg_ref[s, pl.ds(c, rows, stride=GATHER_PITCH), :].astype(BF16))
        x = xb_ref[...]
        g = jnp.dot(x, wgb_ref[...], preferred_element_type=F32)
        u = jnp.dot(x, wub_ref[...], preferred_element_type=F32)
        hmid = (g * jax.nn.sigmoid(g)) * u
        y = jnp.dot(hmid.astype(BF16), wdb_ref[...], preferred_element_type=F32)
        for c in range(nsub):
            o_ref[pl.ds(c, rows, stride=nsub), :] = y[:, c * LANES:(c + 1) * LANES]

    @pl.when(jnp.logical_not(valid))
    def _():
        o_ref[...] = jnp.zeros_like(o_ref)


def _experts(h3, buf_tok, blk_exp, n_valid, w_gate, w_up, w_down, layer):
    _, ne, d, de = w_gate.shape
    nsub = d // LANES
    rows = MOE_ROWS
    n_blocks = buf_tok.shape[0] // rows
    any_spec = pl.BlockSpec(memory_space=pl.ANY)
    grid_spec = pltpu.PrefetchScalarGridSpec(
        num_scalar_prefetch=2,
        grid=(n_blocks,),
        in_specs=[pl.BlockSpec((None, 1, rows), lambda i, be, nv: (i, 0, 0), memory_space=pltpu.SMEM),
                  pl.BlockSpec((None, 1, rows), lambda i, be, nv: (jnp.minimum(i + 1, n_blocks - 1), 0, 0),
                               memory_space=pltpu.SMEM),
                  any_spec, any_spec, any_spec, any_spec],
        out_specs=pl.BlockSpec((rows * nsub, LANES), lambda i, be, nv: (i, 0)),
        scratch_shapes=[pltpu.VMEM((d, de), BF16),
                        pltpu.VMEM((d, de), BF16),
                        pltpu.VMEM((de, d), BF16),
                        pltpu.VMEM((2, d // MOE_WCHUNKS, de), F32),
                        pltpu.VMEM((2, de // MOE_WCHUNKS, d), F32),
                        pltpu.VMEM((2, rows * GATHER_PITCH, LANES), F32),
                        pltpu.VMEM((rows, d), BF16),
                        pltpu.SemaphoreType.DMA((2,)),
                        pltpu.SemaphoreType.DMA((2,))],
    )
    toks = buf_tok.reshape(n_blocks, 1, rows)
    return pl.pallas_call(
        functools.partial(_expert_kernel, nsub=nsub, layer=layer),
        grid_spec=grid_spec,
        out_shape=jax.ShapeDtypeStruct((n_blocks * rows * nsub, LANES), F32),
        compiler_params=_cp(("arbitrary",)),
        name="experts",
    )(blk_exp, n_valid, toks, toks, h3, w_gate, w_up, w_down)


def _combine_kernel(pos_ref, t_ref, w_ref, yb_hbm, g2_ref, o_ref, ga_ref, gb_ref, sem, *, nsub):
    tm = t_ref.shape[0]

    def issue(r, c):
        dst = pl.ds(pl.multiple_of(r * GATHER_PITCH, 8), nsub)
        for k, buf in enumerate((ga_ref, gb_ref)):
            src = pl.ds(pl.multiple_of(pos_ref[0, TOP_K * r + k] * nsub, nsub), nsub)
            pltpu.make_async_copy(yb_hbm.at[src], buf.at[dst], sem).start()
        return c

    lax.fori_loop(0, tm, issue, 0)
    for buf in (ga_ref, gb_ref):
        pltpu.make_async_copy(yb_hbm.at[pl.ds(0, tm * nsub)], buf.at[pl.ds(0, tm * nsub)], sem).wait()
    w0 = w_ref[:, 0:1]
    w1 = w_ref[:, 1:2]
    for c in range(nsub):
        sl = slice(c * LANES, (c + 1) * LANES)
        y = (ga_ref[pl.ds(c, tm, stride=GATHER_PITCH), :] * w0
             + gb_ref[pl.ds(c, tm, stride=GATHER_PITCH), :] * w1)
        o_ref[:, sl] = t_ref[:, sl] + g2_ref[0][:, sl] * y


def _combine(t, yb3, pos, w, g2, n_x_rows, seq):
    nt, d = t.shape
    nsub = d // LANES
    tm = ROW_TILE
    mrow = functools.partial(_mod_row, tm=tm, n_x_rows=n_x_rows, seq=seq)
    return pl.pallas_call(
        functools.partial(_combine_kernel, nsub=nsub),
        grid=(nt // tm,),
        in_specs=[pl.BlockSpec((None, 1, TOP_K * tm), lambda i: (i, 0, 0), memory_space=pltpu.SMEM),
                  pl.BlockSpec((tm, d), lambda i: (i, 0)),
                  pl.BlockSpec((tm, TOP_K), lambda i: (i, 0)),
                  pl.BlockSpec(memory_space=pl.ANY),
                  pl.BlockSpec((1, 1, d), lambda i: (mrow(i), 0, 0))],
        out_specs=pl.BlockSpec((tm, d), lambda i: (i, 0)),
        out_shape=jax.ShapeDtypeStruct((nt, d), F32),
        scratch_shapes=[pltpu.VMEM((tm * GATHER_PITCH, LANES), F32),
                        pltpu.VMEM((tm * GATHER_PITCH, LANES), F32),
                        pltpu.SemaphoreType.DMA],
        compiler_params=_cp(("arbitrary",)),
        name="combine",
    )(pos.reshape(nt // tm, 1, TOP_K * tm), t, w, yb3, g2)


def _final_kernel(x_ref, g_ref, o_ref):
    x = x_ref[...]
    o_ref[...] = x * lax.rsqrt(jnp.mean(x * x, axis=-1, keepdims=True) + EPS) * g_ref[...]


def _final_norm(t, g, n_rows):
    d = t.shape[1]
    tm = ROW_TILE
    return pl.pallas_call(
        _final_kernel,
        grid=(n_rows // tm,),
        in_specs=[pl.BlockSpec((tm, d), lambda i: (i, 0)),
                  pl.BlockSpec((1, d), lambda i: (0, 0))],
        out_specs=pl.BlockSpec((tm, d), lambda i: (i, 0)),
        out_shape=jax.ShapeDtypeStruct((n_rows, d), F32),
        compiler_params=_cp(("parallel",)),
        name="final_norm",
    )(t, g)


def kernel(x, c, ctx, c_ctx, w_mod, b_mod, norm1_g, norm2_g, w_in, w_out, fourier_w, mix_norm_g,
           lam_re, lam_im, log_dt, b_re, b_im, c_re, c_im, d_skip, glu_w, glu_b,
           router_w, router_b, w_gate, w_up, w_down, final_g):
    bsz, seq, d = x.shape
    n_ctx = ctx.shape[1]
    depth = w_mod.shape[0]
    df = fourier_w.shape[1] * fourier_w.shape[2]
    ds = d_skip.shape[1]
    ne = router_w.shape[1]
    nx_rows = bsz * seq
    nt = nx_rows + bsz * n_ctx

    t = jnp.concatenate([x.reshape(nx_rows, d), ctx.reshape(bsz * n_ctx, d)], axis=0).astype(F32)

    cvec = jnp.concatenate([c_ctx[None, :], c, jnp.zeros((8 - 1 - bsz, d), c.dtype)], axis=0).astype(F32)
    mod = _adaln(cvec, w_mod, b_mod).reshape(depth, 8, N_MOD, 1, d)

    cc, sc_ = _dft_tables(df)
    w_chan = jnp.concatenate([cc, sc_], axis=1).astype(BF16)
    clx, slx = _dft_tables(seq)
    clx, slx = clx.astype(BF16), (-slx).astype(BF16)
    clc, slc = _dft_tables(n_ctx)
    clc, slc = clc.astype(BF16), (-slc).astype(BF16)

    rw = jnp.pad(router_w.astype(F32), ((0, 0), (0, LANES - ne)))
    rwh = rw.astype(BF16)
    rwl = (rw - rwh.astype(F32)).astype(BF16)
    rb = router_b.astype(F32).reshape(ne, 1)
    n_blocks = -(-(nt * TOP_K) // MOE_ROWS) + ne

    for l in range(depth):
        sh1, sc1, g1, sh2, sc2, g2 = [mod[l, :, k] for k in range(N_MOD)]
        zf, zs = _inproj(t, norm1_g[l].reshape(1, d).astype(F32), sc1, sh1, w_in[l].astype(BF16),
                         df, nx_rows, seq)

        ab = _chan_dft(zf, w_chan)
        hw = fourier_w[l].astype(BF16)
        gain = mix_norm_g[l].astype(F32).reshape(1, -1)
        yf_x = _row_dft(ab, clx, slx, hw, gain[:, :df], 0, bsz, seq, 512, 512)
        yf_c = _row_dft(ab, clc, slc, hw, gain[:, :df], nx_rows, bsz, n_ctx, n_ctx, n_ctx)
        yf = jnp.concatenate([yf_x, yf_c], axis=0)

        tables = _s5_tables(lam_re[l], lam_im[l], log_dt[l], b_re[l], b_im[l], c_re[l], c_im[l], d_skip[l])
        ysx, ysc = _s5_scan(zs, tables, bsz, seq, n_ctx)
        ys = _glu(ysx, ysc, glu_w[l].astype(BF16), glu_b[l].astype(F32).reshape(1, ds), gain[:, df:])

        t, h3, eidx, ew = _merge(t, yf, ys, w_out[l].astype(BF16), g1,
                                 norm2_g[l].reshape(1, d).astype(F32), sc2, sh2, rwh, rwl, rb, nx_rows, seq)

        buf_tok, blk_exp, n_valid, pos, wtok = _dispatch(eidx, ew, ne, MOE_ROWS, n_blocks)
        yb = _experts(h3, buf_tok, blk_exp, n_valid, w_gate, w_up, w_down, l)
        t = _combine(t, yb, pos, wtok, g2, nx_rows, seq)

    out = _final_norm(t, final_g.reshape(1, d).astype(F32), nx_rows)
    return out.reshape(bsz, seq, d).astype(x.dtype)
```

```python
import functools
import math

import jax
import jax.numpy as jnp
from jax import lax
from jax.experimental import pallas as pl
from jax.experimental.pallas import tpu as pltpu

F32 = jnp.float32
BF16 = jnp.bfloat16
EPS = 1e-6

FOURIER_HEADS = 4
SSM_GROUP = 16
N_EXPERT_GROUPS = 4
TOP_K = 2
N_MOD = 6

SSM_CHUNK = 16
ROW_TILE = 256
MOE_ROWS = 256
MOE_WCHUNKS = 8
ROW_COPY_UNROLL = 8
LANES = 128
GATHER_PITCH = 24
S5_SET = LANES
VMEM_LIMIT_BYTES = 56 * 1024 * 1024


def _cp(sems):
    return pltpu.CompilerParams(dimension_semantics=sems, vmem_limit_bytes=VMEM_LIMIT_BYTES)


def _mod_row(i, tm, n_x_rows, seq):
    r = i * tm
    return jnp.where(r < n_x_rows, 1 + r // seq, 0)


def _mod_kernel(c_ref, w_ref, b_ref, o_ref):
    c = c_ref[...]
    s = c * jax.nn.sigmoid(c)
    o_ref[...] = jnp.dot(s.astype(BF16), w_ref[...].astype(BF16),
                         preferred_element_type=F32) + b_ref[...]


def _adaln(cvec, w_mod, b_mod):
    depth, d, n = w_mod.shape
    tn = 1024
    return pl.pallas_call(
        _mod_kernel,
        grid=(depth, n // tn),
        in_specs=[pl.BlockSpec((8, d), lambda l, j: (0, 0)),
                  pl.BlockSpec((None, d, tn), lambda l, j: (l, 0, j)),
                  pl.BlockSpec((None, 1, tn), lambda l, j: (l, 0, j))],
        out_specs=pl.BlockSpec((None, 8, tn), lambda l, j: (l, 0, j)),
        out_shape=jax.ShapeDtypeStruct((depth, 8, n), F32),
        compiler_params=_cp(("parallel", "parallel")),
        name="adaln",
    )(cvec, w_mod, b_mod.reshape(depth, 1, n))


def _inproj_kernel(t_ref, g_ref, sc_ref, sh_ref, w_ref, zf_ref, zs_ref):
    x = t_ref[...]
    h = x * lax.rsqrt(jnp.mean(x * x, axis=-1, keepdims=True) + EPS) * g_ref[...]
    h = h * (1.0 + sc_ref[0]) + sh_ref[0]
    z = jnp.dot(h.astype(BF16), w_ref[...], preferred_element_type=F32)
    df = zf_ref.shape[1]
    zf_ref[...] = z[:, :df].astype(zf_ref.dtype)
    zs_ref[...] = z[:, df:]


def _inproj(t, g, sc, sh, w, df, n_x_rows, seq):
    nt, d = t.shape
    n = w.shape[1]
    tm = ROW_TILE
    mrow = functools.partial(_mod_row, tm=tm, n_x_rows=n_x_rows, seq=seq)
    return pl.pallas_call(
        _inproj_kernel,
        grid=(nt // tm,),
        in_specs=[pl.BlockSpec((tm, d), lambda i: (i, 0)),
                  pl.BlockSpec((1, d), lambda i: (0, 0)),
                  pl.BlockSpec((1, 1, d), lambda i: (mrow(i), 0, 0)),
                  pl.BlockSpec((1, 1, d), lambda i: (mrow(i), 0, 0)),
                  pl.BlockSpec((d, n), lambda i: (0, 0))],
        out_specs=[pl.BlockSpec((tm, df), lambda i: (i, 0)),
                   pl.BlockSpec((tm, n - df), lambda i: (i, 0))],
        out_shape=[jax.ShapeDtypeStruct((nt, df), BF16),
                   jax.ShapeDtypeStruct((nt, n - df), F32)],
        compiler_params=_cp(("parallel",)),
        name="inproj",
    )(t, g, sc, sh, w)


def _dft_tables(n):
    r = 1
    while r * r < n:
        r *= 2
    q = n // r
    k = jnp.arange(n, dtype=jnp.int32)[:, None]
    step = 2.0 * math.pi / n
    pa = ((k * (jnp.arange(q, dtype=jnp.int32)[None, :] * r)) % n).astype(F32) * step
    pb = ((k * jnp.arange(r, dtype=jnp.int32)[None, :]) % n).astype(F32) * step
    ca, sa = jnp.cos(pa)[:, :, None], jnp.sin(pa)[:, :, None]
    cb, sb = jnp.cos(pb)[:, None, :], jnp.sin(pb)[:, None, :]
    c = (ca * cb - sa * sb).reshape(n, n)
    s = (sa * cb + ca * sb).reshape(n, n)
    return c, s


def _chan_dft_kernel(z_ref, w_ref, o_ref):
    o_ref[...] = jnp.dot(z_ref[...], w_ref[...], preferred_element_type=F32).astype(o_ref.dtype)


def _chan_dft(z, w):
    nt = z.shape[0]
    df, n = w.shape
    tm = 512
    return pl.pallas_call(
        _chan_dft_kernel,
        grid=(nt // tm,),
        in_specs=[pl.BlockSpec((tm, df), lambda i: (i, 0)),
                  pl.BlockSpec((df, n), lambda i: (0, 0))],
        out_specs=pl.BlockSpec((tm, n), lambda i: (i, 0)),
        out_shape=jax.ShapeDtypeStruct((nt, n), BF16),
        compiler_params=_cp(("parallel",)),
        name="chan_dft",
    )(z, w)


def _row_dft_kernel(cl_ref, sl_ref, a_ref, b_ref, hw_ref, gain_ref, o_ref, acc_ref, *, scale):
    k = pl.program_id(2)

    @pl.when(k == 0)
    def _():
        acc_ref[...] = jnp.zeros_like(acc_ref)

    acc_ref[...] += (jnp.dot(cl_ref[...], a_ref[...], preferred_element_type=F32)
                     + jnp.dot(sl_ref[...], b_ref[...], preferred_element_type=F32))

    @pl.when(k == pl.num_programs(2) - 1)
    def _():
        f = acc_ref[...] * scale
        nh, hd, _ = hw_ref.shape
        ys = [jnp.dot(f[:, h * hd:(h + 1) * hd].astype(BF16), hw_ref[h],
                      preferred_element_type=F32) for h in range(nh)]
        ss = ys[0] * ys[0]
        ssum = jnp.sum(ss, axis=-1, keepdims=True)
        for h in range(1, nh):
            ssum = ssum + jnp.sum(ys[h] * ys[h], axis=-1, keepdims=True)
        r = lax.rsqrt(ssum / (nh * hd) + EPS)
        for h in range(nh):
            o_ref[:, h * hd:(h + 1) * hd] = (ys[h] * r * gain_ref[:, h * hd:(h + 1) * hd]).astype(o_ref.dtype)


def _row_dft(ab, cl, sl_neg, hw, gain, row0, bsz, length, tm, tk):
    df = ab.shape[1] // 2
    kb = length // tk
    mb = length // tm
    off = row0 // tk
    scale = 1.0 / math.sqrt(length * df)
    return pl.pallas_call(
        functools.partial(_row_dft_kernel, scale=scale),
        grid=(bsz, mb, kb),
        in_specs=[pl.BlockSpec((tm, tk), lambda b, m, k: (m, k)),
                  pl.BlockSpec((tm, tk), lambda b, m, k: (m, k)),
                  pl.BlockSpec((tk, df), lambda b, m, k: (off + b * kb + k, 0)),
                  pl.BlockSpec((tk, df), lambda b, m, k: (off + b * kb + k, 1)),
                  pl.BlockSpec(hw.shape, lambda b, m, k: (0, 0, 0)),
                  pl.BlockSpec((1, df), lambda b, m, k: (0, 0))],
        out_specs=pl.BlockSpec((tm, df), lambda b, m, k: (b * mb + m, 0)),
        out_shape=jax.ShapeDtypeStruct((bsz * length, df), BF16),
        scratch_shapes=[pltpu.VMEM((tm, df), F32)],
        compiler_params=_cp(("parallel", "parallel", "arbitrary")),
        name="row_dft",
    )(cl, sl_neg, ab, ab, hw, gain)


def _s5_tables(lam_re, lam_im, log_dt, b_re, b_im, c_re, c_im, d_skip):
    tc = SSM_CHUNK
    g, p = lam_re.shape[1:]
    hch = b_re.shape[-1]
    lr, li = lam_re.astype(F32), lam_im.astype(F32)
    dt = jnp.exp(log_dt.astype(F32))[..., None]
    er, ei = lr * dt, li * dt

    def lpow(k):
        m = jnp.exp(er * k)
        return m * jnp.cos(ei * k), m * jnp.sin(ei * k)

    l1r, l1i = lpow(1.0)
    den = lr * lr + li * li
    qr = ((l1r - 1.0) * lr + l1i * li) / den
    qi = (l1i * lr - (l1r - 1.0) * li) / den
    br, bi = b_re.astype(F32), b_im.astype(F32)
    bbr = qr[..., None] * br - qi[..., None] * bi
    bbi = qr[..., None] * bi + qi[..., None] * br
    cr, ci = c_re.astype(F32), c_im.astype(F32)

    ks = jnp.arange(tc + 1, dtype=F32)
    pwr = jnp.stack([lpow(k)[0] for k in range(tc + 1)], axis=-1)
    pwi = jnp.stack([lpow(k)[1] for k in range(tc + 1)], axis=-1)
    del ks

    cbr = jnp.einsum("dgip,dgpj->dgpij", cr, bbr) - jnp.einsum("dgip,dgpj->dgpij", ci, bbi)
    cbi = jnp.einsum("dgip,dgpj->dgpij", cr, bbi) + jnp.einsum("dgip,dgpj->dgpij", ci, bbr)
    klag = (jnp.einsum("dgpij,dgpl->dglij", cbr, pwr[..., :tc])
            - jnp.einsum("dgpij,dgpl->dglij", cbi, pwi[..., :tc]))

    t_in = jnp.arange(tc)[:, None]
    t_out = jnp.arange(tc)[None, :]
    lag_f = t_out - t_in
    lag_b = t_in - t_out
    kf = klag[0][:, jnp.clip(lag_f, 0, tc - 1)] * (lag_f >= 0)[None, :, :, None, None]
    kb = klag[1][:, jnp.clip(lag_b, 0, tc - 1)] * (lag_b >= 0)[None, :, :, None, None]
    eye_t = (t_in == t_out).astype(F32)[None, :, :, None, None]
    dg = d_skip.astype(F32).reshape(g, hch)
    dmat = eye_t * (jnp.eye(hch, dtype=F32) * dg[:, None, :])[:, None, None, :, :]
    m = (kf + kb + dmat).transpose(0, 1, 4, 2, 3).reshape(g, tc * hch, tc * hch)

    idx_f = jnp.arange(tc - 1, -1, -1)
    idx_b = jnp.arange(tc)

    def st(d, idx):
        wr = pwr[d][..., idx][:, :, :, None] * bbr[d][:, :, None, :] - pwi[d][..., idx][:, :, :, None] * bbi[d][:, :, None, :]
        wi = pwr[d][..., idx][:, :, :, None] * bbi[d][:, :, None, :] + pwi[d][..., idx][:, :, :, None] * bbr[d][:, :, None, :]
        return (wr.transpose(0, 2, 3, 1).reshape(g, tc * hch, p),
                wi.transpose(0, 2, 3, 1).reshape(g, tc * hch, p))

    sfr, sfi = st(0, idx_f)
    sbr, sbi = st(1, idx_b)
    w_st = jnp.concatenate([sfr, sbr, sfi, sbi], axis=-1)

    def so(d, idx):
        wr = cr[d][:, None, :, :] * pwr[d][..., idx].transpose(0, 2, 1)[:, :, None, :] \
            - ci[d][:, None, :, :] * pwi[d][..., idx].transpose(0, 2, 1)[:, :, None, :]
        wi = cr[d][:, None, :, :] * pwi[d][..., idx].transpose(0, 2, 1)[:, :, None, :] \
            + ci[d][:, None, :, :] * pwr[d][..., idx].transpose(0, 2, 1)[:, :, None, :]
        return (wr.reshape(g, tc * hch, p).transpose(0, 2, 1),
                wi.reshape(g, tc * hch, p).transpose(0, 2, 1))

    ofr, ofi = so(0, jnp.arange(1, tc + 1))
    obr, obi = so(1, jnp.arange(tc, 0, -1))
    zero = jnp.zeros_like(ofr)
    w_of = jnp.concatenate([ofr, zero, -ofi, zero], axis=1)
    w_ob = jnp.concatenate([zero, obr, zero, -obi], axis=1)

    a_re = jnp.concatenate([pwr[0][..., tc], pwr[1][..., tc]], axis=-1)
    a_im = jnp.concatenate([pwi[0][..., tc], pwi[1][..., tc]], axis=-1)

    gs = S5_SET // hch
    ns = g // gs
    kd = tc * hch
    w_o = w_of + w_ob
    return (m.astype(BF16).reshape(ns, gs, kd, kd),
            w_st.astype(BF16).reshape(ns, gs, kd, 4 * p),
            w_o.astype(BF16).reshape(ns, gs, 4 * p, kd),
            a_re.reshape(ns, gs, 2 * p), a_im.reshape(ns, gs, 2 * p))


def _chunk_perm(tc, gs, hch):
    n = tc * gs * hch
    r = jnp.arange(n, dtype=jnp.int32)
    t, g, j = r // (gs * hch), (r // hch) % gs, r % hch
    dst = g * (tc * hch) + t * hch + j
    fwd = (dst[:, None] == r[None, :]).astype(BF16)
    return fwd, fwd.T


def _s5_kernel(zx_ref, zc_ref, pf_ref, pb_ref, m_ref, wst_ref, wo_ref, are_ref, aim_ref, ox_ref, oc_ref,
               sh_ref, xb_ref, yb_ref, *, ctx_chunks, x_chunks, pitch):
    tc = SSM_CHUNK
    nch = ctx_chunks + x_chunks
    lanes = zx_ref.shape[1]
    gs, kd, nst = wst_ref.shape
    for t in range(tc):
        xb_ref[0:ctx_chunks, t * lanes:(t + 1) * lanes] = zc_ref[pl.ds(t, ctx_chunks, stride=tc), :].astype(BF16)
        xb_ref[ctx_chunks:nch, t * lanes:(t + 1) * lanes] = zx_ref[pl.ds(t, x_chunks, stride=tc), :].astype(BF16)
    xb_ref[...] = jnp.dot(xb_ref[...], pf_ref[...], preferred_element_type=F32).astype(BF16)

    nslab = gs * nst // lanes
    half = nslab // 2
    for j in range(gs):
        s = jnp.dot(xb_ref[:, j * kd:(j + 1) * kd], wst_ref[j], preferred_element_type=F32)
        sh_ref[j * pitch:j * pitch + nch, :] = s[:, 0:lanes]
        sh_ref[(half + j) * pitch:(half + j) * pitch + nch, :] = s[:, lanes:2 * lanes]

    fwd = lax.broadcasted_iota(jnp.int32, (half, lanes), 1) < (lanes // 2)
    a_re = are_ref[...]
    a_im = aim_ref[...]

    def rows(c, part):
        return pl.ds(part * half * pitch + c, half, stride=pitch)

    def step(i, carry):
        hr, hi = carry
        cf = i
        cb = jnp.where(i < ctx_chunks, ctx_chunks - 1 - i, nch - 1 + ctx_chunks - i)
        same = cf == cb
        sfr, sfi = sh_ref[rows(cf, 0), :], sh_ref[rows(cf, 1), :]
        sbr, sbi = sh_ref[rows(cb, 0), :], sh_ref[rows(cb, 1), :]
        sh_ref[rows(cf, 0), :] = jnp.where(fwd, hr, sfr)
        sh_ref[rows(cf, 1), :] = jnp.where(fwd, hi, sfi)
        sh_ref[rows(cb, 0), :] = jnp.where(fwd, jnp.where(same, hr, sbr), hr)
        sh_ref[rows(cb, 1), :] = jnp.where(fwd, jnp.where(same, hi, sbi), hi)
        sr = jnp.where(fwd, sfr, sbr)
        si = jnp.where(fwd, sfi, sbi)
        return a_re * hr - a_im * hi + sr, a_re * hi + a_im * hr + si

    zero = jnp.zeros((half, lanes), F32)
    lax.fori_loop(0, nch, step, (zero, zero))

    for j in range(gs):
        hb = jnp.concatenate([sh_ref[j * pitch:j * pitch + nch, :],
                              sh_ref[(half + j) * pitch:(half + j) * pitch + nch, :]], axis=1).astype(BF16)
        yj = (jnp.dot(xb_ref[:, j * kd:(j + 1) * kd], m_ref[j], preferred_element_type=F32)
              + jnp.dot(hb, wo_ref[j], preferred_element_type=F32))
        yb_ref[:, j * kd:(j + 1) * kd] = yj.astype(BF16)
    y = jnp.dot(yb_ref[...], pb_ref[...], preferred_element_type=F32)
    for t in range(tc):
        oc_ref[pl.ds(t, ctx_chunks, stride=tc), :] = y[0:ctx_chunks, t * lanes:(t + 1) * lanes]
        ox_ref[pl.ds(t, x_chunks, stride=tc), :] = y[ctx_chunks:nch, t * lanes:(t + 1) * lanes]


def _s5_scan(zs, tables, bsz, seq, n_ctx):
    m, w_st, w_o, a_re, a_im = tables
    ns, gs, kd, nst = w_st.shape
    tc = SSM_CHUNK
    lanes = S5_SET
    kset = gs * kd
    ctx_chunks, x_chunks = n_ctx // tc, seq // tc
    nch = ctx_chunks + x_chunks
    pitch = -(-nch // 8) * 8
    if (pitch // 8) % 2 == 0:
        pitch += 8
    ctx0 = bsz * seq // n_ctx
    perm_f, perm_b = _chunk_perm(tc, gs, kd // tc)
    kern = functools.partial(_s5_kernel, ctx_chunks=ctx_chunks, x_chunks=x_chunks, pitch=pitch)
    per_set = lambda arr: pl.BlockSpec((None,) + arr.shape[1:], lambda s, b: (s,) + (0,) * (arr.ndim - 1))
    return pl.pallas_call(
        kern,
        grid=(ns, bsz),
        in_specs=[pl.BlockSpec((seq, lanes), lambda s, b: (b, s)),
                  pl.BlockSpec((n_ctx, lanes), lambda s, b: (ctx0 + b, s)),
                  pl.BlockSpec((kset, kset), lambda s, b: (0, 0)),
                  pl.BlockSpec((kset, kset), lambda s, b: (0, 0)),
                  per_set(m), per_set(w_st), per_set(w_o), per_set(a_re), per_set(a_im)],
        out_specs=[pl.BlockSpec((seq, lanes), lambda s, b: (b, s)),
                   pl.BlockSpec((n_ctx, lanes), lambda s, b: (b, s))],
        out_shape=[jax.ShapeDtypeStruct((bsz * seq, zs.shape[1]), F32),
                   jax.ShapeDtypeStruct((bsz * n_ctx, zs.shape[1]), F32)],
        scratch_shapes=[pltpu.VMEM((gs * nst // lanes * pitch, lanes), F32),
                        pltpu.VMEM((nch, kset), BF16),
                        pltpu.VMEM((nch, kset), BF16)],
        compiler_params=_cp(("parallel", "parallel")),
        name="s5_scan",
    )(zs, zs, perm_f, perm_b, m, w_st, w_o, a_re, a_im)


def _glu_kernel(yx_ref, yc_ref, w_ref, b_ref, gain_ref, o_ref, *, nx_tiles):
    y = jnp.where(pl.program_id(0) < nx_tiles, yx_ref[...], yc_ref[...])
    g = jax.nn.gelu(y)
    v = g * jax.nn.sigmoid(jnp.dot(g.astype(BF16), w_ref[...], preferred_element_type=F32) + b_ref[...])
    r = lax.rsqrt(jnp.mean(v * v, axis=-1, keepdims=True) + EPS)
    o_ref[...] = (v * r * gain_ref[...]).astype(o_ref.dtype)


def _glu(yx, yc, w, b, gain):
    ds = yx.shape[1]
    tm = 512
    nx_tiles, nc_tiles = yx.shape[0] // tm, yc.shape[0] // tm
    return pl.pallas_call(
        functools.partial(_glu_kernel, nx_tiles=nx_tiles),
        grid=(nx_tiles + nc_tiles,),
        in_specs=[pl.BlockSpec((tm, ds), lambda i: (jnp.minimum(i, nx_tiles - 1), 0)),
                  pl.BlockSpec((tm, ds), lambda i: (jnp.maximum(i - nx_tiles, 0), 0)),
                  pl.BlockSpec((ds, ds), lambda i: (0, 0)),
                  pl.BlockSpec((1, ds), lambda i: (0, 0)),
                  pl.BlockSpec((1, ds), lambda i: (0, 0))],
        out_specs=pl.BlockSpec((tm, ds), lambda i: (i, 0)),
        out_shape=jax.ShapeDtypeStruct((yx.shape[0] + yc.shape[0], ds), BF16),
        compiler_params=_cp(("parallel",)),
        name="glu",
    )(yx, yc, w, b, gain)


def _route(scores, sel, n_groups):
    epg = len(sel) // n_groups
    gscore = []
    for q in range(n_groups):
        v = sel[q * epg:(q + 1) * epg]
        best = None
        for a in range(epg):
            for b in range(a + 1, epg):
                s = v[a] + v[b]
                best = s if best is None else jnp.maximum(best, s)
        gscore.append(best)
    gbest = gscore[0]
    gidx = jnp.zeros(gbest.shape, jnp.int32)
    for q in range(1, n_groups):
        upd = gscore[q] > gbest
        gbest = jnp.where(upd, gscore[q], gbest)
        gidx = jnp.where(upd, q, gidx)
    vin = list(sel[:epg])
    sin = list(scores[:epg])
    for q in range(1, n_groups):
        pick = gidx == q
        for j in range(epg):
            vin[j] = jnp.where(pick, sel[q * epg + j], vin[j])
            sin[j] = jnp.where(pick, scores[q * epg + j], sin[j])
    b1 = vin[0]
    i1 = jnp.zeros(gbest.shape, jnp.int32)
    for j in range(1, epg):
        upd = vin[j] > b1
        b1 = jnp.where(upd, vin[j], b1)
        i1 = jnp.where(upd, j, i1)
    b2 = vin[0]
    i2 = jnp.zeros(gbest.shape, jnp.int32)
    have = jnp.zeros(gbest.shape, jnp.bool_)
    for j in range(epg):
        cand = i1 != j
        upd = cand & (jnp.logical_not(have) | (vin[j] > b2))
        b2 = jnp.where(upd, vin[j], b2)
        i2 = jnp.where(upd, j, i2)
        have = have | cand
    s1 = sin[0]
    s2 = sin[0]
    for j in range(1, epg):
        s1 = jnp.where(i1 == j, sin[j], s1)
        s2 = jnp.where(i2 == j, sin[j], s2)
    tot = s1 + s2
    return (gidx * epg + i1, gidx * epg + i2), (s1 / tot, s2 / tot)


def _merge_kernel(t_ref, yf_ref, ys_ref, wo_ref, g1_ref, n2_ref, sc2_ref, sh2_ref, rwh_ref, rwl_ref, rb_ref,
                  tn_ref, h3_ref, eidx_ref, ew_ref):
    df = yf_ref.shape[1]
    tm, d = t_ref.shape
    o = (jnp.dot(yf_ref[...], wo_ref[0:df, :], preferred_element_type=F32)
         + jnp.dot(ys_ref[...], wo_ref[df:, :], preferred_element_type=F32))
    tn = t_ref[...] + g1_ref[0] * o
    tn_ref[...] = tn
    h2 = tn * lax.rsqrt(jnp.mean(tn * tn, axis=-1, keepdims=True) + EPS) * n2_ref[...]
    h2 = h2 * (1.0 + sc2_ref[0]) + sh2_ref[0]
    nsub = d // LANES
    for c in range(nsub):
        h3_ref[pl.ds(c, tm, stride=nsub), :] = h2[:, c * LANES:(c + 1) * LANES]
    h_hi = h2.astype(BF16)
    h_lo = (h2 - h_hi.astype(F32)).astype(BF16)
    lg = (jnp.dot(h_hi, rwh_ref[...], preferred_element_type=F32)
          + jnp.dot(h_lo, rwh_ref[...], preferred_element_type=F32)
          + jnp.dot(h_hi, rwl_ref[...], preferred_element_type=F32))
    ne = rb_ref.shape[0]
    logits = lg.T[0:ne, :]
    scores = jax.nn.sigmoid(logits)
    sel = scores + rb_ref[...]
    srows = [scores[e:e + 1, :] for e in range(ne)]
    vrows = [sel[e:e + 1, :] for e in range(ne)]
    (e1, e2), (w1, w2) = _route(srows, vrows, N_EXPERT_GROUPS)
    eidx_ref[...] = jnp.zeros_like(eidx_ref)
    ew_ref[...] = jnp.zeros_like(ew_ref)
    eidx_ref[0:1, :] = e1
    eidx_ref[1:2, :] = e2
    ew_ref[0:1, :] = w1
    ew_ref[1:2, :] = w2


def _merge(t, yf, ys, wo, g1, n2, sc2, sh2, rwh, rwl, rb, n_x_rows, seq):
    nt, d = t.shape
    df = yf.shape[1]
    ne = rb.shape[0]
    nsub = d // LANES
    tm = ROW_TILE
    mrow = functools.partial(_mod_row, tm=tm, n_x_rows=n_x_rows, seq=seq)
    mspec = pl.BlockSpec((1, 1, d), lambda i: (mrow(i), 0, 0))
    return pl.pallas_call(
        _merge_kernel,
        grid=(nt // tm,),
        in_specs=[pl.BlockSpec((tm, d), lambda i: (i, 0)),
                  pl.BlockSpec((tm, df), lambda i: (i, 0)),
                  pl.BlockSpec((tm, d - df), lambda i: (i, 0)),
                  pl.BlockSpec((d, d), lambda i: (0, 0)),
                  mspec,
                  pl.BlockSpec((1, d), lambda i: (0, 0)),
                  mspec, mspec,
                  pl.BlockSpec((d, LANES), lambda i: (0, 0)),
                  pl.BlockSpec((d, LANES), lambda i: (0, 0)),
                  pl.BlockSpec((ne, 1), lambda i: (0, 0))],
        out_specs=[pl.BlockSpec((tm, d), lambda i: (i, 0)),
                   pl.BlockSpec((tm * nsub, LANES), lambda i: (i, 0)),
                   pl.BlockSpec((8, tm), lambda i: (0, i)),
                   pl.BlockSpec((8, tm), lambda i: (0, i))],
        out_shape=[jax.ShapeDtypeStruct((nt, d), F32),
                   jax.ShapeDtypeStruct((nt * nsub, LANES), F32),
                   jax.ShapeDtypeStruct((8, nt), jnp.int32),
                   jax.ShapeDtypeStruct((8, nt), F32)],
        compiler_params=_cp(("parallel",)),
        name="merge_route",
    )(t, yf, ys, wo, g1, n2, sc2, sh2, rwh, rwl, rb)


def _dispatch(eidx, ew, n_experts, rows, n_blocks):
    nt = eidx.shape[1]
    a = nt * TOP_K
    flat_e = eidx[:TOP_K].T.reshape(a)
    onehot = (flat_e[:, None] == jnp.arange(n_experts, dtype=jnp.int32)[None, :]).astype(jnp.int32)
    csum = jnp.cumsum(onehot, axis=0)
    counts = csum[-1]
    padded = (counts + rows - 1) // rows * rows
    pad_end = jnp.cumsum(padded)
    pad_start = pad_end - padded
    dest = jnp.sum(onehot * (csum - 1 + pad_start[None, :]), axis=1)
    tok = jnp.arange(a, dtype=jnp.int32) // TOP_K
    buf_tok = jnp.zeros((n_blocks * rows,), jnp.int32).at[dest].set(tok)
    n_valid = (pad_end[-1] // rows).astype(jnp.int32)
    blk_start = jnp.arange(n_blocks, dtype=jnp.int32) * rows
    blk_exp = jnp.sum((pad_end[None, :] <= blk_start[:, None]).astype(jnp.int32), axis=1)
    blk_exp = jnp.minimum(blk_exp, n_experts - 1)
    last_exp = jnp.sum(jnp.where(jnp.arange(n_blocks) == n_valid - 1, blk_exp, 0))
    blk_exp = jnp.where(jnp.arange(n_blocks) < n_valid, blk_exp, last_exp).astype(jnp.int32)
    return buf_tok, blk_exp, n_valid.reshape(1), dest.reshape(nt, TOP_K), ew[:TOP_K].T


def _start_row_copies(n_rows, src_hbm, src_row, dst, sem, nsub):
    def body(q, c):
        for k in range(ROW_COPY_UNROLL):
            r = q * ROW_COPY_UNROLL + k
            src = src_hbm.at[pl.ds(pl.multiple_of(src_row(r) * nsub, nsub), nsub)]
            pltpu.make_async_copy(src, dst.at[pl.ds(pl.multiple_of(r * GATHER_PITCH, 8), nsub)], sem).start()
        return c

    lax.fori_loop(0, n_rows // ROW_COPY_UNROLL, body, 0)


def _wait_row_copies(n_rows, src_hbm, dst, sem, nsub):
    pltpu.make_async_copy(src_hbm.at[pl.ds(0, n_rows * nsub)], dst.at[pl.ds(0, n_rows * nsub)], sem).wait()


def _expert_kernel(be_ref, nv_ref, tok_ref, tokn_ref, h_hbm, wg_hbm, wu_hbm, wd_hbm, o_ref,
                   wgb_ref, wub_ref, wdb_ref, stg_ref, std_ref, xg_ref, xb_ref, sem, wsem, *, nsub, layer):
    i = pl.program_id(0)
    rows = xb_ref.shape[0]
    valid = i < nv_ref[0]
    slot = i % 2
    e = be_ref[i]
    first = (i == 0) | (be_ref[jnp.maximum(i - 1, 0)] != e)

    def gather(toks, s):
        _start_row_copies(rows, h_hbm, lambda r: toks[0, r], xg_ref.at[s], sem.at[s], nsub)

    @pl.when(i == 0)
    def _():
        gather(tok_ref, 0)

    next_valid = (i + 1 < nv_ref[0]) & (i + 1 < pl.num_programs(0))
    for s in range(2):
        @pl.when(next_valid & (slot == 1 - s))
        def _():
            gather(tokn_ref, s)

    @pl.when(valid & first)
    def _():
        chunks = []
        for src, dst, stg in ((wg_hbm, wgb_ref, stg_ref), (wu_hbm, wub_ref, stg_ref), (wd_hbm, wdb_ref, std_ref)):
            nr = stg.shape[1]
            chunks += [(src, dst, stg, k * nr, nr) for k in range(dst.shape[0] // nr)]

        def copy(k):
            src, _, stg, r0, nr = chunks[k]
            return pltpu.make_async_copy(src.at[layer, e, pl.ds(r0, nr), :], stg.at[k % 2], wsem.at[k % 2])

        copy(0).start()
        for k, (_, dst, stg, r0, nr) in enumerate(chunks):
            if k + 1 < len(chunks):
                copy(k + 1).start()
            copy(k).wait()
            dst[r0:r0 + nr, :] = stg[k % 2].astype(BF16)

    @pl.when(valid)
    def _():
        for s in range(2):
            @pl.when(slot == s)
            def _():
                _wait_row_copies(rows, h_hbm, xg_ref.at[s], sem.at[s], nsub)
                for c in range(nsub):
                    xb_ref[:, c * LANES:(c + 1) * LANES] = (
                        xg_ref[s, pl.ds(c, rows, stride=GATHER_PITCH), :].astype(BF16))
        x = xb_ref[...]
        g = jnp.dot(x, wgb_ref[...], preferred_element_type=F32)
        u = jnp.dot(x, wub_ref[...], preferred_element_type=F32)
        hmid = (g * jax.nn.sigmoid(g)) * u
        y = jnp.dot(hmid.astype(BF16), wdb_ref[...], preferred_element_type=F32)
        for c in range(nsub):
            o_ref[pl.ds(c, rows, stride=nsub), :] = y[:, c * LANES:(c + 1) * LANES]

    @pl.when(jnp.logical_not(valid))
    def _():
        o_ref[...] = jnp.zeros_like(o_ref)


def _experts(h3, buf_tok, blk_exp, n_valid, w_gate, w_up, w_down, layer):
    _, ne, d, de = w_gate.shape
    nsub = d // LANES
    rows = MOE_ROWS
    n_blocks = buf_tok.shape[0] // rows
    any_spec = pl.BlockSpec(memory_space=pl.ANY)
    grid_spec = pltpu.PrefetchScalarGridSpec(
        num_scalar_prefetch=2,
        grid=(n_blocks,),
        in_specs=[pl.BlockSpec((None, 1, rows), lambda i, be, nv: (i, 0, 0), memory_space=pltpu.SMEM),
                  pl.BlockSpec((None, 1, rows), lambda i, be, nv: (jnp.minimum(i + 1, n_blocks - 1), 0, 0),
                               memory_space=pltpu.SMEM),
                  any_spec, any_spec, any_spec, any_spec],
        out_specs=pl.BlockSpec((rows * nsub, LANES), lambda i, be, nv: (i, 0)),
        scratch_shapes=[pltpu.VMEM((d, de), BF16),
                        pltpu.VMEM((d, de), BF16),
                        pltpu.VMEM((de, d), BF16),
                        pltpu.VMEM((2, d // MOE_WCHUNKS, de), F32),
                        pltpu.VMEM((2, de // MOE_WCHUNKS, d), F32),
                        pltpu.VMEM((2, rows * GATHER_PITCH, LANES), F32),
                        pltpu.VMEM((rows, d), BF16),
                        pltpu.SemaphoreType.DMA((2,)),
                        pltpu.SemaphoreType.DMA((2,))],
    )
    toks = buf_tok.reshape(n_blocks, 1, rows)
    return pl.pallas_call(
        functools.partial(_expert_kernel, nsub=nsub, layer=layer),
        grid_spec=grid_spec,
        out_shape=jax.ShapeDtypeStruct((n_blocks * rows * nsub, LANES), F32),
        compiler_params=_cp(("arbitrary",)),
        name="experts",
    )(blk_exp, n_valid, toks, toks, h3, w_gate, w_up, w_down)


def _combine_kernel(pos_ref, posn_ref, t_ref, w_ref, yb_hbm, g2_ref, o_ref, gk_ref, sem, *, nsub):
    i = pl.program_id(0)
    tm = t_ref.shape[0]
    slot = i % 2

    def gather(pos, s):
        for k in range(TOP_K):
            _start_row_copies(tm, yb_hbm, lambda r, k=k: pos[0, TOP_K * r + k], gk_ref.at[s, k], sem.at[s], nsub)

    @pl.when(i == 0)
    def _():
        gather(pos_ref, 0)

    for s in range(2):
        @pl.when((i + 1 < pl.num_programs(0)) & (slot == 1 - s))
        def _():
            gather(posn_ref, s)

    w0 = w_ref[:, 0:1]
    w1 = w_ref[:, 1:2]
    for s in range(2):
        @pl.when(slot == s)
        def _():
            for k in range(TOP_K):
                _wait_row_copies(tm, yb_hbm, gk_ref.at[s, k], sem.at[s], nsub)
            for c in range(nsub):
                sl = slice(c * LANES, (c + 1) * LANES)
                y = (gk_ref[s, 0, pl.ds(c, tm, stride=GATHER_PITCH), :] * w0
                     + gk_ref[s, 1, pl.ds(c, tm, stride=GATHER_PITCH), :] * w1)
                o_ref[:, sl] = t_ref[:, sl] + g2_ref[0][:, sl] * y


def _combine(t, yb3, pos, w, g2, n_x_rows, seq):
    nt, d = t.shape
    nsub = d // LANES
    tm = ROW_TILE
    mrow = functools.partial(_mod_row, tm=tm, n_x_rows=n_x_rows, seq=seq)
    nb = nt // tm
    posb = pos.reshape(nb, 1, TOP_K * tm)
    return pl.pallas_call(
        functools.partial(_combine_kernel, nsub=nsub),
        grid=(nb,),
        in_specs=[pl.BlockSpec((None, 1, TOP_K * tm), lambda i: (i, 0, 0), memory_space=pltpu.SMEM),
                  pl.BlockSpec((None, 1, TOP_K * tm), lambda i: (jnp.minimum(i + 1, nb - 1), 0, 0),
                               memory_space=pltpu.SMEM),
                  pl.BlockSpec((tm, d), lambda i: (i, 0)),
                  pl.BlockSpec((tm, TOP_K), lambda i: (i, 0)),
                  pl.BlockSpec(memory_space=pl.ANY),
                  pl.BlockSpec((1, 1, d), lambda i: (mrow(i), 0, 0))],
        out_specs=pl.BlockSpec((tm, d), lambda i: (i, 0)),
        out_shape=jax.ShapeDtypeStruct((nt, d), F32),
        scratch_shapes=[pltpu.VMEM((2, TOP_K, tm * GATHER_PITCH, LANES), F32),
                        pltpu.SemaphoreType.DMA((2,))],
        compiler_params=_cp(("arbitrary",)),
        name="combine",
    )(posb, posb, t, w, yb3, g2)


def _final_kernel(x_ref, g_ref, o_ref):
    x = x_ref[...]
    o_ref[...] = x * lax.rsqrt(jnp.mean(x * x, axis=-1, keepdims=True) + EPS) * g_ref[...]


def _final_norm(t, g, n_rows):
    d = t.shape[1]
    tm = ROW_TILE
    return pl.pallas_call(
        _final_kernel,
        grid=(n_rows // tm,),
        in_specs=[pl.BlockSpec((tm, d), lambda i: (i, 0)),
                  pl.BlockSpec((1, d), lambda i: (0, 0))],
        out_specs=pl.BlockSpec((tm, d), lambda i: (i, 0)),
        out_shape=jax.ShapeDtypeStruct((n_rows, d), F32),
        compiler_params=_cp(("parallel",)),
        name="final_norm",
    )(t, g)


def kernel(x, c, ctx, c_ctx, w_mod, b_mod, norm1_g, norm2_g, w_in, w_out, fourier_w, mix_norm_g,
           lam_re, lam_im, log_dt, b_re, b_im, c_re, c_im, d_skip, glu_w, glu_b,
           router_w, router_b, w_gate, w_up, w_down, final_g):
    bsz, seq, d = x.shape
    n_ctx = ctx.shape[1]
    depth = w_mod.shape[0]
    df = fourier_w.shape[1] * fourier_w.shape[2]
    ds = d_skip.shape[1]
    ne = router_w.shape[1]
    nx_rows = bsz * seq
    nt = nx_rows + bsz * n_ctx

    t = jnp.concatenate([x.reshape(nx_rows, d), ctx.reshape(bsz * n_ctx, d)], axis=0).astype(F32)

    cvec = jnp.concatenate([c_ctx[None, :], c, jnp.zeros((8 - 1 - bsz, d), c.dtype)], axis=0).astype(F32)
    mod = _adaln(cvec, w_mod, b_mod).reshape(depth, 8, N_MOD, 1, d)

    cc, sc_ = _dft_tables(df)
    w_chan = jnp.concatenate([cc, sc_], axis=1).astype(BF16)
    clx, slx = _dft_tables(seq)
    clx, slx = clx.astype(BF16), (-slx).astype(BF16)
    clc, slc = _dft_tables(n_ctx)
    clc, slc = clc.astype(BF16), (-slc).astype(BF16)

    rw = jnp.pad(router_w.astype(F32), ((0, 0), (0, LANES - ne)))
    rwh = rw.astype(BF16)
    rwl = (rw - rwh.astype(F32)).astype(BF16)
    rb = router_b.astype(F32).reshape(ne, 1)
    n_blocks = -(-(nt * TOP_K) // MOE_ROWS) + ne

    for l in range(depth):
        sh1, sc1, g1, sh2, sc2, g2 = [mod[l, :, k] for k in range(N_MOD)]
        zf, zs = _inproj(t, norm1_g[l].reshape(1, d).astype(F32), sc1, sh1, w_in[l].astype(BF16),
                         df, nx_rows, seq)

        ab = _chan_dft(zf, w_chan)
        hw = fourier_w[l].astype(BF16)
        gain = mix_norm_g[l].astype(F32).reshape(1, -1)
        yf_x = _row_dft(ab, clx, slx, hw, gain[:, :df], 0, bsz, seq, 512, 512)
        yf_c = _row_dft(ab, clc, slc, hw, gain[:, :df], nx_rows, bsz, n_ctx, n_ctx, n_ctx)
        yf = jnp.concatenate([yf_x, yf_c], axis=0)

        tables = _s5_tables(lam_re[l], lam_im[l], log_dt[l], b_re[l], b_im[l], c_re[l], c_im[l], d_skip[l])
        ysx, ysc = _s5_scan(zs, tables, bsz, seq, n_ctx)
        ys = _glu(ysx, ysc, glu_w[l].astype(BF16), glu_b[l].astype(F32).reshape(1, ds), gain[:, df:])

        t, h3, eidx, ew = _merge(t, yf, ys, w_out[l].astype(BF16), g1,
                                 norm2_g[l].reshape(1, d).astype(F32), sc2, sh2, rwh, rwl, rb, nx_rows, seq)

        buf_tok, blk_exp, n_valid, pos, wtok = _dispatch(eidx, ew, ne, MOE_ROWS, n_blocks)
        yb = _experts(h3, buf_tok, blk_exp, n_valid, w_gate, w_up, w_down, l)
        t = _combine(t, yb, pos, wtok, g2, nx_rows, seq)

    out = _final_norm(t, final_g.reshape(1, d).astype(F32), nx_rows)
    return out.reshape(bsz, seq, d).astype(x.dtype)
```

```python
import functools
import math

import jax
import jax.numpy as jnp
from jax import lax
from jax.experimental import pallas as pl
from jax.experimental.pallas import tpu as pltpu

F32 = jnp.float32
BF16 = jnp.bfloat16
EPS = 1e-6

FOURIER_HEADS = 4
SSM_GROUP = 16
N_EXPERT_GROUPS = 4
TOP_K = 2
N_MOD = 6

SSM_CHUNK = 16
ROW_TILE = 256
MOE_ROWS = 256
MOE_WCHUNKS = 8
ROW_COPY_UNROLL = 8
LANES = 128
GATHER_PITCH = 24
S5_SET = LANES
VMEM_LIMIT_BYTES = 56 * 1024 * 1024


def _cp(sems):
    return pltpu.CompilerParams(dimension_semantics=sems, vmem_limit_bytes=VMEM_LIMIT_BYTES)


def _mod_row(i, tm, n_x_rows, seq):
    r = i * tm
    return jnp.where(r < n_x_rows, 1 + r // seq, 0)


def _mod_kernel(c_ref, w_ref, b_ref, o_ref):
    c = c_ref[...]
    s = c * jax.nn.sigmoid(c)
    o_ref[...] = jnp.dot(s.astype(BF16), w_ref[...].astype(BF16),
                         preferred_element_type=F32) + b_ref[...]


def _adaln(cvec, w_mod, b_mod):
    depth, d, n = w_mod.shape
    tn = 1024
    return pl.pallas_call(
        _mod_kernel,
        grid=(depth, n // tn),
        in_specs=[pl.BlockSpec((8, d), lambda l, j: (0, 0)),
                  pl.BlockSpec((None, d, tn), lambda l, j: (l, 0, j)),
                  pl.BlockSpec((None, 1, tn), lambda l, j: (l, 0, j))],
        out_specs=pl.BlockSpec((None, 8, tn), lambda l, j: (l, 0, j)),
        out_shape=jax.ShapeDtypeStruct((depth, 8, n), F32),
        compiler_params=_cp(("parallel", "parallel")),
        name="adaln",
    )(cvec, w_mod, b_mod.reshape(depth, 1, n))


def _inproj_kernel(t_ref, g_ref, sc_ref, sh_ref, w_ref, zf_ref, zs_ref):
    x = t_ref[...]
    h = x * lax.rsqrt(jnp.mean(x * x, axis=-1, keepdims=True) + EPS) * g_ref[...]
    h = h * (1.0 + sc_ref[0]) + sh_ref[0]
    z = jnp.dot(h.astype(BF16), w_ref[...], preferred_element_type=F32)
    df = zf_ref.shape[1]
    zf_ref[...] = z[:, :df].astype(zf_ref.dtype)
    zs_ref[...] = z[:, df:]


def _inproj(t, g, sc, sh, w, df, n_x_rows, seq):
    nt, d = t.shape
    n = w.shape[1]
    tm = ROW_TILE
    mrow = functools.partial(_mod_row, tm=tm, n_x_rows=n_x_rows, seq=seq)
    return pl.pallas_call(
        _inproj_kernel,
        grid=(nt // tm,),
        in_specs=[pl.BlockSpec((tm, d), lambda i: (i, 0)),
                  pl.BlockSpec((1, d), lambda i: (0, 0)),
                  pl.BlockSpec((1, 1, d), lambda i: (mrow(i), 0, 0)),
                  pl.BlockSpec((1, 1, d), lambda i: (mrow(i), 0, 0)),
                  pl.BlockSpec((d, n), lambda i: (0, 0))],
        out_specs=[pl.BlockSpec((tm, df), lambda i: (i, 0)),
                   pl.BlockSpec((tm, n - df), lambda i: (i, 0))],
        out_shape=[jax.ShapeDtypeStruct((nt, df), BF16),
                   jax.ShapeDtypeStruct((nt, n - df), F32)],
        compiler_params=_cp(("parallel",)),
        name="inproj",
    )(t, g, sc, sh, w)


def _dft_tables(n):
    r = 1
    while r * r < n:
        r *= 2
    q = n // r
    k = jnp.arange(n, dtype=jnp.int32)[:, None]
    step = 2.0 * math.pi / n
    pa = ((k * (jnp.arange(q, dtype=jnp.int32)[None, :] * r)) % n).astype(F32) * step
    pb = ((k * jnp.arange(r, dtype=jnp.int32)[None, :]) % n).astype(F32) * step
    ca, sa = jnp.cos(pa)[:, :, None], jnp.sin(pa)[:, :, None]
    cb, sb = jnp.cos(pb)[:, None, :], jnp.sin(pb)[:, None, :]
    c = (ca * cb - sa * sb).reshape(n, n)
    s = (sa * cb + ca * sb).reshape(n, n)
    return c, s


def _chan_dft_kernel(z_ref, w_ref, o_ref):
    o_ref[...] = jnp.dot(z_ref[...], w_ref[...], preferred_element_type=F32).astype(o_ref.dtype)


def _chan_dft(z, w):
    nt = z.shape[0]
    df, n = w.shape
    tm = min(512, nt)
    return pl.pallas_call(
        _chan_dft_kernel,
        grid=(nt // tm,),
        in_specs=[pl.BlockSpec((tm, df), lambda i: (i, 0)),
                  pl.BlockSpec((df, n), lambda i: (0, 0))],
        out_specs=pl.BlockSpec((tm, n), lambda i: (i, 0)),
        out_shape=jax.ShapeDtypeStruct((nt, n), BF16),
        compiler_params=_cp(("parallel",)),
        name="chan_dft",
    )(z, w)


def _chan_sym_kernel(z_ref, zr_ref, w_ref, o_ref):
    z = z_ref[...].astype(F32)
    zr = zr_ref[...].astype(F32)
    c = w_ref.shape[0]
    o_ref[:, :c] = jnp.dot((z + zr).astype(BF16), w_ref[:, :c], preferred_element_type=F32).astype(o_ref.dtype)
    o_ref[:, c:] = jnp.dot((z - zr).astype(BF16), w_ref[:, c:], preferred_element_type=F32).astype(o_ref.dtype)


def _chan_sym(z, zr, w, row0, bsz, length):
    df, n = w.shape
    half = length // 2
    tm = min(512, half)
    hb = half // tm
    off = row0 // tm
    return pl.pallas_call(
        _chan_sym_kernel,
        grid=(bsz, hb),
        in_specs=[pl.BlockSpec((tm, df), lambda b, i: (off + b * 2 * hb + i, 0)),
                  pl.BlockSpec((tm, df), lambda b, i: (b * hb + i, 0)),
                  pl.BlockSpec((df, n), lambda b, i: (0, 0))],
        out_specs=pl.BlockSpec((tm, n), lambda b, i: (b * hb + i, 0)),
        out_shape=jax.ShapeDtypeStruct((bsz * half, n), BF16),
        compiler_params=_cp(("parallel", "parallel")),
        name="chan_sym",
    )(z, zr, w)


def _fourier_epilogue(f, hw_ref, gain_ref, o_ref):
    nh, hd, _ = hw_ref.shape
    ys = [jnp.dot(f[:, h * hd:(h + 1) * hd].astype(BF16), hw_ref[h],
                  preferred_element_type=F32) for h in range(nh)]
    ssum = jnp.sum(ys[0] * ys[0], axis=-1, keepdims=True)
    for h in range(1, nh):
        ssum = ssum + jnp.sum(ys[h] * ys[h], axis=-1, keepdims=True)
    r = lax.rsqrt(ssum / (nh * hd) + EPS)
    for h in range(nh):
        o_ref[:, h * hd:(h + 1) * hd] = (ys[h] * r * gain_ref[:, h * hd:(h + 1) * hd]).astype(o_ref.dtype)


def _row_sym_kernel(ch_ref, sh_ref, alt_ref, ae_ref, bo_ref, amid_ref, hw_ref, gain_ref, lo_ref, hi_ref,
                    p_ref, q_ref, r_ref, *, scale, mid_sign):
    m = pl.program_id(1)
    k = pl.program_id(2)

    @pl.when(k == 0)
    def _():
        p_ref[...] = jnp.zeros_like(p_ref)
        q_ref[...] = jnp.zeros_like(q_ref)
        r_ref[...] = jnp.zeros_like(r_ref)

    ae = ae_ref[...]
    p_ref[...] += jnp.dot(ch_ref[...], ae, preferred_element_type=F32)
    q_ref[...] += jnp.dot(sh_ref[...], bo_ref[...], preferred_element_type=F32)

    @pl.when(m == 0)
    def _():
        r_ref[...] += jnp.dot(alt_ref[...], ae, preferred_element_type=F32)

    @pl.when(k == pl.num_programs(2) - 1)
    def _():
        tm = p_ref.shape[0]
        row = lax.broadcasted_iota(jnp.int32, (tm, 1), 0) + m * tm
        amid = amid_ref[0:1, :]
        pf = p_ref[...] + jnp.where((row & 1) == 0, 1.0, -1.0) * amid
        q = q_ref[...]
        mid = r_ref[0:1, :] + mid_sign * amid
        hi = jnp.where(row == 0, mid, pf + q)
        _fourier_epilogue((pf - q) * scale, hw_ref, gain_ref, lo_ref)
        _fourier_epilogue(hi * scale, hw_ref, gain_ref, hi_ref)


def _row_sym(aebo, amid, ch, sh, alt, hw, gain, bsz, length):
    df = aebo.shape[1] // 2
    half = length // 2
    tm = tk = min(512, half)
    mb, kb = half // tm, half // tk
    scale = 1.0 / math.sqrt(length * df)
    mid_sign = 1.0 if half % 2 == 0 else -1.0
    out = jax.ShapeDtypeStruct((bsz * half, df), BF16)
    ospec = pl.BlockSpec((tm, df), lambda b, m, k: (b * mb + m, 0))
    return pl.pallas_call(
        functools.partial(_row_sym_kernel, scale=scale, mid_sign=mid_sign),
        grid=(bsz, mb, kb),
        in_specs=[pl.BlockSpec((tm, tk), lambda b, m, k: (m, k)),
                  pl.BlockSpec((tm, tk), lambda b, m, k: (m, k)),
                  pl.BlockSpec((8, tk), lambda b, m, k: (0, k)),
                  pl.BlockSpec((tk, df), lambda b, m, k: (b * kb + k, 0)),
                  pl.BlockSpec((tk, df), lambda b, m, k: (b * kb + k, 1)),
                  pl.BlockSpec((8, df), lambda b, m, k: (b, 0)),
                  pl.BlockSpec(hw.shape, lambda b, m, k: (0, 0, 0)),
                  pl.BlockSpec((1, df), lambda b, m, k: (0, 0))],
        out_specs=[ospec, ospec],
        out_shape=[out, out],
        scratch_shapes=[pltpu.VMEM((tm, df), F32), pltpu.VMEM((tm, df), F32), pltpu.VMEM((8, df), F32)],
        compiler_params=_cp(("parallel", "parallel", "arbitrary")),
        name="row_sym",
    )(ch, sh, alt, aebo, aebo, amid, hw, gain)


def _fourier_mix(zf, w_chan, tabs, hw, gain, row0, bsz, length):
    ch, sh, alt = tabs
    df = zf.shape[1]
    half = length // 2
    z3 = zf[row0:row0 + bsz * length].reshape(bsz, length, df)
    zr = jnp.concatenate([jnp.zeros((bsz, 1, df), zf.dtype), jnp.flip(z3[:, half + 1:], axis=1)], axis=1)
    aebo = _chan_sym(zf, zr.reshape(bsz * half, df), w_chan, row0, bsz, length)
    zmid = jnp.zeros((bsz, 8, df), zf.dtype).at[:, 0].set(z3[:, half])
    amid = _chan_dft(zmid.reshape(bsz * 8, df), w_chan)[:, :df].astype(F32)
    lo, hi = _row_sym(aebo, amid, ch, sh, alt, hw, gain, bsz, length)
    hi3 = hi.reshape(bsz, half, df)
    y = jnp.concatenate([lo.reshape(bsz, half, df), hi3[:, :1], jnp.flip(hi3[:, 1:], axis=1)], axis=1)
    return y.reshape(bsz * length, df)


def _half_tables(length):
    half = length // 2
    c, s = _dft_tables(length)
    n = jnp.arange(half, dtype=jnp.int32)
    alt = jnp.where((lax.broadcasted_iota(jnp.int32, (8, half), 0) == 0),
                    jnp.where(n % 2 == 0, 1.0, -1.0)[None, :], 0.0)
    return c[:half, :half].astype(BF16), s[:half, :half].astype(BF16), alt.astype(BF16)


def _s5_tables(lam_re, lam_im, log_dt, b_re, b_im, c_re, c_im, d_skip):
    tc = SSM_CHUNK
    g, p = lam_re.shape[1:]
    hch = b_re.shape[-1]
    lr, li = lam_re.astype(F32), lam_im.astype(F32)
    dt = jnp.exp(log_dt.astype(F32))[..., None]
    er, ei = lr * dt, li * dt

    def lpow(k):
        m = jnp.exp(er * k)
        return m * jnp.cos(ei * k), m * jnp.sin(ei * k)

    l1r, l1i = lpow(1.0)
    den = lr * lr + li * li
    qr = ((l1r - 1.0) * lr + l1i * li) / den
    qi = (l1i * lr - (l1r - 1.0) * li) / den
    br, bi = b_re.astype(F32), b_im.astype(F32)
    bbr = qr[..., None] * br - qi[..., None] * bi
    bbi = qr[..., None] * bi + qi[..., None] * br
    cr, ci = c_re.astype(F32), c_im.astype(F32)

    ks = jnp.arange(tc + 1, dtype=F32)
    pwr = jnp.stack([lpow(k)[0] for k in range(tc + 1)], axis=-1)
    pwi = jnp.stack([lpow(k)[1] for k in range(tc + 1)], axis=-1)
    del ks

    cbr = jnp.einsum("dgip,dgpj->dgpij", cr, bbr) - jnp.einsum("dgip,dgpj->dgpij", ci, bbi)
    cbi = jnp.einsum("dgip,dgpj->dgpij", cr, bbi) + jnp.einsum("dgip,dgpj->dgpij", ci, bbr)
    klag = (jnp.einsum("dgpij,dgpl->dglij", cbr, pwr[..., :tc])
            - jnp.einsum("dgpij,dgpl->dglij", cbi, pwi[..., :tc]))

    t_in = jnp.arange(tc)[:, None]
    t_out = jnp.arange(tc)[None, :]
    lag_f = t_out - t_in
    lag_b = t_in - t_out
    kf = klag[0][:, jnp.clip(lag_f, 0, tc - 1)] * (lag_f >= 0)[None, :, :, None, None]
    kb = klag[1][:, jnp.clip(lag_b, 0, tc - 1)] * (lag_b >= 0)[None, :, :, None, None]
    eye_t = (t_in == t_out).astype(F32)[None, :, :, None, None]
    dg = d_skip.astype(F32).reshape(g, hch)
    dmat = eye_t * (jnp.eye(hch, dtype=F32) * dg[:, None, :])[:, None, None, :, :]
    m = (kf + kb + dmat).transpose(0, 1, 4, 2, 3).reshape(g, tc * hch, tc * hch)

    idx_f = jnp.arange(tc - 1, -1, -1)
    idx_b = jnp.arange(tc)

    def st(d, idx):
        wr = pwr[d][..., idx][:, :, :, None] * bbr[d][:, :, None, :] - pwi[d][..., idx][:, :, :, None] * bbi[d][:, :, None, :]
        wi = pwr[d][..., idx][:, :, :, None] * bbi[d][:, :, None, :] + pwi[d][..., idx][:, :, :, None] * bbr[d][:, :, None, :]
        return (wr.transpose(0, 2, 3, 1).reshape(g, tc * hch, p),
                wi.transpose(0, 2, 3, 1).reshape(g, tc * hch, p))

    sfr, sfi = st(0, idx_f)
    sbr, sbi = st(1, idx_b)
    w_st = jnp.concatenate([sfr, sbr, sfi, sbi], axis=-1)

    def so(d, idx):
        wr = cr[d][:, None, :, :] * pwr[d][..., idx].transpose(0, 2, 1)[:, :, None, :] \
            - ci[d][:, None, :, :] * pwi[d][..., idx].transpose(0, 2, 1)[:, :, None, :]
        wi = cr[d][:, None, :, :] * pwi[d][..., idx].transpose(0, 2, 1)[:, :, None, :] \
            + ci[d][:, None, :, :] * pwr[d][..., idx].transpose(0, 2, 1)[:, :, None, :]
        return (wr.reshape(g, tc * hch, p).transpose(0, 2, 1),
                wi.reshape(g, tc * hch, p).transpose(0, 2, 1))

    ofr, ofi = so(0, jnp.arange(1, tc + 1))
    obr, obi = so(1, jnp.arange(tc, 0, -1))
    zero = jnp.zeros_like(ofr)
    w_of = jnp.concatenate([ofr, zero, -ofi, zero], axis=1)
    w_ob = jnp.concatenate([zero, obr, zero, -obi], axis=1)

    a_re = jnp.concatenate([pwr[0][..., tc], pwr[1][..., tc]], axis=-1)
    a_im = jnp.concatenate([pwi[0][..., tc], pwi[1][..., tc]], axis=-1)

    gs = S5_SET // hch
    ns = g // gs
    kd = tc * hch
    w_o = w_of + w_ob
    return (m.astype(BF16).reshape(ns, gs, kd, kd),
            w_st.astype(BF16).reshape(ns, gs, kd, 4 * p),
            w_o.astype(BF16).reshape(ns, gs, 4 * p, kd),
            a_re.reshape(ns, gs, 2 * p), a_im.reshape(ns, gs, 2 * p))


def _chunk_perm(tc, gs, hch):
    n = tc * gs * hch
    r = jnp.arange(n, dtype=jnp.int32)
    t, g, j = r // (gs * hch), (r // hch) % gs, r % hch
    dst = g * (tc * hch) + t * hch + j
    fwd = (dst[:, None] == r[None, :]).astype(BF16)
    return fwd, fwd.T


def _s5_kernel(zx_ref, zc_ref, pf_ref, pb_ref, m_ref, wst_ref, wo_ref, are_ref, aim_ref, ox_ref, oc_ref,
               sh_ref, xb_ref, yb_ref, *, ctx_chunks, x_chunks, pitch):
    tc = SSM_CHUNK
    nch = ctx_chunks + x_chunks
    lanes = zx_ref.shape[1]
    gs, kd, nst = wst_ref.shape
    for t in range(tc):
        xb_ref[0:ctx_chunks, t * lanes:(t + 1) * lanes] = zc_ref[pl.ds(t, ctx_chunks, stride=tc), :].astype(BF16)
        xb_ref[ctx_chunks:nch, t * lanes:(t + 1) * lanes] = zx_ref[pl.ds(t, x_chunks, stride=tc), :].astype(BF16)
    xb_ref[...] = jnp.dot(xb_ref[...], pf_ref[...], preferred_element_type=F32).astype(BF16)

    nslab = gs * nst // lanes
    half = nslab // 2
    for j in range(gs):
        s = jnp.dot(xb_ref[:, j * kd:(j + 1) * kd], wst_ref[j], preferred_element_type=F32)
        sh_ref[j * pitch:j * pitch + nch, :] = s[:, 0:lanes]
        sh_ref[(half + j) * pitch:(half + j) * pitch + nch, :] = s[:, lanes:2 * lanes]

    fwd = lax.broadcasted_iota(jnp.int32, (half, lanes), 1) < (lanes // 2)
    a_re = are_ref[...]
    a_im = aim_ref[...]

    def rows(c, part):
        return pl.ds(part * half * pitch + c, half, stride=pitch)

    def step(i, carry):
        hr, hi = carry
        cf = i
        cb = jnp.where(i < ctx_chunks, ctx_chunks - 1 - i, nch - 1 + ctx_chunks - i)
        same = cf == cb
        sfr, sfi = sh_ref[rows(cf, 0), :], sh_ref[rows(cf, 1), :]
        sbr, sbi = sh_ref[rows(cb, 0), :], sh_ref[rows(cb, 1), :]
        sh_ref[rows(cf, 0), :] = jnp.where(fwd, hr, sfr)
        sh_ref[rows(cf, 1), :] = jnp.where(fwd, hi, sfi)
        sh_ref[rows(cb, 0), :] = jnp.where(fwd, jnp.where(same, hr, sbr), hr)
        sh_ref[rows(cb, 1), :] = jnp.where(fwd, jnp.where(same, hi, sbi), hi)
        sr = jnp.where(fwd, sfr, sbr)
        si = jnp.where(fwd, sfi, sbi)
        return a_re * hr - a_im * hi + sr, a_re * hi + a_im * hr + si

    zero = jnp.zeros((half, lanes), F32)
    lax.fori_loop(0, nch, step, (zero, zero))

    for j in range(gs):
        hb = jnp.concatenate([sh_ref[j * pitch:j * pitch + nch, :],
                              sh_ref[(half + j) * pitch:(half + j) * pitch + nch, :]], axis=1).astype(BF16)
        yj = (jnp.dot(xb_ref[:, j * kd:(j + 1) * kd], m_ref[j], preferred_element_type=F32)
              + jnp.dot(hb, wo_ref[j], preferred_element_type=F32))
        yb_ref[:, j * kd:(j + 1) * kd] = yj.astype(BF16)
    y = jnp.dot(yb_ref[...], pb_ref[...], preferred_element_type=F32)
    for t in range(tc):
        oc_ref[pl.ds(t, ctx_chunks, stride=tc), :] = y[0:ctx_chunks, t * lanes:(t + 1) * lanes]
        ox_ref[pl.ds(t, x_chunks, stride=tc), :] = y[ctx_chunks:nch, t * lanes:(t + 1) * lanes]


def _s5_scan(zs, tables, bsz, seq, n_ctx):
    m, w_st, w_o, a_re, a_im = tables
    ns, gs, kd, nst = w_st.shape
    tc = SSM_CHUNK
    lanes = S5_SET
    kset = gs * kd
    ctx_chunks, x_chunks = n_ctx // tc, seq // tc
    nch = ctx_chunks + x_chunks
    pitch = -(-nch // 8) * 8
    if (pitch // 8) % 2 == 0:
        pitch += 8
    ctx0 = bsz * seq // n_ctx
    perm_f, perm_b = _chunk_perm(tc, gs, kd // tc)
    kern = functools.partial(_s5_kernel, ctx_chunks=ctx_chunks, x_chunks=x_chunks, pitch=pitch)
    per_set = lambda arr: pl.BlockSpec((None,) + arr.shape[1:], lambda s, b: (s,) + (0,) * (arr.ndim - 1))
    return pl.pallas_call(
        kern,
        grid=(ns, bsz),
        in_specs=[pl.BlockSpec((seq, lanes), lambda s, b: (b, s)),
                  pl.BlockSpec((n_ctx, lanes), lambda s, b: (ctx0 + b, s)),
                  pl.BlockSpec((kset, kset), lambda s, b: (0, 0)),
                  pl.BlockSpec((kset, kset), lambda s, b: (0, 0)),
                  per_set(m), per_set(w_st), per_set(w_o), per_set(a_re), per_set(a_im)],
        out_specs=[pl.BlockSpec((seq, lanes), lambda s, b: (b, s)),
                   pl.BlockSpec((n_ctx, lanes), lambda s, b: (b, s))],
        out_shape=[jax.ShapeDtypeStruct((bsz * seq, zs.shape[1]), F32),
                   jax.ShapeDtypeStruct((bsz * n_ctx, zs.shape[1]), F32)],
        scratch_shapes=[pltpu.VMEM((gs * nst // lanes * pitch, lanes), F32),
                        pltpu.VMEM((nch, kset), BF16),
                        pltpu.VMEM((nch, kset), BF16)],
        compiler_params=_cp(("parallel", "parallel")),
        name="s5_scan",
    )(zs, zs, perm_f, perm_b, m, w_st, w_o, a_re, a_im)


def _glu_kernel(yx_ref, yc_ref, w_ref, b_ref, gain_ref, o_ref, *, nx_tiles):
    y = jnp.where(pl.program_id(0) < nx_tiles, yx_ref[...], yc_ref[...])
    g = jax.nn.gelu(y)
    v = g * jax.nn.sigmoid(jnp.dot(g.astype(BF16), w_ref[...], preferred_element_type=F32) + b_ref[...])
    r = lax.rsqrt(jnp.mean(v * v, axis=-1, keepdims=True) + EPS)
    o_ref[...] = (v * r * gain_ref[...]).astype(o_ref.dtype)


def _glu(yx, yc, w, b, gain):
    ds = yx.shape[1]
    tm = 512
    nx_tiles, nc_tiles = yx.shape[0] // tm, yc.shape[0] // tm
    return pl.pallas_call(
        functools.partial(_glu_kernel, nx_tiles=nx_tiles),
        grid=(nx_tiles + nc_tiles,),
        in_specs=[pl.BlockSpec((tm, ds), lambda i: (jnp.minimum(i, nx_tiles - 1), 0)),
                  pl.BlockSpec((tm, ds), lambda i: (jnp.maximum(i - nx_tiles, 0), 0)),
                  pl.BlockSpec((ds, ds), lambda i: (0, 0)),
                  pl.BlockSpec((1, ds), lambda i: (0, 0)),
                  pl.BlockSpec((1, ds), lambda i: (0, 0))],
        out_specs=pl.BlockSpec((tm, ds), lambda i: (i, 0)),
        out_shape=jax.ShapeDtypeStruct((yx.shape[0] + yc.shape[0], ds), BF16),
        compiler_params=_cp(("parallel",)),
        name="glu",
    )(yx, yc, w, b, gain)


def _route(scores, sel, n_groups):
    epg = len(sel) // n_groups
    gscore = []
    for q in range(n_groups):
        v = sel[q * epg:(q + 1) * epg]
        best = None
        for a in range(epg):
            for b in range(a + 1, epg):
                s = v[a] + v[b]
                best = s if best is None else jnp.maximum(best, s)
        gscore.append(best)
    gbest = gscore[0]
    gidx = jnp.zeros(gbest.shape, jnp.int32)
    for q in range(1, n_groups):
        upd = gscore[q] > gbest
        gbest = jnp.where(upd, gscore[q], gbest)
        gidx = jnp.where(upd, q, gidx)
    vin = list(sel[:epg])
    sin = list(scores[:epg])
    for q in range(1, n_groups):
        pick = gidx == q
        for j in range(epg):
            vin[j] = jnp.where(pick, sel[q * epg + j], vin[j])
            sin[j] = jnp.where(pick, scores[q * epg + j], sin[j])
    b1 = vin[0]
    i1 = jnp.zeros(gbest.shape, jnp.int32)
    for j in range(1, epg):
        upd = vin[j] > b1
        b1 = jnp.where(upd, vin[j], b1)
        i1 = jnp.where(upd, j, i1)
    b2 = vin[0]
    i2 = jnp.zeros(gbest.shape, jnp.int32)
    have = jnp.zeros(gbest.shape, jnp.bool_)
    for j in range(epg):
        cand = i1 != j
        upd = cand & (jnp.logical_not(have) | (vin[j] > b2))
        b2 = jnp.where(upd, vin[j], b2)
        i2 = jnp.where(upd, j, i2)
        have = have | cand
    s1 = sin[0]
    s2 = sin[0]
    for j in range(1, epg):
        s1 = jnp.where(i1 == j, sin[j], s1)
        s2 = jnp.where(i2 == j, sin[j], s2)
    tot = s1 + s2
    return (gidx * epg + i1, gidx * epg + i2), (s1 / tot, s2 / tot)


def _merge_kernel(t_ref, yf_ref, ys_ref, wo_ref, g1_ref, n2_ref, sc2_ref, sh2_ref, rwh_ref, rwl_ref, rb_ref,
                  tn_ref, h3_ref, eidx_ref, ew_ref):
    df = yf_ref.shape[1]
    tm, d = t_ref.shape
    o = (jnp.dot(yf_ref[...], wo_ref[0:df, :], preferred_element_type=F32)
         + jnp.dot(ys_ref[...], wo_ref[df:, :], preferred_element_type=F32))
    tn = t_ref[...] + g1_ref[0] * o
    tn_ref[...] = tn
    h2 = tn * lax.rsqrt(jnp.mean(tn * tn, axis=-1, keepdims=True) + EPS) * n2_ref[...]
    h2 = h2 * (1.0 + sc2_ref[0]) + sh2_ref[0]
    nsub = d // LANES
    for c in range(nsub):
        h3_ref[pl.ds(c, tm, stride=nsub), :] = h2[:, c * LANES:(c + 1) * LANES]
    h_hi = h2.astype(BF16)
    h_lo = (h2 - h_hi.astype(F32)).astype(BF16)
    lg = (jnp.dot(h_hi, rwh_ref[...], preferred_element_type=F32)
          + jnp.dot(h_lo, rwh_ref[...], preferred_element_type=F32)
          + jnp.dot(h_hi, rwl_ref[...], preferred_element_type=F32))
    ne = rb_ref.shape[0]
    logits = lg.T[0:ne, :]
    scores = jax.nn.sigmoid(logits)
    sel = scores + rb_ref[...]
    srows = [scores[e:e + 1, :] for e in range(ne)]
    vrows = [sel[e:e + 1, :] for e in range(ne)]
    (e1, e2), (w1, w2) = _route(srows, vrows, N_EXPERT_GROUPS)
    eidx_ref[...] = jnp.zeros_like(eidx_ref)
    ew_ref[...] = jnp.zeros_like(ew_ref)
    eidx_ref[0:1, :] = e1
    eidx_ref[1:2, :] = e2
    ew_ref[0:1, :] = w1
    ew_ref[1:2, :] = w2


def _merge(t, yf, ys, wo, g1, n2, sc2, sh2, rwh, rwl, rb, n_x_rows, seq):
    nt, d = t.shape
    df = yf.shape[1]
    ne = rb.shape[0]
    nsub = d // LANES
    tm = ROW_TILE
    mrow = functools.partial(_mod_row, tm=tm, n_x_rows=n_x_rows, seq=seq)
    mspec = pl.BlockSpec((1, 1, d), lambda i: (mrow(i), 0, 0))
    return pl.pallas_call(
        _merge_kernel,
        grid=(nt // tm,),
        in_specs=[pl.BlockSpec((tm, d), lambda i: (i, 0)),
                  pl.BlockSpec((tm, df), lambda i: (i, 0)),
                  pl.BlockSpec((tm, d - df), lambda i: (i, 0)),
                  pl.BlockSpec((d, d), lambda i: (0, 0)),
                  mspec,
                  pl.BlockSpec((1, d), lambda i: (0, 0)),
                  mspec, mspec,
                  pl.BlockSpec((d, LANES), lambda i: (0, 0)),
                  pl.BlockSpec((d, LANES), lambda i: (0, 0)),
                  pl.BlockSpec((ne, 1), lambda i: (0, 0))],
        out_specs=[pl.BlockSpec((tm, d), lambda i: (i, 0)),
                   pl.BlockSpec((tm * nsub, LANES), lambda i: (i, 0)),
                   pl.BlockSpec((8, tm), lambda i: (0, i)),
                   pl.BlockSpec((8, tm), lambda i: (0, i))],
        out_shape=[jax.ShapeDtypeStruct((nt, d), F32),
                   jax.ShapeDtypeStruct((nt * nsub, LANES), F32),
                   jax.ShapeDtypeStruct((8, nt), jnp.int32),
                   jax.ShapeDtypeStruct((8, nt), F32)],
        compiler_params=_cp(("parallel",)),
        name="merge_route",
    )(t, yf, ys, wo, g1, n2, sc2, sh2, rwh, rwl, rb)


def _dispatch(eidx, ew, n_experts, rows, n_blocks):
    nt = eidx.shape[1]
    a = nt * TOP_K
    flat_e = eidx[:TOP_K].T.reshape(a)
    onehot = (flat_e[:, None] == jnp.arange(n_experts, dtype=jnp.int32)[None, :]).astype(jnp.int32)
    csum = jnp.cumsum(onehot, axis=0)
    counts = csum[-1]
    padded = (counts + rows - 1) // rows * rows
    pad_end = jnp.cumsum(padded)
    pad_start = pad_end - padded
    dest = jnp.sum(onehot * (csum - 1 + pad_start[None, :]), axis=1)
    tok = jnp.arange(a, dtype=jnp.int32) // TOP_K
    buf_tok = jnp.zeros((n_blocks * rows,), jnp.int32).at[dest].set(tok)
    n_valid = (pad_end[-1] // rows).astype(jnp.int32)
    blk_start = jnp.arange(n_blocks, dtype=jnp.int32) * rows
    blk_exp = jnp.sum((pad_end[None, :] <= blk_start[:, None]).astype(jnp.int32), axis=1)
    blk_exp = jnp.minimum(blk_exp, n_experts - 1)
    last_exp = jnp.sum(jnp.where(jnp.arange(n_blocks) == n_valid - 1, blk_exp, 0))
    blk_exp = jnp.where(jnp.arange(n_blocks) < n_valid, blk_exp, last_exp).astype(jnp.int32)
    return buf_tok, blk_exp, n_valid.reshape(1), dest.reshape(nt, TOP_K), ew[:TOP_K].T


def _start_row_copies(n_rows, src_hbm, src_row, dst, sem, nsub):
    def body(q, c):
        for k in range(ROW_COPY_UNROLL):
            r = q * ROW_COPY_UNROLL + k
            src = src_hbm.at[pl.ds(pl.multiple_of(src_row(r) * nsub, nsub), nsub)]
            pltpu.make_async_copy(src, dst.at[pl.ds(pl.multiple_of(r * GATHER_PITCH, 8), nsub)], sem).start()
        return c

    lax.fori_loop(0, n_rows // ROW_COPY_UNROLL, body, 0)


def _wait_row_copies(n_rows, src_hbm, dst, sem, nsub):
    pltpu.make_async_copy(src_hbm.at[pl.ds(0, n_rows * nsub)], dst.at[pl.ds(0, n_rows * nsub)], sem).wait()


def _expert_kernel(be_ref, nv_ref, tok_ref, tokn_ref, h_hbm, wg_hbm, wu_hbm, wd_hbm, o_ref,
                   wgb_ref, wub_ref, wdb_ref, stg_ref, std_ref, xg_ref, xb_ref, sem, wsem, *, nsub, layer):
    i = pl.program_id(0)
    rows = xb_ref.shape[0]
    valid = i < nv_ref[0]
    slot = i % 2
    e = be_ref[i]
    first = (i == 0) | (be_ref[jnp.maximum(i - 1, 0)] != e)

    def gather(toks, s):
        _start_row_copies(rows, h_hbm, lambda r: toks[0, r], xg_ref.at[s], sem.at[s], nsub)

    @pl.when(i == 0)
    def _():
        gather(tok_ref, 0)

    next_valid = (i + 1 < nv_ref[0]) & (i + 1 < pl.num_programs(0))
    for s in range(2):
        @pl.when(next_valid & (slot == 1 - s))
        def _():
            gather(tokn_ref, s)

    @pl.when(valid & first)
    def _():
        chunks = []
        for src, dst, stg in ((wg_hbm, wgb_ref, stg_ref), (wu_hbm, wub_ref, stg_ref), (wd_hbm, wdb_ref, std_ref)):
            nr = stg.shape[1]
            chunks += [(src, dst, stg, k * nr, nr) for k in range(dst.shape[0] // nr)]

        def copy(k):
            src, _, stg, r0, nr = chunks[k]
            return pltpu.make_async_copy(src.at[layer, e, pl.ds(r0, nr), :], stg.at[k % 2], wsem.at[k % 2])

        copy(0).start()
        for k, (_, dst, stg, r0, nr) in enumerate(chunks):
            if k + 1 < len(chunks):
                copy(k + 1).start()
            copy(k).wait()
            dst[r0:r0 + nr, :] = stg[k % 2].astype(BF16)

    @pl.when(valid)
    def _():
        for s in range(2):
            @pl.when(slot == s)
            def _():
                _wait_row_copies(rows, h_hbm, xg_ref.at[s], sem.at[s], nsub)
                for c in range(nsub):
                    xb_ref[:, c * LANES:(c + 1) * LANES] = (
                        xg_ref[s, pl.ds(c, rows, stride=GATHER_PITCH), :].astype(BF16))
        x = xb_ref[...]
        g = jnp.dot(x, wgb_ref[...], preferred_element_type=F32)
        u = jnp.dot(x, wub_ref[...], preferred_element_type=F32)
        hmid = (g * jax.nn.sigmoid(g)) * u
        y = jnp.dot(hmid.astype(BF16), wdb_ref[...], preferred_element_type=F32)
        for c in range(nsub):
            o_ref[pl.ds(c, rows, stride=nsub), :] = y[:, c * LANES:(c + 1) * LANES]

    @pl.when(jnp.logical_not(valid))
    def _():
        o_ref[...] = jnp.zeros_like(o_ref)


def _experts(h3, buf_tok, blk_exp, n_valid, w_gate, w_up, w_down, layer):
    _, ne, d, de = w_gate.shape
    nsub = d // LANES
    rows = MOE_ROWS
    n_blocks = buf_tok.shape[0] // rows
    any_spec = pl.BlockSpec(memory_space=pl.ANY)
    grid_spec = pltpu.PrefetchScalarGridSpec(
        num_scalar_prefetch=2,
        grid=(n_blocks,),
        in_specs=[pl.BlockSpec((None, 1, rows), lambda i, be, nv: (i, 0, 0), memory_space=pltpu.SMEM),
                  pl.BlockSpec((None, 1, rows), lambda i, be, nv: (jnp.minimum(i + 1, n_blocks - 1), 0, 0),
                               memory_space=pltpu.SMEM),
                  any_spec, any_spec, any_spec, any_spec],
        out_specs=pl.BlockSpec((rows * nsub, LANES), lambda i, be, nv: (i, 0)),
        scratch_shapes=[pltpu.VMEM((d, de), BF16),
                        pltpu.VMEM((d, de), BF16),
                        pltpu.VMEM((de, d), BF16),
                        pltpu.VMEM((2, d // MOE_WCHUNKS, de), F32),
                        pltpu.VMEM((2, de // MOE_WCHUNKS, d), F32),
                        pltpu.VMEM((2, rows * GATHER_PITCH, LANES), F32),
                        pltpu.VMEM((rows, d), BF16),
                        pltpu.SemaphoreType.DMA((2,)),
                        pltpu.SemaphoreType.DMA((2,))],
    )
    toks = buf_tok.reshape(n_blocks, 1, rows)
    return pl.pallas_call(
        functools.partial(_expert_kernel, nsub=nsub, layer=layer),
        grid_spec=grid_spec,
        out_shape=jax.ShapeDtypeStruct((n_blocks * rows * nsub, LANES), F32),
        compiler_params=_cp(("arbitrary",)),
        name="experts",
    )(blk_exp, n_valid, toks, toks, h3, w_gate, w_up, w_down)


def _combine_kernel(pos_ref, posn_ref, t_ref, w_ref, yb_hbm, g2_ref, o_ref, gk_ref, sem, *, nsub):
    i = pl.program_id(0)
    tm = t_ref.shape[0]
    slot = i % 2

    def gather(pos, s):
        for k in range(TOP_K):
            _start_row_copies(tm, yb_hbm, lambda r, k=k: pos[0, TOP_K * r + k], gk_ref.at[s, k], sem.at[s], nsub)

    @pl.when(i == 0)
    def _():
        gather(pos_ref, 0)

    for s in range(2):
        @pl.when((i + 1 < pl.num_programs(0)) & (slot == 1 - s))
        def _():
            gather(posn_ref, s)

    w0 = w_ref[:, 0:1]
    w1 = w_ref[:, 1:2]
    for s in range(2):
        @pl.when(slot == s)
        def _():
            for k in range(TOP_K):
                _wait_row_copies(tm, yb_hbm, gk_ref.at[s, k], sem.at[s], nsub)
            for c in range(nsub):
                sl = slice(c * LANES, (c + 1) * LANES)
                y = (gk_ref[s, 0, pl.ds(c, tm, stride=GATHER_PITCH), :] * w0
                     + gk_ref[s, 1, pl.ds(c, tm, stride=GATHER_PITCH), :] * w1)
                o_ref[:, sl] = t_ref[:, sl] + g2_ref[0][:, sl] * y


def _combine(t, yb3, pos, w, g2, n_x_rows, seq):
    nt, d = t.shape
    nsub = d // LANES
    tm = ROW_TILE
    mrow = functools.partial(_mod_row, tm=tm, n_x_rows=n_x_rows, seq=seq)
    nb = nt // tm
    posb = pos.reshape(nb, 1, TOP_K * tm)
    return pl.pallas_call(
        functools.partial(_combine_kernel, nsub=nsub),
        grid=(nb,),
        in_specs=[pl.BlockSpec((None, 1, TOP_K * tm), lambda i: (i, 0, 0), memory_space=pltpu.SMEM),
                  pl.BlockSpec((None, 1, TOP_K * tm), lambda i: (jnp.minimum(i + 1, nb - 1), 0, 0),
                               memory_space=pltpu.SMEM),
                  pl.BlockSpec((tm, d), lambda i: (i, 0)),
                  pl.BlockSpec((tm, TOP_K), lambda i: (i, 0)),
                  pl.BlockSpec(memory_space=pl.ANY),
                  pl.BlockSpec((1, 1, d), lambda i: (mrow(i), 0, 0))],
        out_specs=pl.BlockSpec((tm, d), lambda i: (i, 0)),
        out_shape=jax.ShapeDtypeStruct((nt, d), F32),
        scratch_shapes=[pltpu.VMEM((2, TOP_K, tm * GATHER_PITCH, LANES), F32),
                        pltpu.SemaphoreType.DMA((2,))],
        compiler_params=_cp(("arbitrary",)),
        name="combine",
    )(posb, posb, t, w, yb3, g2)


def _final_kernel(x_ref, g_ref, o_ref):
    x = x_ref[...]
    o_ref[...] = x * lax.rsqrt(jnp.mean(x * x, axis=-1, keepdims=True) + EPS) * g_ref[...]


def _final_norm(t, g, n_rows):
    d = t.shape[1]
    tm = ROW_TILE
    return pl.pallas_call(
        _final_kernel,
        grid=(n_rows // tm,),
        in_specs=[pl.BlockSpec((tm, d), lambda i: (i, 0)),
                  pl.BlockSpec((1, d), lambda i: (0, 0))],
        out_specs=pl.BlockSpec((tm, d), lambda i: (i, 0)),
        out_shape=jax.ShapeDtypeStruct((n_rows, d), F32),
        compiler_params=_cp(("parallel",)),
        name="final_norm",
    )(t, g)


def kernel(x, c, ctx, c_ctx, w_mod, b_mod, norm1_g, norm2_g, w_in, w_out, fourier_w, mix_norm_g,
           lam_re, lam_im, log_dt, b_re, b_im, c_re, c_im, d_skip, glu_w, glu_b,
           router_w, router_b, w_gate, w_up, w_down, final_g):
    bsz, seq, d = x.shape
    n_ctx = ctx.shape[1]
    depth = w_mod.shape[0]
    df = fourier_w.shape[1] * fourier_w.shape[2]
    ds = d_skip.shape[1]
    ne = router_w.shape[1]
    nx_rows = bsz * seq
    nt = nx_rows + bsz * n_ctx

    t = jnp.concatenate([x.reshape(nx_rows, d), ctx.reshape(bsz * n_ctx, d)], axis=0).astype(F32)

    cvec = jnp.concatenate([c_ctx[None, :], c, jnp.zeros((8 - 1 - bsz, d), c.dtype)], axis=0).astype(F32)
    mod = _adaln(cvec, w_mod, b_mod).reshape(depth, 8, N_MOD, 1, d)

    cc, sc_ = _dft_tables(df)
    w_chan = jnp.concatenate([cc, sc_], axis=1).astype(BF16)
    tabs_x = _half_tables(seq)
    tabs_c = _half_tables(n_ctx)

    rw = jnp.pad(router_w.astype(F32), ((0, 0), (0, LANES - ne)))
    rwh = rw.astype(BF16)
    rwl = (rw - rwh.astype(F32)).astype(BF16)
    rb = router_b.astype(F32).reshape(ne, 1)
    n_blocks = -(-(nt * TOP_K) // MOE_ROWS) + ne

    for l in range(depth):
        sh1, sc1, g1, sh2, sc2, g2 = [mod[l, :, k] for k in range(N_MOD)]
        zf, zs = _inproj(t, norm1_g[l].reshape(1, d).astype(F32), sc1, sh1, w_in[l].astype(BF16),
                         df, nx_rows, seq)

        hw = fourier_w[l].astype(BF16)
        gain = mix_norm_g[l].astype(F32).reshape(1, -1)
        yf_x = _fourier_mix(zf, w_chan, tabs_x, hw, gain[:, :df], 0, bsz, seq)
        yf_c = _fourier_mix(zf, w_chan, tabs_c, hw, gain[:, :df], nx_rows, bsz, n_ctx)
        yf = jnp.concatenate([yf_x, yf_c], axis=0)

        tables = _s5_tables(lam_re[l], lam_im[l], log_dt[l], b_re[l], b_im[l], c_re[l], c_im[l], d_skip[l])
        ysx, ysc = _s5_scan(zs, tables, bsz, seq, n_ctx)
        ys = _glu(ysx, ysc, glu_w[l].astype(BF16), glu_b[l].astype(F32).reshape(1, ds), gain[:, df:])

        t, h3, eidx, ew = _merge(t, yf, ys, w_out[l].astype(BF16), g1,
                                 norm2_g[l].reshape(1, d).astype(F32), sc2, sh2, rwh, rwl, rb, nx_rows, seq)

        buf_tok, blk_exp, n_valid, pos, wtok = _dispatch(eidx, ew, ne, MOE_ROWS, n_blocks)
        yb = _experts(h3, buf_tok, blk_exp, n_valid, w_gate, w_up, w_down, l)
        t = _combine(t, yb, pos, wtok, g2, nx_rows, seq)

    out = _final_norm(t, final_g.reshape(1, d).astype(F32), nx_rows)
    return out.reshape(bsz, seq, d).astype(x.dtype)
```

```python
import functools
import math

import jax
import jax.numpy as jnp
from jax import lax
from jax.experimental import pallas as pl
from jax.experimental.pallas import tpu as pltpu

F32 = jnp.float32
BF16 = jnp.bfloat16
EPS = 1e-6

FOURIER_HEADS = 4
SSM_GROUP = 16
N_EXPERT_GROUPS = 4
TOP_K = 2
N_MOD = 6

SSM_CHUNK = 16
ROW_TILE = 256
MOE_ROWS = 256
MOE_WCHUNKS = 8
ROW_COPY_UNROLL = 8
LANES = 128
GATHER_PITCH = 24
S5_SET = LANES
VMEM_LIMIT_BYTES = 56 * 1024 * 1024


def _cp(sems):
    return pltpu.CompilerParams(dimension_semantics=sems, vmem_limit_bytes=VMEM_LIMIT_BYTES)


def _mod_row(i, tm, n_x_rows, seq):
    r = i * tm
    return jnp.where(r < n_x_rows, 1 + r // seq, 0)


def _mod_kernel(c_ref, w_ref, b_ref, o_ref):
    c = c_ref[...]
    s = c * jax.nn.sigmoid(c)
    o_ref[...] = jnp.dot(s.astype(BF16), w_ref[...].astype(BF16),
                         preferred_element_type=F32) + b_ref[...]


def _adaln(cvec, w_mod, b_mod):
    depth, d, n = w_mod.shape
    tn = 1024
    return pl.pallas_call(
        _mod_kernel,
        grid=(depth, n // tn),
        in_specs=[pl.BlockSpec((8, d), lambda l, j: (0, 0)),
                  pl.BlockSpec((None, d, tn), lambda l, j: (l, 0, j)),
                  pl.BlockSpec((None, 1, tn), lambda l, j: (l, 0, j))],
        out_specs=pl.BlockSpec((None, 8, tn), lambda l, j: (l, 0, j)),
        out_shape=jax.ShapeDtypeStruct((depth, 8, n), F32),
        compiler_params=_cp(("parallel", "parallel")),
        name="adaln",
    )(cvec, w_mod, b_mod.reshape(depth, 1, n))


def _inproj_kernel(t_ref, g_ref, sc_ref, sh_ref, w_ref, zf_ref, zs_ref):
    x = t_ref[...]
    h = x * lax.rsqrt(jnp.mean(x * x, axis=-1, keepdims=True) + EPS) * g_ref[...]
    h = h * (1.0 + sc_ref[0]) + sh_ref[0]
    z = jnp.dot(h.astype(BF16), w_ref[...], preferred_element_type=F32)
    df = zf_ref.shape[1]
    zf_ref[...] = z[:, :df].astype(zf_ref.dtype)
    zs_ref[...] = z[:, df:]


def _inproj(t, g, sc, sh, w, df, n_x_rows, seq):
    nt, d = t.shape
    n = w.shape[1]
    tm = ROW_TILE
    mrow = functools.partial(_mod_row, tm=tm, n_x_rows=n_x_rows, seq=seq)
    return pl.pallas_call(
        _inproj_kernel,
        grid=(nt // tm,),
        in_specs=[pl.BlockSpec((tm, d), lambda i: (i, 0)),
                  pl.BlockSpec((1, d), lambda i: (0, 0)),
                  pl.BlockSpec((1, 1, d), lambda i: (mrow(i), 0, 0)),
                  pl.BlockSpec((1, 1, d), lambda i: (mrow(i), 0, 0)),
                  pl.BlockSpec((d, n), lambda i: (0, 0))],
        out_specs=[pl.BlockSpec((tm, df), lambda i: (i, 0)),
                   pl.BlockSpec((tm, n - df), lambda i: (i, 0))],
        out_shape=[jax.ShapeDtypeStruct((nt, df), BF16),
                   jax.ShapeDtypeStruct((nt, n - df), F32)],
        compiler_params=_cp(("parallel",)),
        name="inproj",
    )(t, g, sc, sh, w)


def _dft_tables(n):
    r = 1
    while r * r < n:
        r *= 2
    q = n // r
    k = jnp.arange(n, dtype=jnp.int32)[:, None]
    step = 2.0 * math.pi / n
    pa = ((k * (jnp.arange(q, dtype=jnp.int32)[None, :] * r)) % n).astype(F32) * step
    pb = ((k * jnp.arange(r, dtype=jnp.int32)[None, :]) % n).astype(F32) * step
    ca, sa = jnp.cos(pa)[:, :, None], jnp.sin(pa)[:, :, None]
    cb, sb = jnp.cos(pb)[:, None, :], jnp.sin(pb)[:, None, :]
    c = (ca * cb - sa * sb).reshape(n, n)
    s = (sa * cb + ca * sb).reshape(n, n)
    return c, s


def _chan_dft_kernel(z_ref, w_ref, o_ref):
    o_ref[...] = jnp.dot(z_ref[...], w_ref[...], preferred_element_type=F32).astype(o_ref.dtype)


def _chan_dft(z, w):
    nt = z.shape[0]
    df, n = w.shape
    tm = min(512, nt)
    return pl.pallas_call(
        _chan_dft_kernel,
        grid=(nt // tm,),
        in_specs=[pl.BlockSpec((tm, df), lambda i: (i, 0)),
                  pl.BlockSpec((df, n), lambda i: (0, 0))],
        out_specs=pl.BlockSpec((tm, n), lambda i: (i, 0)),
        out_shape=jax.ShapeDtypeStruct((nt, n), BF16),
        compiler_params=_cp(("parallel",)),
        name="chan_dft",
    )(z, w)


def _chan_sym_kernel(z_ref, zr_ref, w_ref, o_ref):
    z = z_ref[...].astype(F32)
    zr = zr_ref[...].astype(F32)
    c = w_ref.shape[0]
    o_ref[:, :c] = jnp.dot((z + zr).astype(BF16), w_ref[:, :c], preferred_element_type=F32).astype(o_ref.dtype)
    o_ref[:, c:] = jnp.dot((z - zr).astype(BF16), w_ref[:, c:], preferred_element_type=F32).astype(o_ref.dtype)


def _chan_sym(z, zr, w, row0, bsz, length):
    df, n = w.shape
    half = length // 2
    tm = min(512, half)
    hb = half // tm
    off = row0 // tm
    return pl.pallas_call(
        _chan_sym_kernel,
        grid=(bsz, hb),
        in_specs=[pl.BlockSpec((tm, df), lambda b, i: (off + b * 2 * hb + i, 0)),
                  pl.BlockSpec((tm, df), lambda b, i: (b * hb + i, 0)),
                  pl.BlockSpec((df, n), lambda b, i: (0, 0))],
        out_specs=pl.BlockSpec((tm, n), lambda b, i: (b * hb + i, 0)),
        out_shape=jax.ShapeDtypeStruct((bsz * half, n), BF16),
        compiler_params=_cp(("parallel", "parallel")),
        name="chan_sym",
    )(z, zr, w)


def _fourier_epilogue(f, hw_ref, gain_ref, o_ref):
    nh, hd, _ = hw_ref.shape
    ys = [jnp.dot(f[:, h * hd:(h + 1) * hd].astype(BF16), hw_ref[h],
                  preferred_element_type=F32) for h in range(nh)]
    ssum = jnp.sum(ys[0] * ys[0], axis=-1, keepdims=True)
    for h in range(1, nh):
        ssum = ssum + jnp.sum(ys[h] * ys[h], axis=-1, keepdims=True)
    r = lax.rsqrt(ssum / (nh * hd) + EPS)
    for h in range(nh):
        o_ref[:, h * hd:(h + 1) * hd] = (ys[h] * r * gain_ref[:, h * hd:(h + 1) * hd]).astype(o_ref.dtype)


def _row_sym_kernel(ch_ref, sh_ref, alt_ref, ae_ref, bo_ref, amid_ref, hw_ref, gain_ref, lo_ref, hi_ref,
                    p_ref, q_ref, r_ref, *, scale, mid_sign):
    m = pl.program_id(1)
    k = pl.program_id(2)

    @pl.when(k == 0)
    def _():
        p_ref[...] = jnp.zeros_like(p_ref)
        q_ref[...] = jnp.zeros_like(q_ref)
        r_ref[...] = jnp.zeros_like(r_ref)

    ae = ae_ref[...]
    p_ref[...] += jnp.dot(ch_ref[...], ae, preferred_element_type=F32)
    q_ref[...] += jnp.dot(sh_ref[...], bo_ref[...], preferred_element_type=F32)

    @pl.when(m == 0)
    def _():
        r_ref[...] += jnp.dot(alt_ref[...], ae, preferred_element_type=F32)

    @pl.when(k == pl.num_programs(2) - 1)
    def _():
        tm = p_ref.shape[0]
        row = lax.broadcasted_iota(jnp.int32, (tm, 1), 0) + m * tm
        amid = amid_ref[0:1, :]
        pf = p_ref[...] + jnp.where((row & 1) == 0, 1.0, -1.0) * amid
        q = q_ref[...]
        mid = r_ref[0:1, :] + mid_sign * amid
        hi = jnp.where(row == 0, mid, pf + q)
        _fourier_epilogue((pf - q) * scale, hw_ref, gain_ref, lo_ref)
        _fourier_epilogue(hi * scale, hw_ref, gain_ref, hi_ref)


def _without_alias_ref(kernel, pos):
    def wrapped(*refs):
        return kernel(*refs[:pos], *refs[pos + 1:])
    return wrapped


def _row_sym(aebo, amid, ch, sh, alt, hw, gain, bsz, length, row0, ybuf, n_rows):
    df = aebo.shape[1] // 2
    half = length // 2
    tm = tk = min(512, half)
    mb, kb = half // tm, half // tk
    scale = 1.0 / math.sqrt(length * df)
    mid_sign = 1.0 if half % 2 == 0 else -1.0
    off = row0 // tm
    kern = functools.partial(_row_sym_kernel, scale=scale, mid_sign=mid_sign)
    in_specs = [pl.BlockSpec((tm, tk), lambda b, m, k: (m, k)),
                pl.BlockSpec((tm, tk), lambda b, m, k: (m, k)),
                pl.BlockSpec((8, tk), lambda b, m, k: (0, k)),
                pl.BlockSpec((tk, df), lambda b, m, k: (b * kb + k, 0)),
                pl.BlockSpec((tk, df), lambda b, m, k: (b * kb + k, 1)),
                pl.BlockSpec((8, df), lambda b, m, k: (b, 0)),
                pl.BlockSpec(hw.shape, lambda b, m, k: (0, 0, 0)),
                pl.BlockSpec((1, df), lambda b, m, k: (0, 0))]
    args = [ch, sh, alt, aebo, aebo, amid, hw, gain]
    aliases = {}
    if ybuf is not None:
        kern = _without_alias_ref(kern, len(args))
        in_specs.append(pl.BlockSpec(memory_space=pl.ANY))
        aliases = {len(args): 0}
        args.append(ybuf)
    return pl.pallas_call(
        kern,
        grid=(bsz, mb, kb),
        in_specs=in_specs,
        out_specs=[pl.BlockSpec((tm, df), lambda b, m, k: (off + b * 2 * mb + m, 0)),
                   pl.BlockSpec((tm, df), lambda b, m, k: (b * mb + m, 0))],
        out_shape=[jax.ShapeDtypeStruct((n_rows, df), BF16),
                   jax.ShapeDtypeStruct((bsz * half, df), BF16)],
        scratch_shapes=[pltpu.VMEM((tm, df), F32), pltpu.VMEM((tm, df), F32), pltpu.VMEM((8, df), F32)],
        input_output_aliases=aliases,
        compiler_params=_cp(("parallel", "parallel", "arbitrary")),
        name="row_sym",
    )(*args)


def _mirror_kernel(j_ref, a_ref, b_ref, o_ref, *, keep_first):
    rev = jnp.dot(j_ref[...], a_ref[...], preferred_element_type=F32)
    first = b_ref[0:1, :].astype(F32)
    if not keep_first:
        first = jnp.where(pl.program_id(1) == 0, 0.0, first)
    row = lax.broadcasted_iota(jnp.int32, (o_ref.shape[0], 1), 0)
    o_ref[...] = jnp.where(row == 0, first, rev).astype(o_ref.dtype)


def _mirror(src, src_row0, src_stride, half, bsz, keep_first, dst_row0=0, dst_stride=None, ybuf=None):
    df = src.shape[1]
    tm = min(512, half)
    nb = half // tm
    dst_stride = half if dst_stride is None else dst_stride
    soff, sstr, doff, dstr = src_row0 // tm, src_stride // tm, dst_row0 // tm, dst_stride // tm
    r = jnp.arange(tm, dtype=jnp.int32)
    jmat = ((r[:, None] + r[None, :] == tm) & (r[:, None] > 0)).astype(BF16)
    kern = functools.partial(_mirror_kernel, keep_first=keep_first)
    in_specs = [pl.BlockSpec((tm, tm), lambda b, i: (0, 0)),
                pl.BlockSpec((tm, df), lambda b, i: (soff + b * sstr + nb - 1 - i, 0)),
                pl.BlockSpec((tm, df), lambda b, i: (soff + b * sstr + (nb - i) % nb, 0))]
    args = [jmat, src, src]
    aliases = {}
    n_rows = bsz * half
    if ybuf is not None:
        kern = _without_alias_ref(kern, len(args))
        in_specs.append(pl.BlockSpec(memory_space=pl.ANY))
        aliases = {len(args): 0}
        args.append(ybuf)
        n_rows = ybuf.shape[0]
    return pl.pallas_call(
        kern,
        grid=(bsz, nb),
        in_specs=in_specs,
        out_specs=pl.BlockSpec((tm, df), lambda b, i: (doff + b * dstr + i, 0)),
        out_shape=jax.ShapeDtypeStruct((n_rows, df), BF16),
        input_output_aliases=aliases,
        compiler_params=_cp(("parallel", "parallel")),
        name="mirror",
    )(*args)


def _fourier_mix(zf, w_chan, tabs, hw, gain, row0, bsz, length, ybuf):
    ch, sh, alt = tabs
    df = zf.shape[1]
    half = length // 2
    zr = _mirror(zf, row0 + half, length, half, bsz, keep_first=False)
    aebo = _chan_sym(zf, zr, w_chan, row0, bsz, length)
    zmid = zf[row0:row0 + bsz * length].reshape(bsz, length, df)[:, half]
    zmid = jnp.pad(zmid[:, None, :], ((0, 0), (0, 7), (0, 0)))
    amid = _chan_dft(zmid.reshape(bsz * 8, df), w_chan)[:, :df].astype(F32)
    y, hi = _row_sym(aebo, amid, ch, sh, alt, hw, gain, bsz, length, row0, ybuf, zf.shape[0])
    return _mirror(hi, 0, half, half, bsz, keep_first=True, dst_row0=row0 + half, dst_stride=length, ybuf=y)


def _half_tables(length):
    half = length // 2
    c, s = _dft_tables(length)
    n = jnp.arange(half, dtype=jnp.int32)
    alt = jnp.where((lax.broadcasted_iota(jnp.int32, (8, half), 0) == 0),
                    jnp.where(n % 2 == 0, 1.0, -1.0)[None, :], 0.0)
    return c[:half, :half].astype(BF16), s[:half, :half].astype(BF16), alt.astype(BF16)


def _s5_tables(lam_re, lam_im, log_dt, b_re, b_im, c_re, c_im, d_skip):
    tc = SSM_CHUNK
    g, p = lam_re.shape[1:]
    hch = b_re.shape[-1]
    lr, li = lam_re.astype(F32), lam_im.astype(F32)
    dt = jnp.exp(log_dt.astype(F32))[..., None]
    er, ei = lr * dt, li * dt

    def lpow(k):
        m = jnp.exp(er * k)
        return m * jnp.cos(ei * k), m * jnp.sin(ei * k)

    l1r, l1i = lpow(1.0)
    den = lr * lr + li * li
    qr = ((l1r - 1.0) * lr + l1i * li) / den
    qi = (l1i * lr - (l1r - 1.0) * li) / den
    br, bi = b_re.astype(F32), b_im.astype(F32)
    bbr = qr[..., None] * br - qi[..., None] * bi
    bbi = qr[..., None] * bi + qi[..., None] * br
    cr, ci = c_re.astype(F32), c_im.astype(F32)

    ks = jnp.arange(tc + 1, dtype=F32)
    pwr = jnp.stack([lpow(k)[0] for k in range(tc + 1)], axis=-1)
    pwi = jnp.stack([lpow(k)[1] for k in range(tc + 1)], axis=-1)
    del ks

    cbr = jnp.einsum("dgip,dgpj->dgpij", cr, bbr) - jnp.einsum("dgip,dgpj->dgpij", ci, bbi)
    cbi = jnp.einsum("dgip,dgpj->dgpij", cr, bbi) + jnp.einsum("dgip,dgpj->dgpij", ci, bbr)
    klag = (jnp.einsum("dgpij,dgpl->dglij", cbr, pwr[..., :tc])
            - jnp.einsum("dgpij,dgpl->dglij", cbi, pwi[..., :tc]))

    t_in = jnp.arange(tc)[:, None]
    t_out = jnp.arange(tc)[None, :]
    lag_f = t_out - t_in
    lag_b = t_in - t_out
    kf = klag[0][:, jnp.clip(lag_f, 0, tc - 1)] * (lag_f >= 0)[None, :, :, None, None]
    kb = klag[1][:, jnp.clip(lag_b, 0, tc - 1)] * (lag_b >= 0)[None, :, :, None, None]
    eye_t = (t_in == t_out).astype(F32)[None, :, :, None, None]
    dg = d_skip.astype(F32).reshape(g, hch)
    dmat = eye_t * (jnp.eye(hch, dtype=F32) * dg[:, None, :])[:, None, None, :, :]
    m = (kf + kb + dmat).transpose(0, 1, 4, 2, 3).reshape(g, tc * hch, tc * hch)

    idx_f = jnp.arange(tc - 1, -1, -1)
    idx_b = jnp.arange(tc)

    def st(d, idx):
        wr = pwr[d][..., idx][:, :, :, None] * bbr[d][:, :, None, :] - pwi[d][..., idx][:, :, :, None] * bbi[d][:, :, None, :]
        wi = pwr[d][..., idx][:, :, :, None] * bbi[d][:, :, None, :] + pwi[d][..., idx][:, :, :, None] * bbr[d][:, :, None, :]
        return (wr.transpose(0, 2, 3, 1).reshape(g, tc * hch, p),
                wi.transpose(0, 2, 3, 1).reshape(g, tc * hch, p))

    sfr, sfi = st(0, idx_f)
    sbr, sbi = st(1, idx_b)
    w_st = jnp.concatenate([sfr, sbr, sfi, sbi], axis=-1)

    def so(d, idx):
        wr = cr[d][:, None, :, :] * pwr[d][..., idx].transpose(0, 2, 1)[:, :, None, :] \
            - ci[d][:, None, :, :] * pwi[d][..., idx].transpose(0, 2, 1)[:, :, None, :]
        wi = cr[d][:, None, :, :] * pwi[d][..., idx].transpose(0, 2, 1)[:, :, None, :] \
            + ci[d][:, None, :, :] * pwr[d][..., idx].transpose(0, 2, 1)[:, :, None, :]
        return (wr.reshape(g, tc * hch, p).transpose(0, 2, 1),
                wi.reshape(g, tc * hch, p).transpose(0, 2, 1))

    ofr, ofi = so(0, jnp.arange(1, tc + 1))
    obr, obi = so(1, jnp.arange(tc, 0, -1))
    zero = jnp.zeros_like(ofr)
    w_of = jnp.concatenate([ofr, zero, -ofi, zero], axis=1)
    w_ob = jnp.concatenate([zero, obr, zero, -obi], axis=1)

    a_re = jnp.concatenate([pwr[0][..., tc], pwr[1][..., tc]], axis=-1)
    a_im = jnp.concatenate([pwi[0][..., tc], pwi[1][..., tc]], axis=-1)

    gs = S5_SET // hch
    ns = g // gs
    kd = tc * hch
    w_o = w_of + w_ob
    return (m.astype(BF16).reshape(ns, gs, kd, kd),
            w_st.astype(BF16).reshape(ns, gs, kd, 4 * p),
            w_o.astype(BF16).reshape(ns, gs, 4 * p, kd),
            a_re.reshape(ns, gs, 2 * p), a_im.reshape(ns, gs, 2 * p))


def _chunk_perm(tc, gs, hch):
    n = tc * gs * hch
    r = jnp.arange(n, dtype=jnp.int32)
    t, g, j = r // (gs * hch), (r // hch) % gs, r % hch
    dst = g * (tc * hch) + t * hch + j
    fwd = (dst[:, None] == r[None, :]).astype(BF16)
    return fwd, fwd.T


def _s5_kernel(zx_ref, zc_ref, pf_ref, pb_ref, m_ref, wst_ref, wo_ref, are_ref, aim_ref, ox_ref, oc_ref,
               sh_ref, xb_ref, yb_ref, *, ctx_chunks, x_chunks, pitch):
    tc = SSM_CHUNK
    nch = ctx_chunks + x_chunks
    lanes = zx_ref.shape[1]
    gs, kd, nst = wst_ref.shape
    for t in range(tc):
        xb_ref[0:ctx_chunks, t * lanes:(t + 1) * lanes] = zc_ref[pl.ds(t, ctx_chunks, stride=tc), :].astype(BF16)
        xb_ref[ctx_chunks:nch, t * lanes:(t + 1) * lanes] = zx_ref[pl.ds(t, x_chunks, stride=tc), :].astype(BF16)
    xb_ref[...] = jnp.dot(xb_ref[...], pf_ref[...], preferred_element_type=F32).astype(BF16)

    nslab = gs * nst // lanes
    half = nslab // 2
    for j in range(gs):
        s = jnp.dot(xb_ref[:, j * kd:(j + 1) * kd], wst_ref[j], preferred_element_type=F32)
        sh_ref[j * pitch:j * pitch + nch, :] = s[:, 0:lanes]
        sh_ref[(half + j) * pitch:(half + j) * pitch + nch, :] = s[:, lanes:2 * lanes]

    fwd = lax.broadcasted_iota(jnp.int32, (half, lanes), 1) < (lanes // 2)
    a_re = are_ref[...]
    a_im = aim_ref[...]

    def rows(c, part):
        return pl.ds(part * half * pitch + c, half, stride=pitch)

    def step(i, carry):
        hr, hi = carry
        cf = i
        cb = jnp.where(i < ctx_chunks, ctx_chunks - 1 - i, nch - 1 + ctx_chunks - i)
        same = cf == cb
        sfr, sfi = sh_ref[rows(cf, 0), :], sh_ref[rows(cf, 1), :]
        sbr, sbi = sh_ref[rows(cb, 0), :], sh_ref[rows(cb, 1), :]
        sh_ref[rows(cf, 0), :] = jnp.where(fwd, hr, sfr)
        sh_ref[rows(cf, 1), :] = jnp.where(fwd, hi, sfi)
        sh_ref[rows(cb, 0), :] = jnp.where(fwd, jnp.where(same, hr, sbr), hr)
        sh_ref[rows(cb, 1), :] = jnp.where(fwd, jnp.where(same, hi, sbi), hi)
        sr = jnp.where(fwd, sfr, sbr)
        si = jnp.where(fwd, sfi, sbi)
        return a_re * hr - a_im * hi + sr, a_re * hi + a_im * hr + si

    zero = jnp.zeros((half, lanes), F32)
    lax.fori_loop(0, nch, step, (zero, zero))

    for j in range(gs):
        hb = jnp.concatenate([sh_ref[j * pitch:j * pitch + nch, :],
                              sh_ref[(half + j) * pitch:(half + j) * pitch + nch, :]], axis=1).astype(BF16)
        yj = (jnp.dot(xb_ref[:, j * kd:(j + 1) * kd], m_ref[j], preferred_element_type=F32)
              + jnp.dot(hb, wo_ref[j], preferred_element_type=F32))
        yb_ref[:, j * kd:(j + 1) * kd] = yj.astype(BF16)
    y = jnp.dot(yb_ref[...], pb_ref[...], preferred_element_type=F32)
    for t in range(tc):
        oc_ref[pl.ds(t, ctx_chunks, stride=tc), :] = y[0:ctx_chunks, t * lanes:(t + 1) * lanes]
        ox_ref[pl.ds(t, x_chunks, stride=tc), :] = y[ctx_chunks:nch, t * lanes:(t + 1) * lanes]


def _s5_scan(zs, tables, bsz, seq, n_ctx):
    m, w_st, w_o, a_re, a_im = tables
    ns, gs, kd, nst = w_st.shape
    tc = SSM_CHUNK
    lanes = S5_SET
    kset = gs * kd
    ctx_chunks, x_chunks = n_ctx // tc, seq // tc
    nch = ctx_chunks + x_chunks
    pitch = -(-nch // 8) * 8
    if (pitch // 8) % 2 == 0:
        pitch += 8
    ctx0 = bsz * seq // n_ctx
    perm_f, perm_b = _chunk_perm(tc, gs, kd // tc)
    kern = functools.partial(_s5_kernel, ctx_chunks=ctx_chunks, x_chunks=x_chunks, pitch=pitch)
    per_set = lambda arr: pl.BlockSpec((None,) + arr.shape[1:], lambda s, b: (s,) + (0,) * (arr.ndim - 1))
    return pl.pallas_call(
        kern,
        grid=(ns, bsz),
        in_specs=[pl.BlockSpec((seq, lanes), lambda s, b: (b, s)),
                  pl.BlockSpec((n_ctx, lanes), lambda s, b: (ctx0 + b, s)),
                  pl.BlockSpec((kset, kset), lambda s, b: (0, 0)),
                  pl.BlockSpec((kset, kset), lambda s, b: (0, 0)),
                  per_set(m), per_set(w_st), per_set(w_o), per_set(a_re), per_set(a_im)],
        out_specs=[pl.BlockSpec((seq, lanes), lambda s, b: (b, s)),
                   pl.BlockSpec((n_ctx, lanes), lambda s, b: (b, s))],
        out_shape=[jax.ShapeDtypeStruct((bsz * seq, zs.shape[1]), F32),
                   jax.ShapeDtypeStruct((bsz * n_ctx, zs.shape[1]), F32)],
        scratch_shapes=[pltpu.VMEM((gs * nst // lanes * pitch, lanes), F32),
                        pltpu.VMEM((nch, kset), BF16),
                        pltpu.VMEM((nch, kset), BF16)],
        compiler_params=_cp(("parallel", "parallel")),
        name="s5_scan",
    )(zs, zs, perm_f, perm_b, m, w_st, w_o, a_re, a_im)


def _glu_kernel(yx_ref, yc_ref, w_ref, b_ref, gain_ref, o_ref, *, nx_tiles):
    y = jnp.where(pl.program_id(0) < nx_tiles, yx_ref[...], yc_ref[...])
    g = jax.nn.gelu(y)
    v = g * jax.nn.sigmoid(jnp.dot(g.astype(BF16), w_ref[...], preferred_element_type=F32) + b_ref[...])
    r = lax.rsqrt(jnp.mean(v * v, axis=-1, keepdims=True) + EPS)
    o_ref[...] = (v * r * gain_ref[...]).astype(o_ref.dtype)


def _glu(yx, yc, w, b, gain):
    ds = yx.shape[1]
    tm = 512
    nx_tiles, nc_tiles = yx.shape[0] // tm, yc.shape[0] // tm
    return pl.pallas_call(
        functools.partial(_glu_kernel, nx_tiles=nx_tiles),
        grid=(nx_tiles + nc_tiles,),
        in_specs=[pl.BlockSpec((tm, ds), lambda i: (jnp.minimum(i, nx_tiles - 1), 0)),
                  pl.BlockSpec((tm, ds), lambda i: (jnp.maximum(i - nx_tiles, 0), 0)),
                  pl.BlockSpec((ds, ds), lambda i: (0, 0)),
                  pl.BlockSpec((1, ds), lambda i: (0, 0)),
                  pl.BlockSpec((1, ds), lambda i: (0, 0))],
        out_specs=pl.BlockSpec((tm, ds), lambda i: (i, 0)),
        out_shape=jax.ShapeDtypeStruct((yx.shape[0] + yc.shape[0], ds), BF16),
        compiler_params=_cp(("parallel",)),
        name="glu",
    )(yx, yc, w, b, gain)


def _route(scores, sel, n_groups):
    epg = len(sel) // n_groups
    gscore = []
    for q in range(n_groups):
        v = sel[q * epg:(q + 1) * epg]
        best = None
        for a in range(epg):
            for b in range(a + 1, epg):
                s = v[a] + v[b]
                best = s if best is None else jnp.maximum(best, s)
        gscore.append(best)
    gbest = gscore[0]
    gidx = jnp.zeros(gbest.shape, jnp.int32)
    for q in range(1, n_groups):
        upd = gscore[q] > gbest
        gbest = jnp.where(upd, gscore[q], gbest)
        gidx = jnp.where(upd, q, gidx)
    vin = list(sel[:epg])
    sin = list(scores[:epg])
    for q in range(1, n_groups):
        pick = gidx == q
        for j in range(epg):
            vin[j] = jnp.where(pick, sel[q * epg + j], vin[j])
            sin[j] = jnp.where(pick, scores[q * epg + j], sin[j])
    b1 = vin[0]
    i1 = jnp.zeros(gbest.shape, jnp.int32)
    for j in range(1, epg):
        upd = vin[j] > b1
        b1 = jnp.where(upd, vin[j], b1)
        i1 = jnp.where(upd, j, i1)
    b2 = vin[0]
    i2 = jnp.zeros(gbest.shape, jnp.int32)
    have = jnp.zeros(gbest.shape, jnp.bool_)
    for j in range(epg):
        cand = i1 != j
        upd = cand & (jnp.logical_not(have) | (vin[j] > b2))
        b2 = jnp.where(upd, vin[j], b2)
        i2 = jnp.where(upd, j, i2)
        have = have | cand
    s1 = sin[0]
    s2 = sin[0]
    for j in range(1, epg):
        s1 = jnp.where(i1 == j, sin[j], s1)
        s2 = jnp.where(i2 == j, sin[j], s2)
    tot = s1 + s2
    return (gidx * epg + i1, gidx * epg + i2), (s1 / tot, s2 / tot)


def _merge_kernel(t_ref, yf_ref, ys_ref, wo_ref, g1_ref, n2_ref, sc2_ref, sh2_ref, rwh_ref, rwl_ref, rb_ref,
                  tn_ref, h3_ref, eidx_ref, ew_ref):
    df = yf_ref.shape[1]
    tm, d = t_ref.shape
    o = (jnp.dot(yf_ref[...], wo_ref[0:df, :], preferred_element_type=F32)
         + jnp.dot(ys_ref[...], wo_ref[df:, :], preferred_element_type=F32))
    tn = t_ref[...] + g1_ref[0] * o
    tn_ref[...] = tn
    h2 = tn * lax.rsqrt(jnp.mean(tn * tn, axis=-1, keepdims=True) + EPS) * n2_ref[...]
    h2 = h2 * (1.0 + sc2_ref[0]) + sh2_ref[0]
    nsub = d // LANES
    for c in range(nsub):
        h3_ref[pl.ds(c, tm, stride=nsub), :] = h2[:, c * LANES:(c + 1) * LANES]
    h_hi = h2.astype(BF16)
    h_lo = (h2 - h_hi.astype(F32)).astype(BF16)
    lg = (jnp.dot(h_hi, rwh_ref[...], preferred_element_type=F32)
          + jnp.dot(h_lo, rwh_ref[...], preferred_element_type=F32)
          + jnp.dot(h_hi, rwl_ref[...], preferred_element_type=F32))
    ne = rb_ref.shape[0]
    logits = lg.T[0:ne, :]
    scores = jax.nn.sigmoid(logits)
    sel = scores + rb_ref[...]
    srows = [scores[e:e + 1, :] for e in range(ne)]
    vrows = [sel[e:e + 1, :] for e in range(ne)]
    (e1, e2), (w1, w2) = _route(srows, vrows, N_EXPERT_GROUPS)
    eidx_ref[...] = jnp.zeros_like(eidx_ref)
    ew_ref[...] = jnp.zeros_like(ew_ref)
    eidx_ref[0:1, :] = e1
    eidx_ref[1:2, :] = e2
    ew_ref[0:1, :] = w1
    ew_ref[1:2, :] = w2


def _merge(t, yf, ys, wo, g1, n2, sc2, sh2, rwh, rwl, rb, n_x_rows, seq):
    nt, d = t.shape
    df = yf.shape[1]
    ne = rb.shape[0]
    nsub = d // LANES
    tm = ROW_TILE
    mrow = functools.partial(_mod_row, tm=tm, n_x_rows=n_x_rows, seq=seq)
    mspec = pl.BlockSpec((1, 1, d), lambda i: (mrow(i), 0, 0))
    return pl.pallas_call(
        _merge_kernel,
        grid=(nt // tm,),
        in_specs=[pl.BlockSpec((tm, d), lambda i: (i, 0)),
                  pl.BlockSpec((tm, df), lambda i: (i, 0)),
                  pl.BlockSpec((tm, d - df), lambda i: (i, 0)),
                  pl.BlockSpec((d, d), lambda i: (0, 0)),
                  mspec,
                  pl.BlockSpec((1, d), lambda i: (0, 0)),
                  mspec, mspec,
                  pl.BlockSpec((d, LANES), lambda i: (0, 0)),
                  pl.BlockSpec((d, LANES), lambda i: (0, 0)),
                  pl.BlockSpec((ne, 1), lambda i: (0, 0))],
        out_specs=[pl.BlockSpec((tm, d), lambda i: (i, 0)),
                   pl.BlockSpec((tm * nsub, LANES), lambda i: (i, 0)),
                   pl.BlockSpec((8, tm), lambda i: (0, i)),
                   pl.BlockSpec((8, tm), lambda i: (0, i))],
        out_shape=[jax.ShapeDtypeStruct((nt, d), F32),
                   jax.ShapeDtypeStruct((nt * nsub, LANES), F32),
                   jax.ShapeDtypeStruct((8, nt), jnp.int32),
                   jax.ShapeDtypeStruct((8, nt), F32)],
        compiler_params=_cp(("parallel",)),
        name="merge_route",
    )(t, yf, ys, wo, g1, n2, sc2, sh2, rwh, rwl, rb)


def _dispatch(eidx, ew, n_experts, rows, n_blocks):
    nt = eidx.shape[1]
    a = nt * TOP_K
    flat_e = eidx[:TOP_K].T.reshape(a)
    onehot = (flat_e[:, None] == jnp.arange(n_experts, dtype=jnp.int32)[None, :]).astype(jnp.int32)
    csum = jnp.cumsum(onehot, axis=0)
    counts = csum[-1]
    padded = (counts + rows - 1) // rows * rows
    pad_end = jnp.cumsum(padded)
    pad_start = pad_end - padded
    dest = jnp.sum(onehot * (csum - 1 + pad_start[None, :]), axis=1)
    tok = jnp.arange(a, dtype=jnp.int32) // TOP_K
    buf_tok = jnp.zeros((n_blocks * rows,), jnp.int32).at[dest].set(tok)
    n_valid = (pad_end[-1] // rows).astype(jnp.int32)
    blk_start = jnp.arange(n_blocks, dtype=jnp.int32) * rows
    blk_exp = jnp.sum((pad_end[None, :] <= blk_start[:, None]).astype(jnp.int32), axis=1)
    blk_exp = jnp.minimum(blk_exp, n_experts - 1)
    last_exp = jnp.sum(jnp.where(jnp.arange(n_blocks) == n_valid - 1, blk_exp, 0))
    blk_exp = jnp.where(jnp.arange(n_blocks) < n_valid, blk_exp, last_exp).astype(jnp.int32)
    return buf_tok, blk_exp, n_valid.reshape(1), dest.reshape(nt, TOP_K), ew[:TOP_K].T


def _start_row_copies(n_rows, src_hbm, src_row, dst, sem, nsub):
    def body(q, c):
        for k in range(ROW_COPY_UNROLL):
            r = q * ROW_COPY_UNROLL + k
            src = src_hbm.at[pl.ds(pl.multiple_of(src_row(r) * nsub, nsub), nsub)]
            pltpu.make_async_copy(src, dst.at[pl.ds(pl.multiple_of(r * GATHER_PITCH, 8), nsub)], sem).start()
        return c

    lax.fori_loop(0, n_rows // ROW_COPY_UNROLL, body, 0)


def _wait_row_copies(n_rows, src_hbm, dst, sem, nsub):
    pltpu.make_async_copy(src_hbm.at[pl.ds(0, n_rows * nsub)], dst.at[pl.ds(0, n_rows * nsub)], sem).wait()


def _expert_kernel(be_ref, nv_ref, tok_ref, tokn_ref, h_hbm, wg_hbm, wu_hbm, wd_hbm, o_ref,
                   wgb_ref, wub_ref, wdb_ref, stg_ref, std_ref, xg_ref, xb_ref, sem, wsem, *, nsub, layer):
    i = pl.program_id(0)
    rows = xb_ref.shape[0]
    valid = i < nv_ref[0]
    slot = i % 2
    e = be_ref[i]
    first = (i == 0) | (be_ref[jnp.maximum(i - 1, 0)] != e)

    def gather(toks, s):
        _start_row_copies(rows, h_hbm, lambda r: toks[0, r], xg_ref.at[s], sem.at[s], nsub)

    @pl.when(i == 0)
    def _():
        gather(tok_ref, 0)

    next_valid = (i + 1 < nv_ref[0]) & (i + 1 < pl.num_programs(0))
    for s in range(2):
        @pl.when(next_valid & (slot == 1 - s))
        def _():
            gather(tokn_ref, s)

    @pl.when(valid & first)
    def _():
        chunks = []
        for src, dst, stg in ((wg_hbm, wgb_ref, stg_ref), (wu_hbm, wub_ref, stg_ref), (wd_hbm, wdb_ref, std_ref)):
            nr = stg.shape[1]
            chunks += [(src, dst, stg, k * nr, nr) for k in range(dst.shape[0] // nr)]

        def copy(k):
            src, _, stg, r0, nr = chunks[k]
            return pltpu.make_async_copy(src.at[layer, e, pl.ds(r0, nr), :], stg.at[k % 2], wsem.at[k % 2])

        copy(0).start()
        for k, (_, dst, stg, r0, nr) in enumerate(chunks):
            if k + 1 < len(chunks):
                copy(k + 1).start()
            copy(k).wait()
            dst[r0:r0 + nr, :] = stg[k % 2].astype(BF16)

    @pl.when(valid)
    def _():
        for s in range(2):
            @pl.when(slot == s)
            def _():
                _wait_row_copies(rows, h_hbm, xg_ref.at[s], sem.at[s], nsub)
                for c in range(nsub):
                    xb_ref[:, c * LANES:(c + 1) * LANES] = (
                        xg_ref[s, pl.ds(c, rows, stride=GATHER_PITCH), :].astype(BF16))
        x = xb_ref[...]
        g = jnp.dot(x, wgb_ref[...], preferred_element_type=F32)
        u = jnp.dot(x, wub_ref[...], preferred_element_type=F32)
        hmid = (g * jax.nn.sigmoid(g)) * u
        y = jnp.dot(hmid.astype(BF16), wdb_ref[...], preferred_element_type=F32)
        for c in range(nsub):
            o_ref[pl.ds(c, rows, stride=nsub), :] = y[:, c * LANES:(c + 1) * LANES]

    @pl.when(jnp.logical_not(valid))
    def _():
        o_ref[...] = jnp.zeros_like(o_ref)


def _experts(h3, buf_tok, blk_exp, n_valid, w_gate, w_up, w_down, layer):
    _, ne, d, de = w_gate.shape
    nsub = d // LANES
    rows = MOE_ROWS
    n_blocks = buf_tok.shape[0] // rows
    any_spec = pl.BlockSpec(memory_space=pl.ANY)
    grid_spec = pltpu.PrefetchScalarGridSpec(
        num_scalar_prefetch=2,
        grid=(n_blocks,),
        in_specs=[pl.BlockSpec((None, 1, rows), lambda i, be, nv: (i, 0, 0), memory_space=pltpu.SMEM),
                  pl.BlockSpec((None, 1, rows), lambda i, be, nv: (jnp.minimum(i + 1, n_blocks - 1), 0, 0),
                               memory_space=pltpu.SMEM),
                  any_spec, any_spec, any_spec, any_spec],
        out_specs=pl.BlockSpec((rows * nsub, LANES), lambda i, be, nv: (i, 0)),
        scratch_shapes=[pltpu.VMEM((d, de), BF16),
                        pltpu.VMEM((d, de), BF16),
                        pltpu.VMEM((de, d), BF16),
                        pltpu.VMEM((2, d // MOE_WCHUNKS, de), F32),
                        pltpu.VMEM((2, de // MOE_WCHUNKS, d), F32),
                        pltpu.VMEM((2, rows * GATHER_PITCH, LANES), F32),
                        pltpu.VMEM((rows, d), BF16),
                        pltpu.SemaphoreType.DMA((2,)),
                        pltpu.SemaphoreType.DMA((2,))],
    )
    toks = buf_tok.reshape(n_blocks, 1, rows)
    return pl.pallas_call(
        functools.partial(_expert_kernel, nsub=nsub, layer=layer),
        grid_spec=grid_spec,
        out_shape=jax.ShapeDtypeStruct((n_blocks * rows * nsub, LANES), F32),
        compiler_params=_cp(("arbitrary",)),
        name="experts",
    )(blk_exp, n_valid, toks, toks, h3, w_gate, w_up, w_down)


def _combine_kernel(pos_ref, posn_ref, t_ref, w_ref, yb_hbm, g2_ref, o_ref, gk_ref, sem, *, nsub):
    i = pl.program_id(0)
    tm = t_ref.shape[0]
    slot = i % 2

    def gather(pos, s):
        for k in range(TOP_K):
            _start_row_copies(tm, yb_hbm, lambda r, k=k: pos[0, TOP_K * r + k], gk_ref.at[s, k], sem.at[s], nsub)

    @pl.when(i == 0)
    def _():
        gather(pos_ref, 0)

    for s in range(2):
        @pl.when((i + 1 < pl.num_programs(0)) & (slot == 1 - s))
        def _():
            gather(posn_ref, s)

    w0 = w_ref[:, 0:1]
    w1 = w_ref[:, 1:2]
    for s in range(2):
        @pl.when(slot == s)
        def _():
            for k in range(TOP_K):
                _wait_row_copies(tm, yb_hbm, gk_ref.at[s, k], sem.at[s], nsub)
            for c in range(nsub):
                sl = slice(c * LANES, (c + 1) * LANES)
                y = (gk_ref[s, 0, pl.ds(c, tm, stride=GATHER_PITCH), :] * w0
                     + gk_ref[s, 1, pl.ds(c, tm, stride=GATHER_PITCH), :] * w1)
                o_ref[:, sl] = t_ref[:, sl] + g2_ref[0][:, sl] * y


def _combine(t, yb3, pos, w, g2, n_x_rows, seq):
    nt, d = t.shape
    nsub = d // LANES
    tm = ROW_TILE
    mrow = functools.partial(_mod_row, tm=tm, n_x_rows=n_x_rows, seq=seq)
    nb = nt // tm
    posb = pos.reshape(nb, 1, TOP_K * tm)
    return pl.pallas_call(
        functools.partial(_combine_kernel, nsub=nsub),
        grid=(nb,),
        in_specs=[pl.BlockSpec((None, 1, TOP_K * tm), lambda i: (i, 0, 0), memory_space=pltpu.SMEM),
                  pl.BlockSpec((None, 1, TOP_K * tm), lambda i: (jnp.minimum(i + 1, nb - 1), 0, 0),
                               memory_space=pltpu.SMEM),
                  pl.BlockSpec((tm, d), lambda i: (i, 0)),
                  pl.BlockSpec((tm, TOP_K), lambda i: (i, 0)),
                  pl.BlockSpec(memory_space=pl.ANY),
                  pl.BlockSpec((1, 1, d), lambda i: (mrow(i), 0, 0))],
        out_specs=pl.BlockSpec((tm, d), lambda i: (i, 0)),
        out_shape=jax.ShapeDtypeStruct((nt, d), F32),
        scratch_shapes=[pltpu.VMEM((2, TOP_K, tm * GATHER_PITCH, LANES), F32),
                        pltpu.SemaphoreType.DMA((2,))],
        compiler_params=_cp(("arbitrary",)),
        name="combine",
    )(posb, posb, t, w, yb3, g2)


def _final_kernel(x_ref, g_ref, o_ref):
    x = x_ref[...]
    o_ref[...] = x * lax.rsqrt(jnp.mean(x * x, axis=-1, keepdims=True) + EPS) * g_ref[...]


def _final_norm(t, g, n_rows):
    d = t.shape[1]
    tm = ROW_TILE
    return pl.pallas_call(
        _final_kernel,
        grid=(n_rows // tm,),
        in_specs=[pl.BlockSpec((tm, d), lambda i: (i, 0)),
                  pl.BlockSpec((1, d), lambda i: (0, 0))],
        out_specs=pl.BlockSpec((tm, d), lambda i: (i, 0)),
        out_shape=jax.ShapeDtypeStruct((n_rows, d), F32),
        compiler_params=_cp(("parallel",)),
        name="final_norm",
    )(t, g)


def kernel(x, c, ctx, c_ctx, w_mod, b_mod, norm1_g, norm2_g, w_in, w_out, fourier_w, mix_norm_g,
           lam_re, lam_im, log_dt, b_re, b_im, c_re, c_im, d_skip, glu_w, glu_b,
           router_w, router_b, w_gate, w_up, w_down, final_g):
    bsz, seq, d = x.shape
    n_ctx = ctx.shape[1]
    depth = w_mod.shape[0]
    df = fourier_w.shape[1] * fourier_w.shape[2]
    ds = d_skip.shape[1]
    ne = router_w.shape[1]
    nx_rows = bsz * seq
    nt = nx_rows + bsz * n_ctx

    t = jnp.concatenate([x.reshape(nx_rows, d), ctx.reshape(bsz * n_ctx, d)], axis=0).astype(F32)

    cvec = jnp.concatenate([c_ctx[None, :], c, jnp.zeros((8 - 1 - bsz, d), c.dtype)], axis=0).astype(F32)
    mod = _adaln(cvec, w_mod, b_mod).reshape(depth, 8, N_MOD, 1, d)

    cc, sc_ = _dft_tables(df)
    w_chan = jnp.concatenate([cc, sc_], axis=1).astype(BF16)
    tabs_x = _half_tables(seq)
    tabs_c = _half_tables(n_ctx)

    rw = jnp.pad(router_w.astype(F32), ((0, 0), (0, LANES - ne)))
    rwh = rw.astype(BF16)
    rwl = (rw - rwh.astype(F32)).astype(BF16)
    rb = router_b.astype(F32).reshape(ne, 1)
    n_blocks = -(-(nt * TOP_K) // MOE_ROWS) + ne

    for l in range(depth):
        sh1, sc1, g1, sh2, sc2, g2 = [mod[l, :, k] for k in range(N_MOD)]
        zf, zs = _inproj(t, norm1_g[l].reshape(1, d).astype(F32), sc1, sh1, w_in[l].astype(BF16),
                         df, nx_rows, seq)

        hw = fourier_w[l].astype(BF16)
        gain = mix_norm_g[l].astype(F32).reshape(1, -1)
        yf = _fourier_mix(zf, w_chan, tabs_x, hw, gain[:, :df], 0, bsz, seq, None)
        yf = _fourier_mix(zf, w_chan, tabs_c, hw, gain[:, :df], nx_rows, bsz, n_ctx, yf)

        tables = _s5_tables(lam_re[l], lam_im[l], log_dt[l], b_re[l], b_im[l], c_re[l], c_im[l], d_skip[l])
        ysx, ysc = _s5_scan(zs, tables, bsz, seq, n_ctx)
        ys = _glu(ysx, ysc, glu_w[l].astype(BF16), glu_b[l].astype(F32).reshape(1, ds), gain[:, df:])

        t, h3, eidx, ew = _merge(t, yf, ys, w_out[l].astype(BF16), g1,
                                 norm2_g[l].reshape(1, d).astype(F32), sc2, sh2, rwh, rwl, rb, nx_rows, seq)

        buf_tok, blk_exp, n_valid, pos, wtok = _dispatch(eidx, ew, ne, MOE_ROWS, n_blocks)
        yb = _experts(h3, buf_tok, blk_exp, n_valid, w_gate, w_up, w_down, l)
        t = _combine(t, yb, pos, wtok, g2, nx_rows, seq)

    out = _final_norm(t, final_g.reshape(1, d).astype(F32), nx_rows)
    return out.reshape(bsz, seq, d).astype(x.dtype)
```

```python
import functools
import math

import jax
import jax.numpy as jnp
from jax import lax
from jax.experimental import pallas as pl
from jax.experimental.pallas import tpu as pltpu

F32 = jnp.float32
BF16 = jnp.bfloat16
EPS = 1e-6

FOURIER_HEADS = 4
SSM_GROUP = 16
N_EXPERT_GROUPS = 4
TOP_K = 2
N_MOD = 6

SSM_CHUNK = 16
ROW_TILE = 256
MOE_ROWS = 256
MOE_WCHUNKS = 8
ROW_COPY_UNROLL = 8
LANES = 128
GATHER_PITCH = 24
S5_SET = LANES
VMEM_LIMIT_BYTES = 56 * 1024 * 1024


def _cp(sems):
    return pltpu.CompilerParams(dimension_semantics=sems, vmem_limit_bytes=VMEM_LIMIT_BYTES)


def _mod_row(i, tm, n_x_rows, seq):
    r = i * tm
    return jnp.where(r < n_x_rows, 1 + r // seq, 0)


def _mod_kernel(c_ref, w_ref, b_ref, o_ref):
    c = c_ref[...]
    s = c * jax.nn.sigmoid(c)
    o_ref[...] = jnp.dot(s.astype(BF16), w_ref[...].astype(BF16),
                         preferred_element_type=F32) + b_ref[...]


def _adaln(cvec, w_mod, b_mod):
    depth, d, n = w_mod.shape
    tn = 1024
    return pl.pallas_call(
        _mod_kernel,
        grid=(depth, n // tn),
        in_specs=[pl.BlockSpec((8, d), lambda l, j: (0, 0)),
                  pl.BlockSpec((None, d, tn), lambda l, j: (l, 0, j)),
                  pl.BlockSpec((None, 1, tn), lambda l, j: (l, 0, j))],
        out_specs=pl.BlockSpec((None, 8, tn), lambda l, j: (l, 0, j)),
        out_shape=jax.ShapeDtypeStruct((depth, 8, n), F32),
        compiler_params=_cp(("parallel", "parallel")),
        name="adaln",
    )(cvec, w_mod, b_mod.reshape(depth, 1, n))


def _inproj_kernel(t_ref, g_ref, sc_ref, sh_ref, w_ref, zf_ref, zs_ref):
    x = t_ref[...]
    h = x * lax.rsqrt(jnp.mean(x * x, axis=-1, keepdims=True) + EPS) * g_ref[...]
    h = h * (1.0 + sc_ref[0]) + sh_ref[0]
    z = jnp.dot(h.astype(BF16), w_ref[...], preferred_element_type=F32)
    df = zf_ref.shape[1]
    zf_ref[...] = z[:, :df].astype(zf_ref.dtype)
    zs_ref[...] = z[:, df:]


def _inproj(t, g, sc, sh, w, df, n_x_rows, seq):
    nt, d = t.shape
    n = w.shape[1]
    tm = ROW_TILE
    mrow = functools.partial(_mod_row, tm=tm, n_x_rows=n_x_rows, seq=seq)
    return pl.pallas_call(
        _inproj_kernel,
        grid=(nt // tm,),
        in_specs=[pl.BlockSpec((tm, d), lambda i: (i, 0)),
                  pl.BlockSpec((1, d), lambda i: (0, 0)),
                  pl.BlockSpec((1, 1, d), lambda i: (mrow(i), 0, 0)),
                  pl.BlockSpec((1, 1, d), lambda i: (mrow(i), 0, 0)),
                  pl.BlockSpec((d, n), lambda i: (0, 0))],
        out_specs=[pl.BlockSpec((tm, df), lambda i: (i, 0)),
                   pl.BlockSpec((tm, n - df), lambda i: (i, 0))],
        out_shape=[jax.ShapeDtypeStruct((nt, df), BF16),
                   jax.ShapeDtypeStruct((nt, n - df), F32)],
        compiler_params=_cp(("parallel",)),
        name="inproj",
    )(t, g, sc, sh, w)


def _dft_tables(n):
    r = 1
    while r * r < n:
        r *= 2
    q = n // r
    k = jnp.arange(n, dtype=jnp.int32)[:, None]
    step = 2.0 * math.pi / n
    pa = ((k * (jnp.arange(q, dtype=jnp.int32)[None, :] * r)) % n).astype(F32) * step
    pb = ((k * jnp.arange(r, dtype=jnp.int32)[None, :]) % n).astype(F32) * step
    ca, sa = jnp.cos(pa)[:, :, None], jnp.sin(pa)[:, :, None]
    cb, sb = jnp.cos(pb)[:, None, :], jnp.sin(pb)[:, None, :]
    c = (ca * cb - sa * sb).reshape(n, n)
    s = (sa * cb + ca * sb).reshape(n, n)
    return c, s


def _chan_dft_kernel(z_ref, w_ref, o_ref):
    o_ref[...] = jnp.dot(z_ref[...], w_ref[...], preferred_element_type=F32).astype(o_ref.dtype)


def _chan_dft(z, w):
    nt = z.shape[0]
    df, n = w.shape
    tm = min(512, nt)
    return pl.pallas_call(
        _chan_dft_kernel,
        grid=(nt // tm,),
        in_specs=[pl.BlockSpec((tm, df), lambda i: (i, 0)),
                  pl.BlockSpec((df, n), lambda i: (0, 0))],
        out_specs=pl.BlockSpec((tm, n), lambda i: (i, 0)),
        out_shape=jax.ShapeDtypeStruct((nt, n), BF16),
        compiler_params=_cp(("parallel",)),
        name="chan_dft",
    )(z, w)


def _chan_sym_kernel(z_ref, zr_ref, w_ref, o_ref):
    z = z_ref[...].astype(F32)
    zr = zr_ref[...].astype(F32)
    c = w_ref.shape[0]
    o_ref[:, :c] = jnp.dot((z + zr).astype(BF16), w_ref[:, :c], preferred_element_type=F32).astype(o_ref.dtype)
    o_ref[:, c:] = jnp.dot((z - zr).astype(BF16), w_ref[:, c:], preferred_element_type=F32).astype(o_ref.dtype)


def _chan_sym(z, zr, w, row0, bsz, length):
    df, n = w.shape
    half = length // 2
    tm = min(512, half)
    hb = half // tm
    off = row0 // tm
    return pl.pallas_call(
        _chan_sym_kernel,
        grid=(bsz, hb),
        in_specs=[pl.BlockSpec((tm, df), lambda b, i: (off + b * 2 * hb + i, 0)),
                  pl.BlockSpec((tm, df), lambda b, i: (b * hb + i, 0)),
                  pl.BlockSpec((df, n), lambda b, i: (0, 0))],
        out_specs=pl.BlockSpec((tm, n), lambda b, i: (b * hb + i, 0)),
        out_shape=jax.ShapeDtypeStruct((bsz * half, n), BF16),
        compiler_params=_cp(("parallel", "parallel")),
        name="chan_sym",
    )(z, zr, w)


def _fourier_epilogue(f, hw_ref, gain_ref, o_ref):
    nh, hd, _ = hw_ref.shape
    ys = [jnp.dot(f[:, h * hd:(h + 1) * hd].astype(BF16), hw_ref[h],
                  preferred_element_type=F32) for h in range(nh)]
    ssum = jnp.sum(ys[0] * ys[0], axis=-1, keepdims=True)
    for h in range(1, nh):
        ssum = ssum + jnp.sum(ys[h] * ys[h], axis=-1, keepdims=True)
    r = lax.rsqrt(ssum / (nh * hd) + EPS)
    for h in range(nh):
        o_ref[:, h * hd:(h + 1) * hd] = (ys[h] * r * gain_ref[:, h * hd:(h + 1) * hd]).astype(o_ref.dtype)


def _row_sym_kernel(ch_ref, sh_ref, alt_ref, ae_ref, bo_ref, amid_ref, hw_ref, gain_ref, lo_ref, hi_ref,
                    p_ref, q_ref, r_ref, *, scale, mid_sign):
    m = pl.program_id(1)
    k = pl.program_id(2)

    @pl.when(k == 0)
    def _():
        p_ref[...] = jnp.zeros_like(p_ref)
        q_ref[...] = jnp.zeros_like(q_ref)
        r_ref[...] = jnp.zeros_like(r_ref)

    ae = ae_ref[...]
    p_ref[...] += jnp.dot(ch_ref[...], ae, preferred_element_type=F32)
    q_ref[...] += jnp.dot(sh_ref[...], bo_ref[...], preferred_element_type=F32)

    @pl.when(m == 0)
    def _():
        r_ref[...] += jnp.dot(alt_ref[...], ae, preferred_element_type=F32)

    @pl.when(k == pl.num_programs(2) - 1)
    def _():
        tm = p_ref.shape[0]
        row = lax.broadcasted_iota(jnp.int32, (tm, 1), 0) + m * tm
        amid = amid_ref[0:1, :]
        pf = p_ref[...] + jnp.where((row & 1) == 0, 1.0, -1.0) * amid
        q = q_ref[...]
        mid = r_ref[0:1, :] + mid_sign * amid
        hi = jnp.where(row == 0, mid, pf + q)
        _fourier_epilogue((pf - q) * scale, hw_ref, gain_ref, lo_ref)
        _fourier_epilogue(hi * scale, hw_ref, gain_ref, hi_ref)


def _without_alias_ref(kernel, pos):
    def wrapped(*refs):
        return kernel(*refs[:pos], *refs[pos + 1:])
    return wrapped


def _row_sym(aebo, amid, ch, sh, alt, hw, gain, bsz, length, row0, ybuf, n_rows):
    df = aebo.shape[1] // 2
    half = length // 2
    tm = tk = min(512, half)
    mb, kb = half // tm, half // tk
    scale = 1.0 / math.sqrt(length * df)
    mid_sign = 1.0 if half % 2 == 0 else -1.0
    off = row0 // tm
    kern = functools.partial(_row_sym_kernel, scale=scale, mid_sign=mid_sign)
    in_specs = [pl.BlockSpec((tm, tk), lambda b, m, k: (m, k)),
                pl.BlockSpec((tm, tk), lambda b, m, k: (m, k)),
                pl.BlockSpec((8, tk), lambda b, m, k: (0, k)),
                pl.BlockSpec((tk, df), lambda b, m, k: (b * kb + k, 0)),
                pl.BlockSpec((tk, df), lambda b, m, k: (b * kb + k, 1)),
                pl.BlockSpec((8, df), lambda b, m, k: (b, 0)),
                pl.BlockSpec(hw.shape, lambda b, m, k: (0, 0, 0)),
                pl.BlockSpec((1, df), lambda b, m, k: (0, 0))]
    args = [ch, sh, alt, aebo, aebo, amid, hw, gain]
    aliases = {}
    if ybuf is not None:
        kern = _without_alias_ref(kern, len(args))
        in_specs.append(pl.BlockSpec(memory_space=pl.ANY))
        aliases = {len(args): 0}
        args.append(ybuf)
    return pl.pallas_call(
        kern,
        grid=(bsz, mb, kb),
        in_specs=in_specs,
        out_specs=[pl.BlockSpec((tm, df), lambda b, m, k: (off + b * 2 * mb + m, 0)),
                   pl.BlockSpec((tm, df), lambda b, m, k: (b * mb + m, 0))],
        out_shape=[jax.ShapeDtypeStruct((n_rows, df), BF16),
                   jax.ShapeDtypeStruct((bsz * half, df), BF16)],
        scratch_shapes=[pltpu.VMEM((tm, df), F32), pltpu.VMEM((tm, df), F32), pltpu.VMEM((8, df), F32)],
        input_output_aliases=aliases,
        compiler_params=_cp(("parallel", "parallel", "arbitrary")),
        name="row_sym",
    )(*args)


def _mirror_kernel(j_ref, a_ref, b_ref, o_ref, *, keep_first):
    rev = jnp.dot(j_ref[...], a_ref[...], preferred_element_type=F32)
    first = b_ref[0:1, :].astype(F32)
    if not keep_first:
        first = jnp.where(pl.program_id(1) == 0, 0.0, first)
    row = lax.broadcasted_iota(jnp.int32, (o_ref.shape[0], 1), 0)
    o_ref[...] = jnp.where(row == 0, first, rev).astype(o_ref.dtype)


def _mirror(src, src_row0, src_stride, half, bsz, keep_first, dst_row0=0, dst_stride=None, ybuf=None):
    df = src.shape[1]
    tm = min(512, half)
    nb = half // tm
    dst_stride = half if dst_stride is None else dst_stride
    soff, sstr, doff, dstr = src_row0 // tm, src_stride // tm, dst_row0 // tm, dst_stride // tm
    r = jnp.arange(tm, dtype=jnp.int32)
    jmat = ((r[:, None] + r[None, :] == tm) & (r[:, None] > 0)).astype(BF16)
    kern = functools.partial(_mirror_kernel, keep_first=keep_first)
    in_specs = [pl.BlockSpec((tm, tm), lambda b, i: (0, 0)),
                pl.BlockSpec((tm, df), lambda b, i: (soff + b * sstr + nb - 1 - i, 0)),
                pl.BlockSpec((tm, df), lambda b, i: (soff + b * sstr + (nb - i) % nb, 0))]
    args = [jmat, src, src]
    aliases = {}
    n_rows = bsz * half
    if ybuf is not None:
        kern = _without_alias_ref(kern, len(args))
        in_specs.append(pl.BlockSpec(memory_space=pl.ANY))
        aliases = {len(args): 0}
        args.append(ybuf)
        n_rows = ybuf.shape[0]
    return pl.pallas_call(
        kern,
        grid=(bsz, nb),
        in_specs=in_specs,
        out_specs=pl.BlockSpec((tm, df), lambda b, i: (doff + b * dstr + i, 0)),
        out_shape=jax.ShapeDtypeStruct((n_rows, df), BF16),
        input_output_aliases=aliases,
        compiler_params=_cp(("parallel", "parallel")),
        name="mirror",
    )(*args)


def _fourier_mix(zf, w_chan, tabs, hw, gain, row0, bsz, length, ybuf):
    ch, sh, alt = tabs
    df = zf.shape[1]
    half = length // 2
    zr = _mirror(zf, row0 + half, length, half, bsz, keep_first=False)
    aebo = _chan_sym(zf, zr, w_chan, row0, bsz, length)
    zmid = zf[row0:row0 + bsz * length].reshape(bsz, length, df)[:, half]
    zmid = jnp.pad(zmid[:, None, :], ((0, 0), (0, 7), (0, 0)))
    amid = _chan_dft(zmid.reshape(bsz * 8, df), w_chan)[:, :df].astype(F32)
    y, hi = _row_sym(aebo, amid, ch, sh, alt, hw, gain, bsz, length, row0, ybuf, zf.shape[0])
    return _mirror(hi, 0, half, half, bsz, keep_first=True, dst_row0=row0 + half, dst_stride=length, ybuf=y)


def _half_tables(length):
    half = length // 2
    c, s = _dft_tables(length)
    n = jnp.arange(half, dtype=jnp.int32)
    alt = jnp.where((lax.broadcasted_iota(jnp.int32, (8, half), 0) == 0),
                    jnp.where(n % 2 == 0, 1.0, -1.0)[None, :], 0.0)
    return c[:half, :half].astype(BF16), s[:half, :half].astype(BF16), alt.astype(BF16)


def _s5_tables(lam_re, lam_im, log_dt, b_re, b_im, c_re, c_im, d_skip):
    tc = SSM_CHUNK
    g, p = lam_re.shape[1:]
    hch = b_re.shape[-1]
    lr, li = lam_re.astype(F32), lam_im.astype(F32)
    dt = jnp.exp(log_dt.astype(F32))[..., None]
    er, ei = lr * dt, li * dt

    def lpow(k):
        m = jnp.exp(er * k)
        return m * jnp.cos(ei * k), m * jnp.sin(ei * k)

    l1r, l1i = lpow(1.0)
    den = lr * lr + li * li
    qr = ((l1r - 1.0) * lr + l1i * li) / den
    qi = (l1i * lr - (l1r - 1.0) * li) / den
    br, bi = b_re.astype(F32), b_im.astype(F32)
    bbr = qr[..., None] * br - qi[..., None] * bi
    bbi = qr[..., None] * bi + qi[..., None] * br
    cr, ci = c_re.astype(F32), c_im.astype(F32)

    ks = jnp.arange(tc + 1, dtype=F32)
    pwr = jnp.stack([lpow(k)[0] for k in range(tc + 1)], axis=-1)
    pwi = jnp.stack([lpow(k)[1] for k in range(tc + 1)], axis=-1)
    del ks

    cbr = jnp.einsum("dgip,dgpj->dgpij", cr, bbr) - jnp.einsum("dgip,dgpj->dgpij", ci, bbi)
    cbi = jnp.einsum("dgip,dgpj->dgpij", cr, bbi) + jnp.einsum("dgip,dgpj->dgpij", ci, bbr)
    klag = (jnp.einsum("dgpij,dgpl->dglij", cbr, pwr[..., :tc])
            - jnp.einsum("dgpij,dgpl->dglij", cbi, pwi[..., :tc]))

    t_in = jnp.arange(tc)[:, None]
    t_out = jnp.arange(tc)[None, :]
    lag_f = t_out - t_in
    lag_b = t_in - t_out
    kf = klag[0][:, jnp.clip(lag_f, 0, tc - 1)] * (lag_f >= 0)[None, :, :, None, None]
    kb = klag[1][:, jnp.clip(lag_b, 0, tc - 1)] * (lag_b >= 0)[None, :, :, None, None]
    eye_t = (t_in == t_out).astype(F32)[None, :, :, None, None]
    dg = d_skip.astype(F32).reshape(g, hch)
    dmat = eye_t * (jnp.eye(hch, dtype=F32) * dg[:, None, :])[:, None, None, :, :]
    m = (kf + kb + dmat).transpose(0, 1, 4, 2, 3).reshape(g, tc * hch, tc * hch)

    idx_f = jnp.arange(tc - 1, -1, -1)
    idx_b = jnp.arange(tc)

    def st(d, idx):
        wr = pwr[d][..., idx][:, :, :, None] * bbr[d][:, :, None, :] - pwi[d][..., idx][:, :, :, None] * bbi[d][:, :, None, :]
        wi = pwr[d][..., idx][:, :, :, None] * bbi[d][:, :, None, :] + pwi[d][..., idx][:, :, :, None] * bbr[d][:, :, None, :]
        return (wr.transpose(0, 2, 3, 1).reshape(g, tc * hch, p),
                wi.transpose(0, 2, 3, 1).reshape(g, tc * hch, p))

    sfr, sfi = st(0, idx_f)
    sbr, sbi = st(1, idx_b)
    w_st = jnp.concatenate([sfr, sbr, sfi, sbi], axis=-1)

    def so(d, idx):
        wr = cr[d][:, None, :, :] * pwr[d][..., idx].transpose(0, 2, 1)[:, :, None, :] \
            - ci[d][:, None, :, :] * pwi[d][..., idx].transpose(0, 2, 1)[:, :, None, :]
        wi = cr[d][:, None, :, :] * pwi[d][..., idx].transpose(0, 2, 1)[:, :, None, :] \
            + ci[d][:, None, :, :] * pwr[d][..., idx].transpose(0, 2, 1)[:, :, None, :]
        return (wr.reshape(g, tc * hch, p).transpose(0, 2, 1),
                wi.reshape(g, tc * hch, p).transpose(0, 2, 1))

    ofr, ofi = so(0, jnp.arange(1, tc + 1))
    obr, obi = so(1, jnp.arange(tc, 0, -1))
    zero = jnp.zeros_like(ofr)
    w_of = jnp.concatenate([ofr, zero, -ofi, zero], axis=1)
    w_ob = jnp.concatenate([zero, obr, zero, -obi], axis=1)

    a_re = jnp.concatenate([pwr[0][..., tc], pwr[1][..., tc]], axis=-1)
    a_im = jnp.concatenate([pwi[0][..., tc], pwi[1][..., tc]], axis=-1)

    gs = S5_SET // hch
    ns = g // gs
    kd = tc * hch
    w_o = w_of + w_ob
    return (m.astype(BF16).reshape(ns, gs, kd, kd),
            w_st.astype(BF16).reshape(ns, gs, kd, 4 * p),
            w_o.astype(BF16).reshape(ns, gs, 4 * p, kd),
            a_re.reshape(ns, gs, 2 * p), a_im.reshape(ns, gs, 2 * p))


def _chunk_perm(tc, gs, hch):
    n = tc * gs * hch
    r = jnp.arange(n, dtype=jnp.int32)
    t, g, j = r // (gs * hch), (r // hch) % gs, r % hch
    dst = g * (tc * hch) + t * hch + j
    fwd = (dst[:, None] == r[None, :]).astype(BF16)
    return fwd, fwd.T


def _s5_kernel(zx_ref, zc_ref, pf_ref, pb_ref, m_ref, wst_ref, wo_ref, are_ref, aim_ref, ox_ref, oc_ref,
               sh_ref, xb_ref, yb_ref, *, ctx_chunks, x_chunks, pitch):
    tc = SSM_CHUNK
    nch = ctx_chunks + x_chunks
    lanes = zx_ref.shape[1]
    gs, kd, nst = wst_ref.shape
    for t in range(tc):
        xb_ref[0:ctx_chunks, t * lanes:(t + 1) * lanes] = zc_ref[pl.ds(t, ctx_chunks, stride=tc), :].astype(BF16)
        xb_ref[ctx_chunks:nch, t * lanes:(t + 1) * lanes] = zx_ref[pl.ds(t, x_chunks, stride=tc), :].astype(BF16)
    xb_ref[...] = jnp.dot(xb_ref[...], pf_ref[...], preferred_element_type=F32).astype(BF16)

    nslab = gs * nst // lanes
    half = nslab // 2
    for j in range(gs):
        s = jnp.dot(xb_ref[:, j * kd:(j + 1) * kd], wst_ref[j], preferred_element_type=F32)
        sh_ref[j * pitch:j * pitch + nch, :] = s[:, 0:lanes]
        sh_ref[(half + j) * pitch:(half + j) * pitch + nch, :] = s[:, lanes:2 * lanes]

    fwd = lax.broadcasted_iota(jnp.int32, (half, lanes), 1) < (lanes // 2)
    a_re = are_ref[...]
    a_im = aim_ref[...]

    def rows(c, part):
        return pl.ds(part * half * pitch + c, half, stride=pitch)

    def step(i, carry):
        hr, hi = carry
        cf = i
        cb = jnp.where(i < ctx_chunks, ctx_chunks - 1 - i, nch - 1 + ctx_chunks - i)
        same = cf == cb
        sfr, sfi = sh_ref[rows(cf, 0), :], sh_ref[rows(cf, 1), :]
        sbr, sbi = sh_ref[rows(cb, 0), :], sh_ref[rows(cb, 1), :]
        sh_ref[rows(cf, 0), :] = jnp.where(fwd, hr, sfr)
        sh_ref[rows(cf, 1), :] = jnp.where(fwd, hi, sfi)
        sh_ref[rows(cb, 0), :] = jnp.where(fwd, jnp.where(same, hr, sbr), hr)
        sh_ref[rows(cb, 1), :] = jnp.where(fwd, jnp.where(same, hi, sbi), hi)
        sr = jnp.where(fwd, sfr, sbr)
        si = jnp.where(fwd, sfi, sbi)
        return a_re * hr - a_im * hi + sr, a_re * hi + a_im * hr + si

    zero = jnp.zeros((half, lanes), F32)
    lax.fori_loop(0, nch, step, (zero, zero))

    for j in range(gs):
        hb = jnp.concatenate([sh_ref[j * pitch:j * pitch + nch, :],
                              sh_ref[(half + j) * pitch:(half + j) * pitch + nch, :]], axis=1).astype(BF16)
        yj = (jnp.dot(xb_ref[:, j * kd:(j + 1) * kd], m_ref[j], preferred_element_type=F32)
              + jnp.dot(hb, wo_ref[j], preferred_element_type=F32))
        yb_ref[:, j * kd:(j + 1) * kd] = yj.astype(BF16)
    y = jnp.dot(yb_ref[...], pb_ref[...], preferred_element_type=F32)
    for t in range(tc):
        oc_ref[pl.ds(t, ctx_chunks, stride=tc), :] = y[0:ctx_chunks, t * lanes:(t + 1) * lanes]
        ox_ref[pl.ds(t, x_chunks, stride=tc), :] = y[ctx_chunks:nch, t * lanes:(t + 1) * lanes]


def _s5_scan(zs, tables, bsz, seq, n_ctx):
    m, w_st, w_o, a_re, a_im = tables
    ns, gs, kd, nst = w_st.shape
    tc = SSM_CHUNK
    lanes = S5_SET
    kset = gs * kd
    ctx_chunks, x_chunks = n_ctx // tc, seq // tc
    nch = ctx_chunks + x_chunks
    pitch = -(-nch // 8) * 8
    if (pitch // 8) % 2 == 0:
        pitch += 8
    ctx0 = bsz * seq // n_ctx
    perm_f, perm_b = _chunk_perm(tc, gs, kd // tc)
    kern = functools.partial(_s5_kernel, ctx_chunks=ctx_chunks, x_chunks=x_chunks, pitch=pitch)
    per_set = lambda arr: pl.BlockSpec((None,) + arr.shape[1:], lambda s, b: (s,) + (0,) * (arr.ndim - 1))
    return pl.pallas_call(
        kern,
        grid=(ns, bsz),
        in_specs=[pl.BlockSpec((seq, lanes), lambda s, b: (b, s)),
                  pl.BlockSpec((n_ctx, lanes), lambda s, b: (ctx0 + b, s)),
                  pl.BlockSpec((kset, kset), lambda s, b: (0, 0)),
                  pl.BlockSpec((kset, kset), lambda s, b: (0, 0)),
                  per_set(m), per_set(w_st), per_set(w_o), per_set(a_re), per_set(a_im)],
        out_specs=[pl.BlockSpec((seq, lanes), lambda s, b: (b, s)),
                   pl.BlockSpec((n_ctx, lanes), lambda s, b: (b, s))],
        out_shape=[jax.ShapeDtypeStruct((bsz * seq, zs.shape[1]), F32),
                   jax.ShapeDtypeStruct((bsz * n_ctx, zs.shape[1]), F32)],
        scratch_shapes=[pltpu.VMEM((gs * nst // lanes * pitch, lanes), F32),
                        pltpu.VMEM((nch, kset), BF16),
                        pltpu.VMEM((nch, kset), BF16)],
        compiler_params=_cp(("parallel", "parallel")),
        name="s5_scan",
    )(zs, zs, perm_f, perm_b, m, w_st, w_o, a_re, a_im)


def _glu_kernel(yx_ref, yc_ref, w_ref, b_ref, gain_ref, o_ref, *, nx_tiles):
    y = jnp.where(pl.program_id(0) < nx_tiles, yx_ref[...], yc_ref[...])
    g = jax.nn.gelu(y)
    v = g * jax.nn.sigmoid(jnp.dot(g.astype(BF16), w_ref[...], preferred_element_type=F32) + b_ref[...])
    r = lax.rsqrt(jnp.mean(v * v, axis=-1, keepdims=True) + EPS)
    o_ref[...] = (v * r * gain_ref[...]).astype(o_ref.dtype)


def _glu(yx, yc, w, b, gain):
    ds = yx.shape[1]
    tm = 512
    nx_tiles, nc_tiles = yx.shape[0] // tm, yc.shape[0] // tm
    return pl.pallas_call(
        functools.partial(_glu_kernel, nx_tiles=nx_tiles),
        grid=(nx_tiles + nc_tiles,),
        in_specs=[pl.BlockSpec((tm, ds), lambda i: (jnp.minimum(i, nx_tiles - 1), 0)),
                  pl.BlockSpec((tm, ds), lambda i: (jnp.maximum(i - nx_tiles, 0), 0)),
                  pl.BlockSpec((ds, ds), lambda i: (0, 0)),
                  pl.BlockSpec((1, ds), lambda i: (0, 0)),
                  pl.BlockSpec((1, ds), lambda i: (0, 0))],
        out_specs=pl.BlockSpec((tm, ds), lambda i: (i, 0)),
        out_shape=jax.ShapeDtypeStruct((yx.shape[0] + yc.shape[0], ds), BF16),
        compiler_params=_cp(("parallel",)),
        name="glu",
    )(yx, yc, w, b, gain)


def _route(scores, sel, n_groups):
    epg = len(sel) // n_groups
    gscore = []
    for q in range(n_groups):
        v = sel[q * epg:(q + 1) * epg]
        best = None
        for a in range(epg):
            for b in range(a + 1, epg):
                s = v[a] + v[b]
                best = s if best is None else jnp.maximum(best, s)
        gscore.append(best)
    gbest = gscore[0]
    gidx = jnp.zeros(gbest.shape, jnp.int32)
    for q in range(1, n_groups):
        upd = gscore[q] > gbest
        gbest = jnp.where(upd, gscore[q], gbest)
        gidx = jnp.where(upd, q, gidx)
    vin = list(sel[:epg])
    sin = list(scores[:epg])
    for q in range(1, n_groups):
        pick = gidx == q
        for j in range(epg):
            vin[j] = jnp.where(pick, sel[q * epg + j], vin[j])
            sin[j] = jnp.where(pick, scores[q * epg + j], sin[j])
    b1 = vin[0]
    i1 = jnp.zeros(gbest.shape, jnp.int32)
    for j in range(1, epg):
        upd = vin[j] > b1
        b1 = jnp.where(upd, vin[j], b1)
        i1 = jnp.where(upd, j, i1)
    b2 = vin[0]
    i2 = jnp.zeros(gbest.shape, jnp.int32)
    have = jnp.zeros(gbest.shape, jnp.bool_)
    for j in range(epg):
        cand = i1 != j
        upd = cand & (jnp.logical_not(have) | (vin[j] > b2))
        b2 = jnp.where(upd, vin[j], b2)
        i2 = jnp.where(upd, j, i2)
        have = have | cand
    s1 = sin[0]
    s2 = sin[0]
    for j in range(1, epg):
        s1 = jnp.where(i1 == j, sin[j], s1)
        s2 = jnp.where(i2 == j, sin[j], s2)
    tot = s1 + s2
    return (gidx * epg + i1, gidx * epg + i2), (s1 / tot, s2 / tot)


def _merge_kernel(t_ref, yf_ref, ys_ref, wo_ref, g1_ref, n2_ref, sc2_ref, sh2_ref, rwh_ref, rwl_ref, rb_ref,
                  tn_ref, h3_ref, eidx_ref, ew_ref):
    df = yf_ref.shape[1]
    tm, d = t_ref.shape
    o = (jnp.dot(yf_ref[...], wo_ref[0:df, :], preferred_element_type=F32)
         + jnp.dot(ys_ref[...], wo_ref[df:, :], preferred_element_type=F32))
    tn = t_ref[...] + g1_ref[0] * o
    tn_ref[...] = tn
    h2 = tn * lax.rsqrt(jnp.mean(tn * tn, axis=-1, keepdims=True) + EPS) * n2_ref[...]
    h2 = h2 * (1.0 + sc2_ref[0]) + sh2_ref[0]
    nsub = d // LANES
    for c in range(nsub):
        h3_ref[pl.ds(c, tm, stride=nsub), :] = h2[:, c * LANES:(c + 1) * LANES]
    h_hi = h2.astype(BF16)
    h_lo = (h2 - h_hi.astype(F32)).astype(BF16)
    lg = (jnp.dot(h_hi, rwh_ref[...], preferred_element_type=F32)
          + jnp.dot(h_lo, rwh_ref[...], preferred_element_type=F32)
          + jnp.dot(h_hi, rwl_ref[...], preferred_element_type=F32))
    ne = rb_ref.shape[0]
    logits = lg.T[0:ne, :]
    scores = jax.nn.sigmoid(logits)
    sel = scores + rb_ref[...]
    srows = [scores[e:e + 1, :] for e in range(ne)]
    vrows = [sel[e:e + 1, :] for e in range(ne)]
    (e1, e2), (w1, w2) = _route(srows, vrows, N_EXPERT_GROUPS)
    eidx_ref[...] = jnp.zeros_like(eidx_ref)
    ew_ref[...] = jnp.zeros_like(ew_ref)
    eidx_ref[0:1, :] = e1
    eidx_ref[1:2, :] = e2
    ew_ref[0:1, :] = w1
    ew_ref[1:2, :] = w2


def _merge(t, yf, ys, wo, g1, n2, sc2, sh2, rwh, rwl, rb, n_x_rows, seq):
    nt, d = t.shape
    df = yf.shape[1]
    ne = rb.shape[0]
    nsub = d // LANES
    tm = ROW_TILE
    mrow = functools.partial(_mod_row, tm=tm, n_x_rows=n_x_rows, seq=seq)
    mspec = pl.BlockSpec((1, 1, d), lambda i: (mrow(i), 0, 0))
    return pl.pallas_call(
        _merge_kernel,
        grid=(nt // tm,),
        in_specs=[pl.BlockSpec((tm, d), lambda i: (i, 0)),
                  pl.BlockSpec((tm, df), lambda i: (i, 0)),
                  pl.BlockSpec((tm, d - df), lambda i: (i, 0)),
                  pl.BlockSpec((d, d), lambda i: (0, 0)),
                  mspec,
                  pl.BlockSpec((1, d), lambda i: (0, 0)),
                  mspec, mspec,
                  pl.BlockSpec((d, LANES), lambda i: (0, 0)),
                  pl.BlockSpec((d, LANES), lambda i: (0, 0)),
                  pl.BlockSpec((ne, 1), lambda i: (0, 0))],
        out_specs=[pl.BlockSpec((tm, d), lambda i: (i, 0)),
                   pl.BlockSpec((tm * nsub, LANES), lambda i: (i, 0)),
                   pl.BlockSpec((8, tm), lambda i: (0, i)),
                   pl.BlockSpec((8, tm), lambda i: (0, i))],
        out_shape=[jax.ShapeDtypeStruct((nt, d), F32),
                   jax.ShapeDtypeStruct((nt * nsub, LANES), F32),
                   jax.ShapeDtypeStruct((8, nt), jnp.int32),
                   jax.ShapeDtypeStruct((8, nt), F32)],
        compiler_params=_cp(("parallel",)),
        name="merge_route",
    )(t, yf, ys, wo, g1, n2, sc2, sh2, rwh, rwl, rb)


def _dispatch(eidx, ew, n_experts, rows, n_blocks):
    nt = eidx.shape[1]
    a = nt * TOP_K
    flat_e = eidx[:TOP_K].T.reshape(a)
    onehot = (flat_e[:, None] == jnp.arange(n_experts, dtype=jnp.int32)[None, :]).astype(jnp.int32)
    csum = jnp.cumsum(onehot, axis=0)
    counts = csum[-1]
    padded = (counts + rows - 1) // rows * rows
    pad_end = jnp.cumsum(padded)
    pad_start = pad_end - padded
    dest = jnp.sum(onehot * (csum - 1 + pad_start[None, :]), axis=1)
    tok = jnp.arange(a, dtype=jnp.int32) // TOP_K
    buf_tok = jnp.zeros((n_blocks * rows,), jnp.int32).at[dest].set(tok)
    n_valid = (pad_end[-1] // rows).astype(jnp.int32)
    blk_start = jnp.arange(n_blocks, dtype=jnp.int32) * rows
    blk_exp = jnp.sum((pad_end[None, :] <= blk_start[:, None]).astype(jnp.int32), axis=1)
    blk_exp = jnp.minimum(blk_exp, n_experts - 1)
    last_exp = jnp.sum(jnp.where(jnp.arange(n_blocks) == n_valid - 1, blk_exp, 0))
    blk_exp = jnp.where(jnp.arange(n_blocks) < n_valid, blk_exp, last_exp).astype(jnp.int32)
    return buf_tok, blk_exp, n_valid.reshape(1), dest.reshape(nt, TOP_K), ew[:TOP_K].T


def _start_row_copies(n_rows, src_hbm, src_row, dst, sem, nsub):
    def body(q, c):
        for k in range(ROW_COPY_UNROLL):
            r = q * ROW_COPY_UNROLL + k
            src = src_hbm.at[pl.ds(pl.multiple_of(src_row(r) * nsub, nsub), nsub)]
            pltpu.make_async_copy(src, dst.at[pl.ds(pl.multiple_of(r * GATHER_PITCH, 8), nsub)], sem).start()
        return c

    lax.fori_loop(0, n_rows // ROW_COPY_UNROLL, body, 0)


def _wait_row_copies(n_rows, src_hbm, dst, sem, nsub):
    pltpu.make_async_copy(src_hbm.at[pl.ds(0, n_rows * nsub)], dst.at[pl.ds(0, n_rows * nsub)], sem).wait()


def _expert_kernel(be_ref, nv_ref, tok_ref, tokn_ref, h_hbm, wg_hbm, wu_hbm, wd_hbm, o_ref,
                   wgb_ref, wub_ref, wdb_ref, stg_ref, std_ref, xg_ref, xb_ref, sem, wsem, *, nsub, layer):
    i = pl.program_id(0)
    rows = xb_ref.shape[0]
    valid = i < nv_ref[0]
    slot = i % 2
    e = be_ref[i]
    first = (i == 0) | (be_ref[jnp.maximum(i - 1, 0)] != e)

    def gather(toks, s):
        _start_row_copies(rows, h_hbm, lambda r: toks[0, r], xg_ref.at[s], sem.at[s], nsub)

    @pl.when(i == 0)
    def _():
        gather(tok_ref, 0)

    next_valid = (i + 1 < nv_ref[0]) & (i + 1 < pl.num_programs(0))
    for s in range(2):
        @pl.when(next_valid & (slot == 1 - s))
        def _():
            gather(tokn_ref, s)

    @pl.when(valid & first)
    def _():
        chunks = []
        for src, dst, stg in ((wg_hbm, wgb_ref, stg_ref), (wu_hbm, wub_ref, stg_ref), (wd_hbm, wdb_ref, std_ref)):
            nr = stg.shape[1]
            chunks += [(src, dst, stg, k * nr, nr) for k in range(dst.shape[0] // nr)]

        def copy(k):
            src, _, stg, r0, nr = chunks[k]
            return pltpu.make_async_copy(src.at[layer, e, pl.ds(r0, nr), :], stg.at[k % 2], wsem.at[k % 2])

        copy(0).start()
        for k, (_, dst, stg, r0, nr) in enumerate(chunks):
            if k + 1 < len(chunks):
                copy(k + 1).start()
            copy(k).wait()
            dst[r0:r0 + nr, :] = stg[k % 2].astype(BF16)

    @pl.when(valid)
    def _():
        for s in range(2):
            @pl.when(slot == s)
            def _():
                _wait_row_copies(rows, h_hbm, xg_ref.at[s], sem.at[s], nsub)
                for c in range(nsub):
                    xb_ref[:, c * LANES:(c + 1) * LANES] = (
                        xg_ref[s, pl.ds(c, rows, stride=GATHER_PITCH), :].astype(BF16))
        x = xb_ref[...]
        g = jnp.dot(x, wgb_ref[...], preferred_element_type=F32)
        u = jnp.dot(x, wub_ref[...], preferred_element_type=F32)
        hmid = (g * jax.nn.sigmoid(g)) * u
        y = jnp.dot(hmid.astype(BF16), wdb_ref[...], preferred_element_type=F32)
        for c in range(nsub):
            o_ref[pl.ds(c, rows, stride=nsub), :] = y[:, c * LANES:(c + 1) * LANES]

    @pl.when(jnp.logical_not(valid))
    def _():
        o_ref[...] = jnp.zeros_like(o_ref)


def _experts(h3, buf_tok, blk_exp, n_valid, w_gate, w_up, w_down, layer):
    _, ne, d, de = w_gate.shape
    nsub = d // LANES
    rows = MOE_ROWS
    n_blocks = buf_tok.shape[0] // rows
    any_spec = pl.BlockSpec(memory_space=pl.ANY)
    grid_spec = pltpu.PrefetchScalarGridSpec(
        num_scalar_prefetch=2,
        grid=(n_blocks,),
        in_specs=[pl.BlockSpec((None, 1, rows), lambda i, be, nv: (i, 0, 0), memory_space=pltpu.SMEM),
                  pl.BlockSpec((None, 1, rows), lambda i, be, nv: (jnp.minimum(i + 1, n_blocks - 1), 0, 0),
                               memory_space=pltpu.SMEM),
                  any_spec, any_spec, any_spec, any_spec],
        out_specs=pl.BlockSpec((rows * nsub, LANES), lambda i, be, nv: (i, 0)),
        scratch_shapes=[pltpu.VMEM((d, de), BF16),
                        pltpu.VMEM((d, de), BF16),
                        pltpu.VMEM((de, d), BF16),
                        pltpu.VMEM((2, d // MOE_WCHUNKS, de), F32),
                        pltpu.VMEM((2, de // MOE_WCHUNKS, d), F32),
                        pltpu.VMEM((2, rows * GATHER_PITCH, LANES), F32),
                        pltpu.VMEM((rows, d), BF16),
                        pltpu.SemaphoreType.DMA((2,)),
                        pltpu.SemaphoreType.DMA((2,))],
    )
    toks = buf_tok.reshape(n_blocks, 1, rows)
    return pl.pallas_call(
        functools.partial(_expert_kernel, nsub=nsub, layer=layer),
        grid_spec=grid_spec,
        out_shape=jax.ShapeDtypeStruct((n_blocks * rows * nsub, LANES), F32),
        compiler_params=_cp(("arbitrary",)),
        name="experts",
    )(blk_exp, n_valid, toks, toks, h3, w_gate, w_up, w_down)


def _combine_kernel(pos_ref, posn_ref, t_ref, w_ref, yb_hbm, g2_ref, fg_ref, o_ref, gk_ref, sem, *, nsub,
                    final_norm):
    i = pl.program_id(0)
    tm = t_ref.shape[0]
    slot = i % 2

    def gather(pos, s):
        for k in range(TOP_K):
            _start_row_copies(tm, yb_hbm, lambda r, k=k: pos[0, TOP_K * r + k], gk_ref.at[s, k], sem.at[s], nsub)

    @pl.when(i == 0)
    def _():
        gather(pos_ref, 0)

    for s in range(2):
        @pl.when((i + 1 < pl.num_programs(0)) & (slot == 1 - s))
        def _():
            gather(posn_ref, s)

    w0 = w_ref[:, 0:1]
    w1 = w_ref[:, 1:2]
    for s in range(2):
        @pl.when(slot == s)
        def _():
            for k in range(TOP_K):
                _wait_row_copies(tm, yb_hbm, gk_ref.at[s, k], sem.at[s], nsub)
            for c in range(nsub):
                sl = slice(c * LANES, (c + 1) * LANES)
                y = (gk_ref[s, 0, pl.ds(c, tm, stride=GATHER_PITCH), :] * w0
                     + gk_ref[s, 1, pl.ds(c, tm, stride=GATHER_PITCH), :] * w1)
                o_ref[:, sl] = t_ref[:, sl] + g2_ref[0][:, sl] * y

    if final_norm:
        x = o_ref[...]
        o_ref[...] = x * lax.rsqrt(jnp.mean(x * x, axis=-1, keepdims=True) + EPS) * fg_ref[...]


def _combine(t, yb3, pos, w, g2, final_g, n_x_rows, seq, n_out_rows, final_norm):
    nt, d = t.shape
    nsub = d // LANES
    tm = ROW_TILE
    mrow = functools.partial(_mod_row, tm=tm, n_x_rows=n_x_rows, seq=seq)
    nb = n_out_rows // tm
    posb = pos.reshape(nt // tm, 1, TOP_K * tm)
    return pl.pallas_call(
        functools.partial(_combine_kernel, nsub=nsub, final_norm=final_norm),
        grid=(nb,),
        in_specs=[pl.BlockSpec((None, 1, TOP_K * tm), lambda i: (i, 0, 0), memory_space=pltpu.SMEM),
                  pl.BlockSpec((None, 1, TOP_K * tm), lambda i: (jnp.minimum(i + 1, nb - 1), 0, 0),
                               memory_space=pltpu.SMEM),
                  pl.BlockSpec((tm, d), lambda i: (i, 0)),
                  pl.BlockSpec((tm, TOP_K), lambda i: (i, 0)),
                  pl.BlockSpec(memory_space=pl.ANY),
                  pl.BlockSpec((1, 1, d), lambda i: (mrow(i), 0, 0)),
                  pl.BlockSpec((1, d), lambda i: (0, 0))],
        out_specs=pl.BlockSpec((tm, d), lambda i: (i, 0)),
        out_shape=jax.ShapeDtypeStruct((n_out_rows, d), F32),
        scratch_shapes=[pltpu.VMEM((2, TOP_K, tm * GATHER_PITCH, LANES), F32),
                        pltpu.SemaphoreType.DMA((2,))],
        compiler_params=_cp(("arbitrary",)),
        name="combine",
    )(posb, posb, t, w, yb3, g2, final_g)


def kernel(x, c, ctx, c_ctx, w_mod, b_mod, norm1_g, norm2_g, w_in, w_out, fourier_w, mix_norm_g,
           lam_re, lam_im, log_dt, b_re, b_im, c_re, c_im, d_skip, glu_w, glu_b,
           router_w, router_b, w_gate, w_up, w_down, final_g):
    bsz, seq, d = x.shape
    n_ctx = ctx.shape[1]
    depth = w_mod.shape[0]
    df = fourier_w.shape[1] * fourier_w.shape[2]
    ds = d_skip.shape[1]
    ne = router_w.shape[1]
    nx_rows = bsz * seq
    nt = nx_rows + bsz * n_ctx

    t = jnp.concatenate([x.reshape(nx_rows, d), ctx.reshape(bsz * n_ctx, d)], axis=0).astype(F32)

    cvec = jnp.concatenate([c_ctx[None, :], c, jnp.zeros((8 - 1 - bsz, d), c.dtype)], axis=0).astype(F32)
    mod = _adaln(cvec, w_mod, b_mod).reshape(depth, 8, N_MOD, 1, d)

    cc, sc_ = _dft_tables(df)
    w_chan = jnp.concatenate([cc, sc_], axis=1).astype(BF16)
    tabs_x = _half_tables(seq)
    tabs_c = _half_tables(n_ctx)
    s5_tabs = jax.vmap(_s5_tables)(lam_re, lam_im, log_dt, b_re, b_im, c_re, c_im, d_skip)

    rw = jnp.pad(router_w.astype(F32), ((0, 0), (0, LANES - ne)))
    rwh = rw.astype(BF16)
    rwl = (rw - rwh.astype(F32)).astype(BF16)
    rb = router_b.astype(F32).reshape(ne, 1)
    fg = final_g.reshape(1, d).astype(F32)
    n_blocks = -(-(nt * TOP_K) // MOE_ROWS) + ne

    for l in range(depth):
        sh1, sc1, g1, sh2, sc2, g2 = [mod[l, :, k] for k in range(N_MOD)]
        zf, zs = _inproj(t, norm1_g[l].reshape(1, d).astype(F32), sc1, sh1, w_in[l].astype(BF16),
                         df, nx_rows, seq)

        hw = fourier_w[l].astype(BF16)
        gain = mix_norm_g[l].astype(F32).reshape(1, -1)
        yf = _fourier_mix(zf, w_chan, tabs_x, hw, gain[:, :df], 0, bsz, seq, jnp.zeros((nt, df), BF16))
        yf = _fourier_mix(zf, w_chan, tabs_c, hw, gain[:, :df], nx_rows, bsz, n_ctx, yf)

        tables = [tab[l] for tab in s5_tabs]
        ysx, ysc = _s5_scan(zs, tables, bsz, seq, n_ctx)
        ys = _glu(ysx, ysc, glu_w[l].astype(BF16), glu_b[l].astype(F32).reshape(1, ds), gain[:, df:])

        t, h3, eidx, ew = _merge(t, yf, ys, w_out[l].astype(BF16), g1,
                                 norm2_g[l].reshape(1, d).astype(F32), sc2, sh2, rwh, rwl, rb, nx_rows, seq)

        buf_tok, blk_exp, n_valid, pos, wtok = _dispatch(eidx, ew, ne, MOE_ROWS, n_blocks)
        yb = _experts(h3, buf_tok, blk_exp, n_valid, w_gate, w_up, w_down, l)
        last = l == depth - 1
        t = _combine(t, yb, pos, wtok, g2, fg, nx_rows, seq, nx_rows if last else nt, last)

    return t.reshape(bsz, seq, d).astype(x.dtype)
```

```python
import functools
import math

import jax
import jax.numpy as jnp
from jax import lax
from jax.experimental import pallas as pl
from jax.experimental.pallas import tpu as pltpu

F32 = jnp.float32
BF16 = jnp.bfloat16
EPS = 1e-6

FOURIER_HEADS = 4
SSM_GROUP = 16
N_EXPERT_GROUPS = 4
TOP_K = 2
N_MOD = 6

SSM_CHUNK = 16
ROW_TILE = 256
MOE_ROWS = 256
MOE_WCHUNKS = 8
ROW_COPY_UNROLL = 8
LANES = 128
GATHER_PITCH = 20
S5_SET = LANES
VMEM_LIMIT_BYTES = 56 * 1024 * 1024


def _cp(sems):
    return pltpu.CompilerParams(dimension_semantics=sems, vmem_limit_bytes=VMEM_LIMIT_BYTES)


def _mod_row(i, tm, n_x_rows, seq):
    r = i * tm
    return jnp.where(r < n_x_rows, 1 + r // seq, 0)


def _mod_kernel(c_ref, w_ref, b_ref, o_ref):
    c = c_ref[...]
    s = c * jax.nn.sigmoid(c)
    o_ref[...] = jnp.dot(s.astype(BF16), w_ref[...].astype(BF16),
                         preferred_element_type=F32) + b_ref[...]


def _adaln(cvec, w_mod, b_mod):
    depth, d, n = w_mod.shape
    tn = 1024
    return pl.pallas_call(
        _mod_kernel,
        grid=(depth, n // tn),
        in_specs=[pl.BlockSpec((8, d), lambda l, j: (0, 0)),
                  pl.BlockSpec((None, d, tn), lambda l, j: (l, 0, j)),
                  pl.BlockSpec((None, 1, tn), lambda l, j: (l, 0, j))],
        out_specs=pl.BlockSpec((None, 8, tn), lambda l, j: (l, 0, j)),
        out_shape=jax.ShapeDtypeStruct((depth, 8, n), F32),
        compiler_params=_cp(("parallel", "parallel")),
        name="adaln",
    )(cvec, w_mod, b_mod.reshape(depth, 1, n))


def _inproj_kernel(t_ref, g_ref, sc_ref, sh_ref, w_ref, zf_ref, zs_ref):
    x = t_ref[...]
    h = x * lax.rsqrt(jnp.mean(x * x, axis=-1, keepdims=True) + EPS) * g_ref[...]
    h = h * (1.0 + sc_ref[0]) + sh_ref[0]
    z = jnp.dot(h.astype(BF16), w_ref[...], preferred_element_type=F32)
    df = zf_ref.shape[1]
    zf_ref[...] = z[:, :df].astype(zf_ref.dtype)
    zs_ref[...] = z[:, df:]


def _inproj(t, g, sc, sh, w, df, n_x_rows, seq):
    nt, d = t.shape
    n = w.shape[1]
    tm = ROW_TILE
    mrow = functools.partial(_mod_row, tm=tm, n_x_rows=n_x_rows, seq=seq)
    return pl.pallas_call(
        _inproj_kernel,
        grid=(nt // tm,),
        in_specs=[pl.BlockSpec((tm, d), lambda i: (i, 0)),
                  pl.BlockSpec((1, d), lambda i: (0, 0)),
                  pl.BlockSpec((1, 1, d), lambda i: (mrow(i), 0, 0)),
                  pl.BlockSpec((1, 1, d), lambda i: (mrow(i), 0, 0)),
                  pl.BlockSpec((d, n), lambda i: (0, 0))],
        out_specs=[pl.BlockSpec((tm, df), lambda i: (i, 0)),
                   pl.BlockSpec((tm, n - df), lambda i: (i, 0))],
        out_shape=[jax.ShapeDtypeStruct((nt, df), BF16),
                   jax.ShapeDtypeStruct((nt, n - df), F32)],
        compiler_params=_cp(("parallel",)),
        name="inproj",
    )(t, g, sc, sh, w)


def _dft_tables(n):
    r = 1
    while r * r < n:
        r *= 2
    q = n // r
    k = jnp.arange(n, dtype=jnp.int32)[:, None]
    step = 2.0 * math.pi / n
    pa = ((k * (jnp.arange(q, dtype=jnp.int32)[None, :] * r)) % n).astype(F32) * step
    pb = ((k * jnp.arange(r, dtype=jnp.int32)[None, :]) % n).astype(F32) * step
    ca, sa = jnp.cos(pa)[:, :, None], jnp.sin(pa)[:, :, None]
    cb, sb = jnp.cos(pb)[:, None, :], jnp.sin(pb)[:, None, :]
    c = (ca * cb - sa * sb).reshape(n, n)
    s = (sa * cb + ca * sb).reshape(n, n)
    return c, s


def _chan_dft_kernel(z_ref, w_ref, o_ref):
    o_ref[...] = jnp.dot(z_ref[...], w_ref[...], preferred_element_type=F32).astype(o_ref.dtype)


def _chan_dft(z, w):
    nt = z.shape[0]
    df, n = w.shape
    tm = min(512, nt)
    return pl.pallas_call(
        _chan_dft_kernel,
        grid=(nt // tm,),
        in_specs=[pl.BlockSpec((tm, df), lambda i: (i, 0)),
                  pl.BlockSpec((df, n), lambda i: (0, 0))],
        out_specs=pl.BlockSpec((tm, n), lambda i: (i, 0)),
        out_shape=jax.ShapeDtypeStruct((nt, n), BF16),
        compiler_params=_cp(("parallel",)),
        name="chan_dft",
    )(z, w)


def _chan_sym_kernel(z_ref, zr_ref, w_ref, o_ref):
    z = z_ref[...].astype(F32)
    zr = zr_ref[...].astype(F32)
    c = w_ref.shape[0]
    o_ref[:, :c] = jnp.dot((z + zr).astype(BF16), w_ref[:, :c], preferred_element_type=F32).astype(o_ref.dtype)
    o_ref[:, c:] = jnp.dot((z - zr).astype(BF16), w_ref[:, c:], preferred_element_type=F32).astype(o_ref.dtype)


def _chan_sym(z, zr, w, row0, bsz, length):
    df, n = w.shape
    half = length // 2
    tm = min(512, half)
    hb = half // tm
    off = row0 // tm
    return pl.pallas_call(
        _chan_sym_kernel,
        grid=(bsz, hb),
        in_specs=[pl.BlockSpec((tm, df), lambda b, i: (off + b * 2 * hb + i, 0)),
                  pl.BlockSpec((tm, df), lambda b, i: (b * hb + i, 0)),
                  pl.BlockSpec((df, n), lambda b, i: (0, 0))],
        out_specs=pl.BlockSpec((tm, n), lambda b, i: (b * hb + i, 0)),
        out_shape=jax.ShapeDtypeStruct((bsz * half, n), BF16),
        compiler_params=_cp(("parallel", "parallel")),
        name="chan_sym",
    )(z, zr, w)


def _fourier_epilogue(f, hw_ref, gain_ref, o_ref):
    nh, hd, _ = hw_ref.shape
    ys = [jnp.dot(f[:, h * hd:(h + 1) * hd].astype(BF16), hw_ref[h],
                  preferred_element_type=F32) for h in range(nh)]
    ssum = jnp.sum(ys[0] * ys[0], axis=-1, keepdims=True)
    for h in range(1, nh):
        ssum = ssum + jnp.sum(ys[h] * ys[h], axis=-1, keepdims=True)
    r = lax.rsqrt(ssum / (nh * hd) + EPS)
    for h in range(nh):
        o_ref[:, h * hd:(h + 1) * hd] = (ys[h] * r * gain_ref[:, h * hd:(h + 1) * hd]).astype(o_ref.dtype)


def _row_sym_kernel(ch_ref, sh_ref, alt_ref, ae_ref, bo_ref, amid_ref, hw_ref, gain_ref, lo_ref, hi_ref,
                    p_ref, q_ref, r_ref, *, scale, mid_sign):
    m = pl.program_id(1)
    k = pl.program_id(2)

    @pl.when(k == 0)
    def _():
        p_ref[...] = jnp.zeros_like(p_ref)
        q_ref[...] = jnp.zeros_like(q_ref)
        r_ref[...] = jnp.zeros_like(r_ref)

    ae = ae_ref[...]
    p_ref[...] += jnp.dot(ch_ref[...], ae, preferred_element_type=F32)
    q_ref[...] += jnp.dot(sh_ref[...], bo_ref[...], preferred_element_type=F32)

    @pl.when(m == 0)
    def _():
        r_ref[...] += jnp.dot(alt_ref[...], ae, preferred_element_type=F32)

    @pl.when(k == pl.num_programs(2) - 1)
    def _():
        tm = p_ref.shape[0]
        row = lax.broadcasted_iota(jnp.int32, (tm, 1), 0) + m * tm
        amid = amid_ref[0:1, :]
        pf = p_ref[...] + jnp.where((row & 1) == 0, 1.0, -1.0) * amid
        q = q_ref[...]
        mid = r_ref[0:1, :] + mid_sign * amid
        hi = jnp.where(row == 0, mid, pf + q)
        _fourier_epilogue((pf - q) * scale, hw_ref, gain_ref, lo_ref)
        _fourier_epilogue(hi * scale, hw_ref, gain_ref, hi_ref)


def _without_alias_ref(kernel, pos):
    def wrapped(*refs):
        return kernel(*refs[:pos], *refs[pos + 1:])
    return wrapped


def _row_sym(aebo, amid, ch, sh, alt, hw, gain, bsz, length, row0, ybuf, n_rows):
    df = aebo.shape[1] // 2
    half = length // 2
    tm = tk = min(512, half)
    mb, kb = half // tm, half // tk
    scale = 1.0 / math.sqrt(length * df)
    mid_sign = 1.0 if half % 2 == 0 else -1.0
    off = row0 // tm
    kern = functools.partial(_row_sym_kernel, scale=scale, mid_sign=mid_sign)
    in_specs = [pl.BlockSpec((tm, tk), lambda b, m, k: (m, k)),
                pl.BlockSpec((tm, tk), lambda b, m, k: (m, k)),
                pl.BlockSpec((8, tk), lambda b, m, k: (0, k)),
                pl.BlockSpec((tk, df), lambda b, m, k: (b * kb + k, 0)),
                pl.BlockSpec((tk, df), lambda b, m, k: (b * kb + k, 1)),
                pl.BlockSpec((8, df), lambda b, m, k: (b, 0)),
                pl.BlockSpec(hw.shape, lambda b, m, k: (0, 0, 0)),
                pl.BlockSpec((1, df), lambda b, m, k: (0, 0))]
    args = [ch, sh, alt, aebo, aebo, amid, hw, gain]
    aliases = {}
    if ybuf is not None:
        kern = _without_alias_ref(kern, len(args))
        in_specs.append(pl.BlockSpec(memory_space=pl.ANY))
        aliases = {len(args): 0}
        args.append(ybuf)
    return pl.pallas_call(
        kern,
        grid=(bsz, mb, kb),
        in_specs=in_specs,
        out_specs=[pl.BlockSpec((tm, df), lambda b, m, k: (off + b * 2 * mb + m, 0)),
                   pl.BlockSpec((tm, df), lambda b, m, k: (b * mb + m, 0))],
        out_shape=[jax.ShapeDtypeStruct((n_rows, df), BF16),
                   jax.ShapeDtypeStruct((bsz * half, df), BF16)],
        scratch_shapes=[pltpu.VMEM((tm, df), F32), pltpu.VMEM((tm, df), F32), pltpu.VMEM((8, df), F32)],
        input_output_aliases=aliases,
        compiler_params=_cp(("parallel", "parallel", "arbitrary")),
        name="row_sym",
    )(*args)


def _mirror_kernel(j_ref, a_ref, b_ref, o_ref, *, keep_first):
    rev = jnp.dot(j_ref[...], a_ref[...], preferred_element_type=F32)
    first = b_ref[0:1, :].astype(F32)
    if not keep_first:
        first = jnp.where(pl.program_id(1) == 0, 0.0, first)
    row = lax.broadcasted_iota(jnp.int32, (o_ref.shape[0], 1), 0)
    o_ref[...] = jnp.where(row == 0, first, rev).astype(o_ref.dtype)


def _mirror(src, src_row0, src_stride, half, bsz, keep_first, dst_row0=0, dst_stride=None, ybuf=None):
    df = src.shape[1]
    tm = min(512, half)
    nb = half // tm
    dst_stride = half if dst_stride is None else dst_stride
    soff, sstr, doff, dstr = src_row0 // tm, src_stride // tm, dst_row0 // tm, dst_stride // tm
    r = jnp.arange(tm, dtype=jnp.int32)
    jmat = ((r[:, None] + r[None, :] == tm) & (r[:, None] > 0)).astype(BF16)
    kern = functools.partial(_mirror_kernel, keep_first=keep_first)
    in_specs = [pl.BlockSpec((tm, tm), lambda b, i: (0, 0)),
                pl.BlockSpec((tm, df), lambda b, i: (soff + b * sstr + nb - 1 - i, 0)),
                pl.BlockSpec((tm, df), lambda b, i: (soff + b * sstr + (nb - i) % nb, 0))]
    args = [jmat, src, src]
    aliases = {}
    n_rows = bsz * half
    if ybuf is not None:
        kern = _without_alias_ref(kern, len(args))
        in_specs.append(pl.BlockSpec(memory_space=pl.ANY))
        aliases = {len(args): 0}
        args.append(ybuf)
        n_rows = ybuf.shape[0]
    return pl.pallas_call(
        kern,
        grid=(bsz, nb),
        in_specs=in_specs,
        out_specs=pl.BlockSpec((tm, df), lambda b, i: (doff + b * dstr + i, 0)),
        out_shape=jax.ShapeDtypeStruct((n_rows, df), BF16),
        input_output_aliases=aliases,
        compiler_params=_cp(("parallel", "parallel")),
        name="mirror",
    )(*args)


def _fourier_mix(zf, w_chan, tabs, hw, gain, row0, bsz, length, ybuf):
    ch, sh, alt = tabs
    df = zf.shape[1]
    half = length // 2
    zr = _mirror(zf, row0 + half, length, half, bsz, keep_first=False)
    aebo = _chan_sym(zf, zr, w_chan, row0, bsz, length)
    zmid = zf[row0:row0 + bsz * length].reshape(bsz, length, df)[:, half]
    zmid = jnp.pad(zmid[:, None, :], ((0, 0), (0, 7), (0, 0)))
    amid = _chan_dft(zmid.reshape(bsz * 8, df), w_chan)[:, :df].astype(F32)
    y, hi = _row_sym(aebo, amid, ch, sh, alt, hw, gain, bsz, length, row0, ybuf, zf.shape[0])
    return _mirror(hi, 0, half, half, bsz, keep_first=True, dst_row0=row0 + half, dst_stride=length, ybuf=y)


def _half_tables(length):
    half = length // 2
    c, s = _dft_tables(length)
    n = jnp.arange(half, dtype=jnp.int32)
    alt = jnp.where((lax.broadcasted_iota(jnp.int32, (8, half), 0) == 0),
                    jnp.where(n % 2 == 0, 1.0, -1.0)[None, :], 0.0)
    return c[:half, :half].astype(BF16), s[:half, :half].astype(BF16), alt.astype(BF16)


def _s5_tables(lam_re, lam_im, log_dt, b_re, b_im, c_re, c_im, d_skip):
    tc = SSM_CHUNK
    g, p = lam_re.shape[1:]
    hch = b_re.shape[-1]
    lr, li = lam_re.astype(F32), lam_im.astype(F32)
    dt = jnp.exp(log_dt.astype(F32))[..., None]
    er, ei = lr * dt, li * dt

    def lpow(k):
        m = jnp.exp(er * k)
        return m * jnp.cos(ei * k), m * jnp.sin(ei * k)

    l1r, l1i = lpow(1.0)
    den = lr * lr + li * li
    qr = ((l1r - 1.0) * lr + l1i * li) / den
    qi = (l1i * lr - (l1r - 1.0) * li) / den
    br, bi = b_re.astype(F32), b_im.astype(F32)
    bbr = qr[..., None] * br - qi[..., None] * bi
    bbi = qr[..., None] * bi + qi[..., None] * br
    cr, ci = c_re.astype(F32), c_im.astype(F32)

    ks = jnp.arange(tc + 1, dtype=F32)
    pwr = jnp.stack([lpow(k)[0] for k in range(tc + 1)], axis=-1)
    pwi = jnp.stack([lpow(k)[1] for k in range(tc + 1)], axis=-1)
    del ks

    cbr = jnp.einsum("dgip,dgpj->dgpij", cr, bbr) - jnp.einsum("dgip,dgpj->dgpij", ci, bbi)
    cbi = jnp.einsum("dgip,dgpj->dgpij", cr, bbi) + jnp.einsum("dgip,dgpj->dgpij", ci, bbr)
    klag = (jnp.einsum("dgpij,dgpl->dglij", cbr, pwr[..., :tc])
            - jnp.einsum("dgpij,dgpl->dglij", cbi, pwi[..., :tc]))

    t_in = jnp.arange(tc)[:, None]
    t_out = jnp.arange(tc)[None, :]
    lag_f = t_out - t_in
    lag_b = t_in - t_out
    kf = klag[0][:, jnp.clip(lag_f, 0, tc - 1)] * (lag_f >= 0)[None, :, :, None, None]
    kb = klag[1][:, jnp.clip(lag_b, 0, tc - 1)] * (lag_b >= 0)[None, :, :, None, None]
    eye_t = (t_in == t_out).astype(F32)[None, :, :, None, None]
    dg = d_skip.astype(F32).reshape(g, hch)
    dmat = eye_t * (jnp.eye(hch, dtype=F32) * dg[:, None, :])[:, None, None, :, :]
    m = (kf + kb + dmat).transpose(0, 1, 4, 2, 3).reshape(g, tc * hch, tc * hch)

    idx_f = jnp.arange(tc - 1, -1, -1)
    idx_b = jnp.arange(tc)

    def st(d, idx):
        wr = pwr[d][..., idx][:, :, :, None] * bbr[d][:, :, None, :] - pwi[d][..., idx][:, :, :, None] * bbi[d][:, :, None, :]
        wi = pwr[d][..., idx][:, :, :, None] * bbi[d][:, :, None, :] + pwi[d][..., idx][:, :, :, None] * bbr[d][:, :, None, :]
        return (wr.transpose(0, 2, 3, 1).reshape(g, tc * hch, p),
                wi.transpose(0, 2, 3, 1).reshape(g, tc * hch, p))

    sfr, sfi = st(0, idx_f)
    sbr, sbi = st(1, idx_b)
    w_st = jnp.concatenate([sfr, sbr, sfi, sbi], axis=-1)

    def so(d, idx):
        wr = cr[d][:, None, :, :] * pwr[d][..., idx].transpose(0, 2, 1)[:, :, None, :] \
            - ci[d][:, None, :, :] * pwi[d][..., idx].transpose(0, 2, 1)[:, :, None, :]
        wi = cr[d][:, None, :, :] * pwi[d][..., idx].transpose(0, 2, 1)[:, :, None, :] \
            + ci[d][:, None, :, :] * pwr[d][..., idx].transpose(0, 2, 1)[:, :, None, :]
        return (wr.reshape(g, tc * hch, p).transpose(0, 2, 1),
                wi.reshape(g, tc * hch, p).transpose(0, 2, 1))

    ofr, ofi = so(0, jnp.arange(1, tc + 1))
    obr, obi = so(1, jnp.arange(tc, 0, -1))
    zero = jnp.zeros_like(ofr)
    w_of = jnp.concatenate([ofr, zero, -ofi, zero], axis=1)
    w_ob = jnp.concatenate([zero, obr, zero, -obi], axis=1)

    a_re = jnp.concatenate([pwr[0][..., tc], pwr[1][..., tc]], axis=-1)
    a_im = jnp.concatenate([pwi[0][..., tc], pwi[1][..., tc]], axis=-1)

    gs = S5_SET // hch
    ns = g // gs
    kd = tc * hch
    w_o = w_of + w_ob
    return (m.astype(BF16).reshape(ns, gs, kd, kd),
            w_st.astype(BF16).reshape(ns, gs, kd, 4 * p),
            w_o.astype(BF16).reshape(ns, gs, 4 * p, kd),
            a_re.reshape(ns, gs, 2 * p), a_im.reshape(ns, gs, 2 * p))


def _chunk_perm(tc, gs, hch):
    n = tc * gs * hch
    r = jnp.arange(n, dtype=jnp.int32)
    t, g, j = r // (gs * hch), (r // hch) % gs, r % hch
    dst = g * (tc * hch) + t * hch + j
    fwd = (dst[:, None] == r[None, :]).astype(BF16)
    return fwd, fwd.T


def _s5_kernel(zx_ref, zc_ref, pf_ref, pb_ref, m_ref, wst_ref, wo_ref, are_ref, aim_ref, ox_ref, oc_ref,
               sh_ref, xb_ref, yb_ref, *, ctx_chunks, x_chunks, pitch):
    tc = SSM_CHUNK
    nch = ctx_chunks + x_chunks
    lanes = zx_ref.shape[1]
    gs, kd, nst = wst_ref.shape
    for t in range(tc):
        xb_ref[0:ctx_chunks, t * lanes:(t + 1) * lanes] = zc_ref[pl.ds(t, ctx_chunks, stride=tc), :].astype(BF16)
        xb_ref[ctx_chunks:nch, t * lanes:(t + 1) * lanes] = zx_ref[pl.ds(t, x_chunks, stride=tc), :].astype(BF16)
    xb_ref[...] = jnp.dot(xb_ref[...], pf_ref[...], preferred_element_type=F32).astype(BF16)

    nslab = gs * nst // lanes
    half = nslab // 2
    for j in range(gs):
        s = jnp.dot(xb_ref[:, j * kd:(j + 1) * kd], wst_ref[j], preferred_element_type=F32)
        sh_ref[j * pitch:j * pitch + nch, :] = s[:, 0:lanes]
        sh_ref[(half + j) * pitch:(half + j) * pitch + nch, :] = s[:, lanes:2 * lanes]

    fwd = lax.broadcasted_iota(jnp.int32, (half, lanes), 1) < (lanes // 2)
    a_re = are_ref[...]
    a_im = aim_ref[...]

    def rows(c, part):
        return pl.ds(part * half * pitch + c, half, stride=pitch)

    def step(i, carry):
        hr, hi = carry
        cf = i
        cb = jnp.where(i < ctx_chunks, ctx_chunks - 1 - i, nch - 1 + ctx_chunks - i)
        same = cf == cb
        sfr, sfi = sh_ref[rows(cf, 0), :], sh_ref[rows(cf, 1), :]
        sbr, sbi = sh_ref[rows(cb, 0), :], sh_ref[rows(cb, 1), :]
        sh_ref[rows(cf, 0), :] = jnp.where(fwd, hr, sfr)
        sh_ref[rows(cf, 1), :] = jnp.where(fwd, hi, sfi)
        sh_ref[rows(cb, 0), :] = jnp.where(fwd, jnp.where(same, hr, sbr), hr)
        sh_ref[rows(cb, 1), :] = jnp.where(fwd, jnp.where(same, hi, sbi), hi)
        sr = jnp.where(fwd, sfr, sbr)
        si = jnp.where(fwd, sfi, sbi)
        return a_re * hr - a_im * hi + sr, a_re * hi + a_im * hr + si

    zero = jnp.zeros((half, lanes), F32)
    lax.fori_loop(0, nch, step, (zero, zero))

    for j in range(gs):
        hb = jnp.concatenate([sh_ref[j * pitch:j * pitch + nch, :],
                              sh_ref[(half + j) * pitch:(half + j) * pitch + nch, :]], axis=1).astype(BF16)
        yj = (jnp.dot(xb_ref[:, j * kd:(j + 1) * kd], m_ref[j], preferred_element_type=F32)
              + jnp.dot(hb, wo_ref[j], preferred_element_type=F32))
        yb_ref[:, j * kd:(j + 1) * kd] = yj.astype(BF16)
    y = jnp.dot(yb_ref[...], pb_ref[...], preferred_element_type=F32)
    for t in range(tc):
        oc_ref[pl.ds(t, ctx_chunks, stride=tc), :] = y[0:ctx_chunks, t * lanes:(t + 1) * lanes]
        ox_ref[pl.ds(t, x_chunks, stride=tc), :] = y[ctx_chunks:nch, t * lanes:(t + 1) * lanes]


def _s5_scan(zs, tables, bsz, seq, n_ctx):
    m, w_st, w_o, a_re, a_im = tables
    ns, gs, kd, nst = w_st.shape
    tc = SSM_CHUNK
    lanes = S5_SET
    kset = gs * kd
    ctx_chunks, x_chunks = n_ctx // tc, seq // tc
    nch = ctx_chunks + x_chunks
    pitch = -(-nch // 8) * 8
    if (pitch // 8) % 2 == 0:
        pitch += 8
    ctx0 = bsz * seq // n_ctx
    perm_f, perm_b = _chunk_perm(tc, gs, kd // tc)
    kern = functools.partial(_s5_kernel, ctx_chunks=ctx_chunks, x_chunks=x_chunks, pitch=pitch)
    per_set = lambda arr: pl.BlockSpec((None,) + arr.shape[1:], lambda s, b: (s,) + (0,) * (arr.ndim - 1))
    return pl.pallas_call(
        kern,
        grid=(ns, bsz),
        in_specs=[pl.BlockSpec((seq, lanes), lambda s, b: (b, s)),
                  pl.BlockSpec((n_ctx, lanes), lambda s, b: (ctx0 + b, s)),
                  pl.BlockSpec((kset, kset), lambda s, b: (0, 0)),
                  pl.BlockSpec((kset, kset), lambda s, b: (0, 0)),
                  per_set(m), per_set(w_st), per_set(w_o), per_set(a_re), per_set(a_im)],
        out_specs=[pl.BlockSpec((seq, lanes), lambda s, b: (b, s)),
                   pl.BlockSpec((n_ctx, lanes), lambda s, b: (b, s))],
        out_shape=[jax.ShapeDtypeStruct((bsz * seq, zs.shape[1]), F32),
                   jax.ShapeDtypeStruct((bsz * n_ctx, zs.shape[1]), F32)],
        scratch_shapes=[pltpu.VMEM((gs * nst // lanes * pitch, lanes), F32),
                        pltpu.VMEM((nch, kset), BF16),
                        pltpu.VMEM((nch, kset), BF16)],
        compiler_params=_cp(("parallel", "parallel")),
        name="s5_scan",
    )(zs, zs, perm_f, perm_b, m, w_st, w_o, a_re, a_im)


def _glu_kernel(yx_ref, yc_ref, w_ref, b_ref, gain_ref, o_ref, *, nx_tiles):
    y = jnp.where(pl.program_id(0) < nx_tiles, yx_ref[...], yc_ref[...])
    g = jax.nn.gelu(y)
    v = g * jax.nn.sigmoid(jnp.dot(g.astype(BF16), w_ref[...], preferred_element_type=F32) + b_ref[...])
    r = lax.rsqrt(jnp.mean(v * v, axis=-1, keepdims=True) + EPS)
    o_ref[...] = (v * r * gain_ref[...]).astype(o_ref.dtype)


def _glu(yx, yc, w, b, gain):
    ds = yx.shape[1]
    tm = 512
    nx_tiles, nc_tiles = yx.shape[0] // tm, yc.shape[0] // tm
    return pl.pallas_call(
        functools.partial(_glu_kernel, nx_tiles=nx_tiles),
        grid=(nx_tiles + nc_tiles,),
        in_specs=[pl.BlockSpec((tm, ds), lambda i: (jnp.minimum(i, nx_tiles - 1), 0)),
                  pl.BlockSpec((tm, ds), lambda i: (jnp.maximum(i - nx_tiles, 0), 0)),
                  pl.BlockSpec((ds, ds), lambda i: (0, 0)),
                  pl.BlockSpec((1, ds), lambda i: (0, 0)),
                  pl.BlockSpec((1, ds), lambda i: (0, 0))],
        out_specs=pl.BlockSpec((tm, ds), lambda i: (i, 0)),
        out_shape=jax.ShapeDtypeStruct((yx.shape[0] + yc.shape[0], ds), BF16),
        compiler_params=_cp(("parallel",)),
        name="glu",
    )(yx, yc, w, b, gain)


def _route(scores, sel, n_groups):
    epg = len(sel) // n_groups
    gscore = []
    for q in range(n_groups):
        v = sel[q * epg:(q + 1) * epg]
        best = None
        for a in range(epg):
            for b in range(a + 1, epg):
                s = v[a] + v[b]
                best = s if best is None else jnp.maximum(best, s)
        gscore.append(best)
    gbest = gscore[0]
    gidx = jnp.zeros(gbest.shape, jnp.int32)
    for q in range(1, n_groups):
        upd = gscore[q] > gbest
        gbest = jnp.where(upd, gscore[q], gbest)
        gidx = jnp.where(upd, q, gidx)
    vin = list(sel[:epg])
    sin = list(scores[:epg])
    for q in range(1, n_groups):
        pick = gidx == q
        for j in range(epg):
            vin[j] = jnp.where(pick, sel[q * epg + j], vin[j])
            sin[j] = jnp.where(pick, scores[q * epg + j], sin[j])
    b1 = vin[0]
    i1 = jnp.zeros(gbest.shape, jnp.int32)
    for j in range(1, epg):
        upd = vin[j] > b1
        b1 = jnp.where(upd, vin[j], b1)
        i1 = jnp.where(upd, j, i1)
    b2 = vin[0]
    i2 = jnp.zeros(gbest.shape, jnp.int32)
    have = jnp.zeros(gbest.shape, jnp.bool_)
    for j in range(epg):
        cand = i1 != j
        upd = cand & (jnp.logical_not(have) | (vin[j] > b2))
        b2 = jnp.where(upd, vin[j], b2)
        i2 = jnp.where(upd, j, i2)
        have = have | cand
    s1 = sin[0]
    s2 = sin[0]
    for j in range(1, epg):
        s1 = jnp.where(i1 == j, sin[j], s1)
        s2 = jnp.where(i2 == j, sin[j], s2)
    tot = s1 + s2
    return (gidx * epg + i1, gidx * epg + i2), (s1 / tot, s2 / tot)


def _merge_kernel(t_ref, yf_ref, ys_ref, wo_ref, g1_ref, n2_ref, sc2_ref, sh2_ref, rwh_ref, rwl_ref, rb_ref,
                  tn_ref, h3_ref, eidx_ref, ew_ref):
    df = yf_ref.shape[1]
    tm, d = t_ref.shape
    o = (jnp.dot(yf_ref[...], wo_ref[0:df, :], preferred_element_type=F32)
         + jnp.dot(ys_ref[...], wo_ref[df:, :], preferred_element_type=F32))
    tn = t_ref[...] + g1_ref[0] * o
    tn_ref[...] = tn
    h2 = tn * lax.rsqrt(jnp.mean(tn * tn, axis=-1, keepdims=True) + EPS) * n2_ref[...]
    h2 = h2 * (1.0 + sc2_ref[0]) + sh2_ref[0]
    _store_token_rows(h3_ref, h2)
    h_hi = h2.astype(BF16)
    h_lo = (h2 - h_hi.astype(F32)).astype(BF16)
    lg = (jnp.dot(h_hi, rwh_ref[...], preferred_element_type=F32)
          + jnp.dot(h_lo, rwh_ref[...], preferred_element_type=F32)
          + jnp.dot(h_hi, rwl_ref[...], preferred_element_type=F32))
    ne = rb_ref.shape[0]
    logits = lg.T[0:ne, :]
    scores = jax.nn.sigmoid(logits)
    sel = scores + rb_ref[...]
    srows = [scores[e:e + 1, :] for e in range(ne)]
    vrows = [sel[e:e + 1, :] for e in range(ne)]
    (e1, e2), (w1, w2) = _route(srows, vrows, N_EXPERT_GROUPS)
    eidx_ref[...] = jnp.zeros_like(eidx_ref)
    ew_ref[...] = jnp.zeros_like(ew_ref)
    eidx_ref[0:1, :] = e1
    eidx_ref[1:2, :] = e2
    ew_ref[0:1, :] = w1
    ew_ref[1:2, :] = w2


def _merge(t, yf, ys, wo, g1, n2, sc2, sh2, rwh, rwl, rb, n_x_rows, seq):
    nt, d = t.shape
    df = yf.shape[1]
    ne = rb.shape[0]
    nsub = d // LANES
    tm = ROW_TILE
    mrow = functools.partial(_mod_row, tm=tm, n_x_rows=n_x_rows, seq=seq)
    mspec = pl.BlockSpec((1, 1, d), lambda i: (mrow(i), 0, 0))
    return pl.pallas_call(
        _merge_kernel,
        grid=(nt // tm,),
        in_specs=[pl.BlockSpec((tm, d), lambda i: (i, 0)),
                  pl.BlockSpec((tm, df), lambda i: (i, 0)),
                  pl.BlockSpec((tm, d - df), lambda i: (i, 0)),
                  pl.BlockSpec((d, d), lambda i: (0, 0)),
                  mspec,
                  pl.BlockSpec((1, d), lambda i: (0, 0)),
                  mspec, mspec,
                  pl.BlockSpec((d, LANES), lambda i: (0, 0)),
                  pl.BlockSpec((d, LANES), lambda i: (0, 0)),
                  pl.BlockSpec((ne, 1), lambda i: (0, 0))],
        out_specs=[pl.BlockSpec((tm, d), lambda i: (i, 0)),
                   pl.BlockSpec((tm * GATHER_PITCH, LANES), lambda i: (i, 0)),
                   pl.BlockSpec((8, tm), lambda i: (0, i)),
                   pl.BlockSpec((8, tm), lambda i: (0, i))],
        out_shape=[jax.ShapeDtypeStruct((nt, d), F32),
                   jax.ShapeDtypeStruct((nt * GATHER_PITCH, LANES), F32),
                   jax.ShapeDtypeStruct((8, nt), jnp.int32),
                   jax.ShapeDtypeStruct((8, nt), F32)],
        compiler_params=_cp(("parallel",)),
        name="merge_route",
    )(t, yf, ys, wo, g1, n2, sc2, sh2, rwh, rwl, rb)


def _dispatch(eidx, ew, n_experts, rows, n_blocks):
    nt = eidx.shape[1]
    a = nt * TOP_K
    flat_e = eidx[:TOP_K].T.reshape(a)
    onehot = (flat_e[:, None] == jnp.arange(n_experts, dtype=jnp.int32)[None, :]).astype(jnp.int32)
    csum = jnp.cumsum(onehot, axis=0)
    counts = csum[-1]
    padded = (counts + rows - 1) // rows * rows
    pad_end = jnp.cumsum(padded)
    pad_start = pad_end - padded
    dest = jnp.sum(onehot * (csum - 1 + pad_start[None, :]), axis=1)
    tok = jnp.arange(a, dtype=jnp.int32) // TOP_K
    buf_tok = jnp.zeros((n_blocks * rows,), jnp.int32).at[dest].set(tok)
    n_valid = (pad_end[-1] // rows).astype(jnp.int32)
    blk_start = jnp.arange(n_blocks, dtype=jnp.int32) * rows
    blk_exp = jnp.sum((pad_end[None, :] <= blk_start[:, None]).astype(jnp.int32), axis=1)
    blk_exp = jnp.minimum(blk_exp, n_experts - 1)
    last_exp = jnp.sum(jnp.where(jnp.arange(n_blocks) == n_valid - 1, blk_exp, 0))
    blk_exp = jnp.where(jnp.arange(n_blocks) < n_valid, blk_exp, last_exp).astype(jnp.int32)
    return buf_tok, blk_exp, n_valid.reshape(1), dest.reshape(nt, TOP_K), ew[:TOP_K].T


def _store_token_rows(o_ref, x):
    rows, d = x.shape
    nsub = d // LANES
    for c in range(nsub):
        o_ref[pl.ds(c, rows, stride=GATHER_PITCH), :] = x[:, c * LANES:(c + 1) * LANES]
    for c in range(nsub, GATHER_PITCH):
        o_ref[pl.ds(c, rows, stride=GATHER_PITCH), :] = jnp.zeros((rows, LANES), o_ref.dtype)


def _start_row_copies(n_rows, src_hbm, src_row, dst, sem, nsub):
    def body(q, c):
        for k in range(ROW_COPY_UNROLL):
            r = q * ROW_COPY_UNROLL + k
            src = src_hbm.at[pl.ds(pl.multiple_of(src_row(r) * GATHER_PITCH, 4), nsub)]
            pltpu.make_async_copy(src, dst.at[pl.ds(pl.multiple_of(r * GATHER_PITCH, 4), nsub)], sem).start()
        return c

    lax.fori_loop(0, n_rows // ROW_COPY_UNROLL, body, 0)


def _wait_row_copies(n_rows, src_hbm, dst, sem, nsub):
    pltpu.make_async_copy(src_hbm.at[pl.ds(0, n_rows * nsub)], dst.at[pl.ds(0, n_rows * nsub)], sem).wait()


def _expert_kernel(be_ref, nv_ref, tok_ref, tokn_ref, h_hbm, wg_hbm, wu_hbm, wd_hbm, o_ref,
                   wgb_ref, wub_ref, wdb_ref, stg_ref, std_ref, xg_ref, xb_ref, sem, wsem, *, nsub, layer):
    i = pl.program_id(0)
    rows = xb_ref.shape[0]
    valid = i < nv_ref[0]
    slot = i % 2
    e = be_ref[i]
    first = (i == 0) | (be_ref[jnp.maximum(i - 1, 0)] != e)

    def gather(toks, s):
        _start_row_copies(rows, h_hbm, lambda r: toks[0, r], xg_ref.at[s], sem.at[s], nsub)

    @pl.when(i == 0)
    def _():
        gather(tok_ref, 0)

    next_valid = (i + 1 < nv_ref[0]) & (i + 1 < pl.num_programs(0))
    for s in range(2):
        @pl.when(next_valid & (slot == 1 - s))
        def _():
            gather(tokn_ref, s)

    @pl.when(valid & first)
    def _():
        chunks = []
        for src, dst, stg in ((wg_hbm, wgb_ref, stg_ref), (wu_hbm, wub_ref, stg_ref), (wd_hbm, wdb_ref, std_ref)):
            nr = stg.shape[1]
            chunks += [(src, dst, stg, k * nr, nr) for k in range(dst.shape[0] // nr)]

        def copy(k):
            src, _, stg, r0, nr = chunks[k]
            return pltpu.make_async_copy(src.at[layer, e, pl.ds(r0, nr), :], stg.at[k % 2], wsem.at[k % 2])

        copy(0).start()
        for k, (_, dst, stg, r0, nr) in enumerate(chunks):
            if k + 1 < len(chunks):
                copy(k + 1).start()
            copy(k).wait()
            dst[r0:r0 + nr, :] = stg[k % 2].astype(BF16)

    @pl.when(valid)
    def _():
        for s in range(2):
            @pl.when(slot == s)
            def _():
                _wait_row_copies(rows, h_hbm, xg_ref.at[s], sem.at[s], nsub)
                for c in range(nsub):
                    xb_ref[:, c * LANES:(c + 1) * LANES] = (
                        xg_ref[s, pl.ds(c, rows, stride=GATHER_PITCH), :].astype(BF16))
        x = xb_ref[...]
        g = jnp.dot(x, wgb_ref[...], preferred_element_type=F32)
        u = jnp.dot(x, wub_ref[...], preferred_element_type=F32)
        hmid = (g * jax.nn.sigmoid(g)) * u
        y = jnp.dot(hmid.astype(BF16), wdb_ref[...], preferred_element_type=F32)
        _store_token_rows(o_ref, y)

    @pl.when(jnp.logical_not(valid))
    def _():
        o_ref[...] = jnp.zeros_like(o_ref)


def _experts(h3, buf_tok, blk_exp, n_valid, w_gate, w_up, w_down, layer):
    _, ne, d, de = w_gate.shape
    nsub = d // LANES
    rows = MOE_ROWS
    n_blocks = buf_tok.shape[0] // rows
    any_spec = pl.BlockSpec(memory_space=pl.ANY)
    grid_spec = pltpu.PrefetchScalarGridSpec(
        num_scalar_prefetch=2,
        grid=(n_blocks,),
        in_specs=[pl.BlockSpec((None, 1, rows), lambda i, be, nv: (i, 0, 0), memory_space=pltpu.SMEM),
                  pl.BlockSpec((None, 1, rows), lambda i, be, nv: (jnp.minimum(i + 1, n_blocks - 1), 0, 0),
                               memory_space=pltpu.SMEM),
                  any_spec, any_spec, any_spec, any_spec],
        out_specs=pl.BlockSpec((rows * GATHER_PITCH, LANES), lambda i, be, nv: (i, 0)),
        scratch_shapes=[pltpu.VMEM((d, de), BF16),
                        pltpu.VMEM((d, de), BF16),
                        pltpu.VMEM((de, d), BF16),
                        pltpu.VMEM((2, d // MOE_WCHUNKS, de), F32),
                        pltpu.VMEM((2, de // MOE_WCHUNKS, d), F32),
                        pltpu.VMEM((2, rows * GATHER_PITCH, LANES), F32),
                        pltpu.VMEM((rows, d), BF16),
                        pltpu.SemaphoreType.DMA((2,)),
                        pltpu.SemaphoreType.DMA((2,))],
    )
    toks = buf_tok.reshape(n_blocks, 1, rows)
    return pl.pallas_call(
        functools.partial(_expert_kernel, nsub=nsub, layer=layer),
        grid_spec=grid_spec,
        out_shape=jax.ShapeDtypeStruct((n_blocks * rows * GATHER_PITCH, LANES), F32),
        compiler_params=_cp(("arbitrary",)),
        name="experts",
    )(blk_exp, n_valid, toks, toks, h3, w_gate, w_up, w_down)


def _combine_kernel(pos_ref, posn_ref, t_ref, w_ref, yb_hbm, g2_ref, fg_ref, o_ref, gk_ref, sem, *, nsub,
                    final_norm):
    i = pl.program_id(0)
    tm = t_ref.shape[0]
    slot = i % 2

    def gather(pos, s):
        for k in range(TOP_K):
            _start_row_copies(tm, yb_hbm, lambda r, k=k: pos[0, TOP_K * r + k], gk_ref.at[s, k], sem.at[s], nsub)

    @pl.when(i == 0)
    def _():
        gather(pos_ref, 0)

    for s in range(2):
        @pl.when((i + 1 < pl.num_programs(0)) & (slot == 1 - s))
        def _():
            gather(posn_ref, s)

    w0 = w_ref[:, 0:1]
    w1 = w_ref[:, 1:2]
    for s in range(2):
        @pl.when(slot == s)
        def _():
            for k in range(TOP_K):
                _wait_row_copies(tm, yb_hbm, gk_ref.at[s, k], sem.at[s], nsub)
            for c in range(nsub):
                sl = slice(c * LANES, (c + 1) * LANES)
                y = (gk_ref[s, 0, pl.ds(c, tm, stride=GATHER_PITCH), :] * w0
                     + gk_ref[s, 1, pl.ds(c, tm, stride=GATHER_PITCH), :] * w1)
                o_ref[:, sl] = t_ref[:, sl] + g2_ref[0][:, sl] * y

    if final_norm:
        x = o_ref[...]
        o_ref[...] = x * lax.rsqrt(jnp.mean(x * x, axis=-1, keepdims=True) + EPS) * fg_ref[...]


def _combine(t, yb3, pos, w, g2, final_g, n_x_rows, seq, n_out_rows, final_norm):
    nt, d = t.shape
    nsub = d // LANES
    tm = ROW_TILE
    mrow = functools.partial(_mod_row, tm=tm, n_x_rows=n_x_rows, seq=seq)
    nb = n_out_rows // tm
    posb = pos.reshape(nt // tm, 1, TOP_K * tm)
    return pl.pallas_call(
        functools.partial(_combine_kernel, nsub=nsub, final_norm=final_norm),
        grid=(nb,),
        in_specs=[pl.BlockSpec((None, 1, TOP_K * tm), lambda i: (i, 0, 0), memory_space=pltpu.SMEM),
                  pl.BlockSpec((None, 1, TOP_K * tm), lambda i: (jnp.minimum(i + 1, nb - 1), 0, 0),
                               memory_space=pltpu.SMEM),
                  pl.BlockSpec((tm, d), lambda i: (i, 0)),
                  pl.BlockSpec((tm, TOP_K), lambda i: (i, 0)),
                  pl.BlockSpec(memory_space=pl.ANY),
                  pl.BlockSpec((1, 1, d), lambda i: (mrow(i), 0, 0)),
                  pl.BlockSpec((1, d), lambda i: (0, 0))],
        out_specs=pl.BlockSpec((tm, d), lambda i: (i, 0)),
        out_shape=jax.ShapeDtypeStruct((n_out_rows, d), F32),
        scratch_shapes=[pltpu.VMEM((2, TOP_K, tm * GATHER_PITCH, LANES), F32),
                        pltpu.SemaphoreType.DMA((2,))],
        compiler_params=_cp(("arbitrary",)),
        name="combine",
    )(posb, posb, t, w, yb3, g2, final_g)


def kernel(x, c, ctx, c_ctx, w_mod, b_mod, norm1_g, norm2_g, w_in, w_out, fourier_w, mix_norm_g,
           lam_re, lam_im, log_dt, b_re, b_im, c_re, c_im, d_skip, glu_w, glu_b,
           router_w, router_b, w_gate, w_up, w_down, final_g):
    bsz, seq, d = x.shape
    n_ctx = ctx.shape[1]
    depth = w_mod.shape[0]
    df = fourier_w.shape[1] * fourier_w.shape[2]
    ds = d_skip.shape[1]
    ne = router_w.shape[1]
    nx_rows = bsz * seq
    nt = nx_rows + bsz * n_ctx

    t = jnp.concatenate([x.reshape(nx_rows, d), ctx.reshape(bsz * n_ctx, d)], axis=0).astype(F32)

    cvec = jnp.concatenate([c_ctx[None, :], c, jnp.zeros((8 - 1 - bsz, d), c.dtype)], axis=0).astype(F32)
    mod = _adaln(cvec, w_mod, b_mod).reshape(depth, 8, N_MOD, 1, d)

    cc, sc_ = _dft_tables(df)
    w_chan = jnp.concatenate([cc, sc_], axis=1).astype(BF16)
    tabs_x = _half_tables(seq)
    tabs_c = _half_tables(n_ctx)
    s5_tabs = jax.vmap(_s5_tables)(lam_re, lam_im, log_dt, b_re, b_im, c_re, c_im, d_skip)

    rw = jnp.pad(router_w.astype(F32), ((0, 0), (0, LANES - ne)))
    rwh = rw.astype(BF16)
    rwl = (rw - rwh.astype(F32)).astype(BF16)
    rb = router_b.astype(F32).reshape(ne, 1)
    fg = final_g.reshape(1, d).astype(F32)
    n_blocks = -(-(nt * TOP_K) // MOE_ROWS) + ne

    for l in range(depth):
        sh1, sc1, g1, sh2, sc2, g2 = [mod[l, :, k] for k in range(N_MOD)]
        zf, zs = _inproj(t, norm1_g[l].reshape(1, d).astype(F32), sc1, sh1, w_in[l].astype(BF16),
                         df, nx_rows, seq)

        hw = fourier_w[l].astype(BF16)
        gain = mix_norm_g[l].astype(F32).reshape(1, -1)
        yf = _fourier_mix(zf, w_chan, tabs_x, hw, gain[:, :df], 0, bsz, seq, jnp.zeros((nt, df), BF16))
        yf = _fourier_mix(zf, w_chan, tabs_c, hw, gain[:, :df], nx_rows, bsz, n_ctx, yf)

        tables = [tab[l] for tab in s5_tabs]
        ysx, ysc = _s5_scan(zs, tables, bsz, seq, n_ctx)
        ys = _glu(ysx, ysc, glu_w[l].astype(BF16), glu_b[l].astype(F32).reshape(1, ds), gain[:, df:])

        t, h3, eidx, ew = _merge(t, yf, ys, w_out[l].astype(BF16), g1,
                                 norm2_g[l].reshape(1, d).astype(F32), sc2, sh2, rwh, rwl, rb, nx_rows, seq)

        buf_tok, blk_exp, n_valid, pos, wtok = _dispatch(eidx, ew, ne, MOE_ROWS, n_blocks)
        yb = _experts(h3, buf_tok, blk_exp, n_valid, w_gate, w_up, w_down, l)
        last = l == depth - 1
        t = _combine(t, yb, pos, wtok, g2, fg, nx_rows, seq, nx_rows if last else nt, last)

    return t.reshape(bsz, seq, d).astype(x.dtype)
```

```python
import functools
import math

import jax
import jax.numpy as jnp
from jax import lax
from jax.experimental import pallas as pl
from jax.experimental.pallas import tpu as pltpu

F32 = jnp.float32
BF16 = jnp.bfloat16
EPS = 1e-6

FOURIER_HEADS = 4
SSM_GROUP = 16
N_EXPERT_GROUPS = 4
TOP_K = 2
N_MOD = 6

SSM_CHUNK = 16
ROW_TILE = 256
MOE_ROWS = 256
MOE_WCHUNKS = 8
ROW_COPY_UNROLL = 8
LANES = 128
GATHER_PITCH = 20
S5_SET = LANES
VMEM_LIMIT_BYTES = 56 * 1024 * 1024


def _cp(sems):
    return pltpu.CompilerParams(dimension_semantics=sems, vmem_limit_bytes=VMEM_LIMIT_BYTES)


def _mod_row(i, tm, n_x_rows, seq):
    r = i * tm
    return jnp.where(r < n_x_rows, 1 + r // seq, 0)


def _mod_kernel(c_ref, w_ref, b_ref, o_ref):
    c = c_ref[...]
    s = c * jax.nn.sigmoid(c)
    o_ref[...] = jnp.dot(s.astype(BF16), w_ref[...].astype(BF16),
                         preferred_element_type=F32) + b_ref[...]


def _adaln(cvec, w_mod, b_mod):
    depth, d, n = w_mod.shape
    tn = 1024
    return pl.pallas_call(
        _mod_kernel,
        grid=(depth, n // tn),
        in_specs=[pl.BlockSpec((8, d), lambda l, j: (0, 0)),
                  pl.BlockSpec((None, d, tn), lambda l, j: (l, 0, j)),
                  pl.BlockSpec((None, 1, tn), lambda l, j: (l, 0, j))],
        out_specs=pl.BlockSpec((None, 8, tn), lambda l, j: (l, 0, j)),
        out_shape=jax.ShapeDtypeStruct((depth, 8, n), F32),
        compiler_params=_cp(("parallel", "parallel")),
        name="adaln",
    )(cvec, w_mod, b_mod.reshape(depth, 1, n))


def _inproj_kernel(t_ref, g_ref, sc_ref, sh_ref, w_ref, zf_ref, zs_ref):
    x = t_ref[...]
    h = x * lax.rsqrt(jnp.mean(x * x, axis=-1, keepdims=True) + EPS) * g_ref[...]
    h = h * (1.0 + sc_ref[0]) + sh_ref[0]
    z = jnp.dot(h.astype(BF16), w_ref[...], preferred_element_type=F32)
    df = zf_ref.shape[1]
    zf_ref[...] = z[:, :df].astype(zf_ref.dtype)
    zs_ref[...] = z[:, df:]


def _inproj(t, g, sc, sh, w, df, n_x_rows, seq):
    nt, d = t.shape
    n = w.shape[1]
    tm = ROW_TILE
    mrow = functools.partial(_mod_row, tm=tm, n_x_rows=n_x_rows, seq=seq)
    return pl.pallas_call(
        _inproj_kernel,
        grid=(nt // tm,),
        in_specs=[pl.BlockSpec((tm, d), lambda i: (i, 0)),
                  pl.BlockSpec((1, d), lambda i: (0, 0)),
                  pl.BlockSpec((1, 1, d), lambda i: (mrow(i), 0, 0)),
                  pl.BlockSpec((1, 1, d), lambda i: (mrow(i), 0, 0)),
                  pl.BlockSpec((d, n), lambda i: (0, 0))],
        out_specs=[pl.BlockSpec((tm, df), lambda i: (i, 0)),
                   pl.BlockSpec((tm, n - df), lambda i: (i, 0))],
        out_shape=[jax.ShapeDtypeStruct((nt, df), BF16),
                   jax.ShapeDtypeStruct((nt, n - df), F32)],
        compiler_params=_cp(("parallel",)),
        name="inproj",
    )(t, g, sc, sh, w)


def _dft_tables(n):
    r = 1
    while r * r < n:
        r *= 2
    q = n // r
    k = jnp.arange(n, dtype=jnp.int32)[:, None]
    step = 2.0 * math.pi / n
    pa = ((k * (jnp.arange(q, dtype=jnp.int32)[None, :] * r)) % n).astype(F32) * step
    pb = ((k * jnp.arange(r, dtype=jnp.int32)[None, :]) % n).astype(F32) * step
    ca, sa = jnp.cos(pa)[:, :, None], jnp.sin(pa)[:, :, None]
    cb, sb = jnp.cos(pb)[:, None, :], jnp.sin(pb)[:, None, :]
    c = (ca * cb - sa * sb).reshape(n, n)
    s = (sa * cb + ca * sb).reshape(n, n)
    return c, s


def _chan_dft_kernel(z_ref, w_ref, o_ref):
    o_ref[...] = jnp.dot(z_ref[...], w_ref[...], preferred_element_type=F32).astype(o_ref.dtype)


def _chan_dft(z, w):
    nt = z.shape[0]
    df, n = w.shape
    tm = min(512, nt)
    return pl.pallas_call(
        _chan_dft_kernel,
        grid=(nt // tm,),
        in_specs=[pl.BlockSpec((tm, df), lambda i: (i, 0)),
                  pl.BlockSpec((df, n), lambda i: (0, 0))],
        out_specs=pl.BlockSpec((tm, n), lambda i: (i, 0)),
        out_shape=jax.ShapeDtypeStruct((nt, n), BF16),
        compiler_params=_cp(("parallel",)),
        name="chan_dft",
    )(z, w)


def _chan_sym_kernel(z_ref, zr_ref, w_ref, o_ref):
    z = z_ref[...].astype(F32)
    zr = zr_ref[...].astype(F32)
    c = w_ref.shape[0]
    o_ref[:, :c] = jnp.dot((z + zr).astype(BF16), w_ref[:, :c], preferred_element_type=F32).astype(o_ref.dtype)
    o_ref[:, c:] = jnp.dot((z - zr).astype(BF16), w_ref[:, c:], preferred_element_type=F32).astype(o_ref.dtype)


def _chan_sym(z, zr, w, row0, bsz, length):
    df, n = w.shape
    half = length // 2
    tm = min(512, half)
    hb = half // tm
    off = row0 // tm
    return pl.pallas_call(
        _chan_sym_kernel,
        grid=(bsz, hb),
        in_specs=[pl.BlockSpec((tm, df), lambda b, i: (off + b * 2 * hb + i, 0)),
                  pl.BlockSpec((tm, df), lambda b, i: (b * hb + i, 0)),
                  pl.BlockSpec((df, n), lambda b, i: (0, 0))],
        out_specs=pl.BlockSpec((tm, n), lambda b, i: (b * hb + i, 0)),
        out_shape=jax.ShapeDtypeStruct((bsz * half, n), BF16),
        compiler_params=_cp(("parallel", "parallel")),
        name="chan_sym",
    )(z, zr, w)


def _fourier_epilogue(f, hw_ref, gain_ref, o_ref):
    nh, hd, _ = hw_ref.shape
    ys = [jnp.dot(f[:, h * hd:(h + 1) * hd].astype(BF16), hw_ref[h],
                  preferred_element_type=F32) for h in range(nh)]
    ssum = jnp.sum(ys[0] * ys[0], axis=-1, keepdims=True)
    for h in range(1, nh):
        ssum = ssum + jnp.sum(ys[h] * ys[h], axis=-1, keepdims=True)
    r = lax.rsqrt(ssum / (nh * hd) + EPS)
    for h in range(nh):
        o_ref[:, h * hd:(h + 1) * hd] = (ys[h] * r * gain_ref[:, h * hd:(h + 1) * hd]).astype(o_ref.dtype)


def _row_sym_kernel(ch_ref, sh_ref, alt_ref, ae_ref, bo_ref, amid_ref, hw_ref, gain_ref, lo_ref, hi_ref,
                    p_ref, q_ref, r_ref, *, scale, mid_sign):
    m = pl.program_id(1)
    k = pl.program_id(2)

    @pl.when(k == 0)
    def _():
        p_ref[...] = jnp.zeros_like(p_ref)
        q_ref[...] = jnp.zeros_like(q_ref)
        r_ref[...] = jnp.zeros_like(r_ref)

    ae = ae_ref[...]
    p_ref[...] += jnp.dot(ch_ref[...], ae, preferred_element_type=F32)
    q_ref[...] += jnp.dot(sh_ref[...], bo_ref[...], preferred_element_type=F32)

    @pl.when(m == 0)
    def _():
        r_ref[...] += jnp.dot(alt_ref[...], ae, preferred_element_type=F32)

    @pl.when(k == pl.num_programs(2) - 1)
    def _():
        tm = p_ref.shape[0]
        row = lax.broadcasted_iota(jnp.int32, (tm, 1), 0) + m * tm
        amid = amid_ref[0:1, :]
        pf = p_ref[...] + jnp.where((row & 1) == 0, 1.0, -1.0) * amid
        q = q_ref[...]
        mid = r_ref[0:1, :] + mid_sign * amid
        hi = jnp.where(row == 0, mid, pf + q)
        _fourier_epilogue((pf - q) * scale, hw_ref, gain_ref, lo_ref)
        _fourier_epilogue(hi * scale, hw_ref, gain_ref, hi_ref)


def _without_alias_ref(kernel, pos):
    def wrapped(*refs):
        return kernel(*refs[:pos], *refs[pos + 1:])
    return wrapped


def _row_sym(aebo, amid, ch, sh, alt, hw, gain, bsz, length, row0, ybuf, n_rows):
    df = aebo.shape[1] // 2
    half = length // 2
    tm = tk = min(512, half)
    mb, kb = half // tm, half // tk
    scale = 1.0 / math.sqrt(length * df)
    mid_sign = 1.0 if half % 2 == 0 else -1.0
    off = row0 // tm
    kern = functools.partial(_row_sym_kernel, scale=scale, mid_sign=mid_sign)
    in_specs = [pl.BlockSpec((tm, tk), lambda b, m, k: (m, k)),
                pl.BlockSpec((tm, tk), lambda b, m, k: (m, k)),
                pl.BlockSpec((8, tk), lambda b, m, k: (0, k)),
                pl.BlockSpec((tk, df), lambda b, m, k: (b * kb + k, 0)),
                pl.BlockSpec((tk, df), lambda b, m, k: (b * kb + k, 1)),
                pl.BlockSpec((8, df), lambda b, m, k: (b, 0)),
                pl.BlockSpec(hw.shape, lambda b, m, k: (0, 0, 0)),
                pl.BlockSpec((1, df), lambda b, m, k: (0, 0))]
    args = [ch, sh, alt, aebo, aebo, amid, hw, gain]
    aliases = {}
    if ybuf is not None:
        kern = _without_alias_ref(kern, len(args))
        in_specs.append(pl.BlockSpec(memory_space=pl.ANY))
        aliases = {len(args): 0}
        args.append(ybuf)
    return pl.pallas_call(
        kern,
        grid=(bsz, mb, kb),
        in_specs=in_specs,
        out_specs=[pl.BlockSpec((tm, df), lambda b, m, k: (off + b * 2 * mb + m, 0)),
                   pl.BlockSpec((tm, df), lambda b, m, k: (b * mb + m, 0))],
        out_shape=[jax.ShapeDtypeStruct((n_rows, df), BF16),
                   jax.ShapeDtypeStruct((bsz * half, df), BF16)],
        scratch_shapes=[pltpu.VMEM((tm, df), F32), pltpu.VMEM((tm, df), F32), pltpu.VMEM((8, df), F32)],
        input_output_aliases=aliases,
        compiler_params=_cp(("parallel", "parallel", "arbitrary")),
        name="row_sym",
    )(*args)


def _mirror_kernel(j_ref, a_ref, b_ref, o_ref, *, keep_first):
    rev = jnp.dot(j_ref[...], a_ref[...], preferred_element_type=F32)
    first = b_ref[0:1, :].astype(F32)
    if not keep_first:
        first = jnp.where(pl.program_id(1) == 0, 0.0, first)
    row = lax.broadcasted_iota(jnp.int32, (o_ref.shape[0], 1), 0)
    o_ref[...] = jnp.where(row == 0, first, rev).astype(o_ref.dtype)


def _mirror(src, src_row0, src_stride, half, bsz, keep_first, dst_row0=0, dst_stride=None, ybuf=None):
    df = src.shape[1]
    tm = min(512, half)
    nb = half // tm
    dst_stride = half if dst_stride is None else dst_stride
    soff, sstr, doff, dstr = src_row0 // tm, src_stride // tm, dst_row0 // tm, dst_stride // tm
    r = jnp.arange(tm, dtype=jnp.int32)
    jmat = ((r[:, None] + r[None, :] == tm) & (r[:, None] > 0)).astype(BF16)
    kern = functools.partial(_mirror_kernel, keep_first=keep_first)
    in_specs = [pl.BlockSpec((tm, tm), lambda b, i: (0, 0)),
                pl.BlockSpec((tm, df), lambda b, i: (soff + b * sstr + nb - 1 - i, 0)),
                pl.BlockSpec((tm, df), lambda b, i: (soff + b * sstr + (nb - i) % nb, 0))]
    args = [jmat, src, src]
    aliases = {}
    n_rows = bsz * half
    if ybuf is not None:
        kern = _without_alias_ref(kern, len(args))
        in_specs.append(pl.BlockSpec(memory_space=pl.ANY))
        aliases = {len(args): 0}
        args.append(ybuf)
        n_rows = ybuf.shape[0]
    return pl.pallas_call(
        kern,
        grid=(bsz, nb),
        in_specs=in_specs,
        out_specs=pl.BlockSpec((tm, df), lambda b, i: (doff + b * dstr + i, 0)),
        out_shape=jax.ShapeDtypeStruct((n_rows, df), BF16),
        input_output_aliases=aliases,
        compiler_params=_cp(("parallel", "parallel")),
        name="mirror",
    )(*args)


def _fourier_mix(zf, w_chan, tabs, hw, gain, row0, bsz, length, ybuf):
    ch, sh, alt = tabs
    df = zf.shape[1]
    half = length // 2
    zr = _mirror(zf, row0 + half, length, half, bsz, keep_first=False)
    aebo = _chan_sym(zf, zr, w_chan, row0, bsz, length)
    zmid = zf[row0:row0 + bsz * length].reshape(bsz, length, df)[:, half]
    zmid = jnp.pad(zmid[:, None, :], ((0, 0), (0, 7), (0, 0)))
    amid = _chan_dft(zmid.reshape(bsz * 8, df), w_chan)[:, :df].astype(F32)
    y, hi = _row_sym(aebo, amid, ch, sh, alt, hw, gain, bsz, length, row0, ybuf, zf.shape[0])
    return _mirror(hi, 0, half, half, bsz, keep_first=True, dst_row0=row0 + half, dst_stride=length, ybuf=y)


def _half_tables(length):
    half = length // 2
    c, s = _dft_tables(length)
    n = jnp.arange(half, dtype=jnp.int32)
    alt = jnp.where((lax.broadcasted_iota(jnp.int32, (8, half), 0) == 0),
                    jnp.where(n % 2 == 0, 1.0, -1.0)[None, :], 0.0)
    return c[:half, :half].astype(BF16), s[:half, :half].astype(BF16), alt.astype(BF16)


def _s5_tables(lam_re, lam_im, log_dt, b_re, b_im, c_re, c_im, d_skip):
    tc = SSM_CHUNK
    g, p = lam_re.shape[1:]
    hch = b_re.shape[-1]
    lr, li = lam_re.astype(F32), lam_im.astype(F32)
    dt = jnp.exp(log_dt.astype(F32))[..., None]
    er, ei = lr * dt, li * dt

    def lpow(k):
        m = jnp.exp(er * k)
        return m * jnp.cos(ei * k), m * jnp.sin(ei * k)

    l1r, l1i = lpow(1.0)
    den = lr * lr + li * li
    qr = ((l1r - 1.0) * lr + l1i * li) / den
    qi = (l1i * lr - (l1r - 1.0) * li) / den
    br, bi = b_re.astype(F32), b_im.astype(F32)
    bbr = qr[..., None] * br - qi[..., None] * bi
    bbi = qr[..., None] * bi + qi[..., None] * br
    cr, ci = c_re.astype(F32), c_im.astype(F32)

    ks = jnp.arange(tc + 1, dtype=F32)
    pwr = jnp.stack([lpow(k)[0] for k in range(tc + 1)], axis=-1)
    pwi = jnp.stack([lpow(k)[1] for k in range(tc + 1)], axis=-1)
    del ks

    cbr = jnp.einsum("dgip,dgpj->dgpij", cr, bbr) - jnp.einsum("dgip,dgpj->dgpij", ci, bbi)
    cbi = jnp.einsum("dgip,dgpj->dgpij", cr, bbi) + jnp.einsum("dgip,dgpj->dgpij", ci, bbr)
    klag = (jnp.einsum("dgpij,dgpl->dgjli", cbr, pwr[..., :tc])
            - jnp.einsum("dgpij,dgpl->dgjli", cbi, pwi[..., :tc]))

    dg = d_skip.astype(F32).reshape(g, hch)
    lag0 = klag[0][:, :, 0] + klag[1][:, :, 0] + dg[:, :, None] * jnp.eye(hch, dtype=F32)[None]
    seq = jnp.concatenate([jnp.flip(klag[1][:, :, 1:], axis=2), lag0[:, :, None, :], klag[0][:, :, 1:]], axis=2)
    seq = seq.reshape(g, hch, (2 * tc - 1) * hch)
    m = jnp.stack([seq[:, :, (tc - 1 - t) * hch:(2 * tc - 1 - t) * hch] for t in range(tc)], axis=1)
    m = m.reshape(g, tc * hch, tc * hch)

    idx_f = jnp.arange(tc - 1, -1, -1)
    idx_b = jnp.arange(tc)

    def st(d, idx):
        wr = pwr[d][..., idx][:, :, :, None] * bbr[d][:, :, None, :] - pwi[d][..., idx][:, :, :, None] * bbi[d][:, :, None, :]
        wi = pwr[d][..., idx][:, :, :, None] * bbi[d][:, :, None, :] + pwi[d][..., idx][:, :, :, None] * bbr[d][:, :, None, :]
        return (wr.transpose(0, 2, 3, 1).reshape(g, tc * hch, p),
                wi.transpose(0, 2, 3, 1).reshape(g, tc * hch, p))

    sfr, sfi = st(0, idx_f)
    sbr, sbi = st(1, idx_b)
    w_st = jnp.concatenate([sfr, sbr, sfi, sbi], axis=-1)

    def so(d, idx):
        wr = cr[d][:, None, :, :] * pwr[d][..., idx].transpose(0, 2, 1)[:, :, None, :] \
            - ci[d][:, None, :, :] * pwi[d][..., idx].transpose(0, 2, 1)[:, :, None, :]
        wi = cr[d][:, None, :, :] * pwi[d][..., idx].transpose(0, 2, 1)[:, :, None, :] \
            + ci[d][:, None, :, :] * pwr[d][..., idx].transpose(0, 2, 1)[:, :, None, :]
        return (wr.reshape(g, tc * hch, p).transpose(0, 2, 1),
                wi.reshape(g, tc * hch, p).transpose(0, 2, 1))

    ofr, ofi = so(0, jnp.arange(1, tc + 1))
    obr, obi = so(1, jnp.arange(tc, 0, -1))
    zero = jnp.zeros_like(ofr)
    w_of = jnp.concatenate([ofr, zero, -ofi, zero], axis=1)
    w_ob = jnp.concatenate([zero, obr, zero, -obi], axis=1)

    a_re = jnp.concatenate([pwr[0][..., tc], pwr[1][..., tc]], axis=-1)
    a_im = jnp.concatenate([pwi[0][..., tc], pwi[1][..., tc]], axis=-1)

    gs = S5_SET // hch
    ns = g // gs
    kd = tc * hch
    w_o = w_of + w_ob
    return (m.astype(BF16).reshape(ns, gs, kd, kd),
            w_st.astype(BF16).reshape(ns, gs, kd, 4 * p),
            w_o.astype(BF16).reshape(ns, gs, 4 * p, kd),
            a_re.reshape(ns, gs, 2 * p), a_im.reshape(ns, gs, 2 * p))


def _chunk_perm(tc, gs, hch):
    n = tc * gs * hch
    r = jnp.arange(n, dtype=jnp.int32)
    t, g, j = r // (gs * hch), (r // hch) % gs, r % hch
    dst = g * (tc * hch) + t * hch + j
    fwd = (dst[:, None] == r[None, :]).astype(BF16)
    return fwd, fwd.T


def _s5_kernel(zx_ref, zc_ref, pf_ref, pb_ref, m_ref, wst_ref, wo_ref, are_ref, aim_ref, ox_ref, oc_ref,
               sh_ref, xb_ref, yb_ref, *, ctx_chunks, x_chunks, pitch):
    tc = SSM_CHUNK
    nch = ctx_chunks + x_chunks
    lanes = zx_ref.shape[1]
    gs, kd, nst = wst_ref.shape
    for t in range(tc):
        xb_ref[0:ctx_chunks, t * lanes:(t + 1) * lanes] = zc_ref[pl.ds(t, ctx_chunks, stride=tc), :].astype(BF16)
        xb_ref[ctx_chunks:nch, t * lanes:(t + 1) * lanes] = zx_ref[pl.ds(t, x_chunks, stride=tc), :].astype(BF16)
    xb_ref[...] = jnp.dot(xb_ref[...], pf_ref[...], preferred_element_type=F32).astype(BF16)

    nslab = gs * nst // lanes
    half = nslab // 2
    for j in range(gs):
        s = jnp.dot(xb_ref[:, j * kd:(j + 1) * kd], wst_ref[j], preferred_element_type=F32)
        sh_ref[j * pitch:j * pitch + nch, :] = s[:, 0:lanes]
        sh_ref[(half + j) * pitch:(half + j) * pitch + nch, :] = s[:, lanes:2 * lanes]

    fwd = lax.broadcasted_iota(jnp.int32, (half, lanes), 1) < (lanes // 2)
    a_re = are_ref[...]
    a_im = aim_ref[...]

    def rows(c, part):
        return pl.ds(part * half * pitch + c, half, stride=pitch)

    def step(i, carry):
        hr, hi = carry
        cf = i
        cb = jnp.where(i < ctx_chunks, ctx_chunks - 1 - i, nch - 1 + ctx_chunks - i)
        same = cf == cb
        sfr, sfi = sh_ref[rows(cf, 0), :], sh_ref[rows(cf, 1), :]
        sbr, sbi = sh_ref[rows(cb, 0), :], sh_ref[rows(cb, 1), :]
        sh_ref[rows(cf, 0), :] = jnp.where(fwd, hr, sfr)
        sh_ref[rows(cf, 1), :] = jnp.where(fwd, hi, sfi)
        sh_ref[rows(cb, 0), :] = jnp.where(fwd, jnp.where(same, hr, sbr), hr)
        sh_ref[rows(cb, 1), :] = jnp.where(fwd, jnp.where(same, hi, sbi), hi)
        sr = jnp.where(fwd, sfr, sbr)
        si = jnp.where(fwd, sfi, sbi)
        return a_re * hr - a_im * hi + sr, a_re * hi + a_im * hr + si

    zero = jnp.zeros((half, lanes), F32)
    lax.fori_loop(0, nch, step, (zero, zero))

    for j in range(gs):
        hb = jnp.concatenate([sh_ref[j * pitch:j * pitch + nch, :],
                              sh_ref[(half + j) * pitch:(half + j) * pitch + nch, :]], axis=1).astype(BF16)
        yj = (jnp.dot(xb_ref[:, j * kd:(j + 1) * kd], m_ref[j], preferred_element_type=F32)
              + jnp.dot(hb, wo_ref[j], preferred_element_type=F32))
        yb_ref[:, j * kd:(j + 1) * kd] = yj.astype(BF16)
    y = jnp.dot(yb_ref[...], pb_ref[...], preferred_element_type=F32)
    for t in range(tc):
        oc_ref[pl.ds(t, ctx_chunks, stride=tc), :] = y[0:ctx_chunks, t * lanes:(t + 1) * lanes]
        ox_ref[pl.ds(t, x_chunks, stride=tc), :] = y[ctx_chunks:nch, t * lanes:(t + 1) * lanes]


def _s5_scan(zs, tables, bsz, seq, n_ctx):
    m, w_st, w_o, a_re, a_im = tables
    ns, gs, kd, nst = w_st.shape
    tc = SSM_CHUNK
    lanes = S5_SET
    kset = gs * kd
    ctx_chunks, x_chunks = n_ctx // tc, seq // tc
    nch = ctx_chunks + x_chunks
    pitch = -(-nch // 8) * 8
    if (pitch // 8) % 2 == 0:
        pitch += 8
    ctx0 = bsz * seq // n_ctx
    perm_f, perm_b = _chunk_perm(tc, gs, kd // tc)
    kern = functools.partial(_s5_kernel, ctx_chunks=ctx_chunks, x_chunks=x_chunks, pitch=pitch)
    per_set = lambda arr: pl.BlockSpec((None,) + arr.shape[1:], lambda s, b: (s,) + (0,) * (arr.ndim - 1))
    return pl.pallas_call(
        kern,
        grid=(ns, bsz),
        in_specs=[pl.BlockSpec((seq, lanes), lambda s, b: (b, s)),
                  pl.BlockSpec((n_ctx, lanes), lambda s, b: (ctx0 + b, s)),
                  pl.BlockSpec((kset, kset), lambda s, b: (0, 0)),
                  pl.BlockSpec((kset, kset), lambda s, b: (0, 0)),
                  per_set(m), per_set(w_st), per_set(w_o), per_set(a_re), per_set(a_im)],
        out_specs=[pl.BlockSpec((seq, lanes), lambda s, b: (b, s)),
                   pl.BlockSpec((n_ctx, lanes), lambda s, b: (b, s))],
        out_shape=[jax.ShapeDtypeStruct((bsz * seq, zs.shape[1]), F32),
                   jax.ShapeDtypeStruct((bsz * n_ctx, zs.shape[1]), F32)],
        scratch_shapes=[pltpu.VMEM((gs * nst // lanes * pitch, lanes), F32),
                        pltpu.VMEM((nch, kset), BF16),
                        pltpu.VMEM((nch, kset), BF16)],
        compiler_params=_cp(("parallel", "parallel")),
        name="s5_scan",
    )(zs, zs, perm_f, perm_b, m, w_st, w_o, a_re, a_im)


def _glu_kernel(yx_ref, yc_ref, w_ref, b_ref, gain_ref, o_ref, *, nx_tiles):
    y = jnp.where(pl.program_id(0) < nx_tiles, yx_ref[...], yc_ref[...])
    g = jax.nn.gelu(y)
    v = g * jax.nn.sigmoid(jnp.dot(g.astype(BF16), w_ref[...], preferred_element_type=F32) + b_ref[...])
    r = lax.rsqrt(jnp.mean(v * v, axis=-1, keepdims=True) + EPS)
    o_ref[...] = (v * r * gain_ref[...]).astype(o_ref.dtype)


def _glu(yx, yc, w, b, gain):
    ds = yx.shape[1]
    tm = 512
    nx_tiles, nc_tiles = yx.shape[0] // tm, yc.shape[0] // tm
    return pl.pallas_call(
        functools.partial(_glu_kernel, nx_tiles=nx_tiles),
        grid=(nx_tiles + nc_tiles,),
        in_specs=[pl.BlockSpec((tm, ds), lambda i: (jnp.minimum(i, nx_tiles - 1), 0)),
                  pl.BlockSpec((tm, ds), lambda i: (jnp.maximum(i - nx_tiles, 0), 0)),
                  pl.BlockSpec((ds, ds), lambda i: (0, 0)),
                  pl.BlockSpec((1, ds), lambda i: (0, 0)),
                  pl.BlockSpec((1, ds), lambda i: (0, 0))],
        out_specs=pl.BlockSpec((tm, ds), lambda i: (i, 0)),
        out_shape=jax.ShapeDtypeStruct((yx.shape[0] + yc.shape[0], ds), BF16),
        compiler_params=_cp(("parallel",)),
        name="glu",
    )(yx, yc, w, b, gain)


def _route(scores, sel, n_groups):
    epg = len(sel) // n_groups
    gscore = []
    for q in range(n_groups):
        v = sel[q * epg:(q + 1) * epg]
        best = None
        for a in range(epg):
            for b in range(a + 1, epg):
                s = v[a] + v[b]
                best = s if best is None else jnp.maximum(best, s)
        gscore.append(best)
    gbest = gscore[0]
    gidx = jnp.zeros(gbest.shape, jnp.int32)
    for q in range(1, n_groups):
        upd = gscore[q] > gbest
        gbest = jnp.where(upd, gscore[q], gbest)
        gidx = jnp.where(upd, q, gidx)
    vin = list(sel[:epg])
    sin = list(scores[:epg])
    for q in range(1, n_groups):
        pick = gidx == q
        for j in range(epg):
            vin[j] = jnp.where(pick, sel[q * epg + j], vin[j])
            sin[j] = jnp.where(pick, scores[q * epg + j], sin[j])
    b1 = vin[0]
    i1 = jnp.zeros(gbest.shape, jnp.int32)
    for j in range(1, epg):
        upd = vin[j] > b1
        b1 = jnp.where(upd, vin[j], b1)
        i1 = jnp.where(upd, j, i1)
    b2 = vin[0]
    i2 = jnp.zeros(gbest.shape, jnp.int32)
    have = jnp.zeros(gbest.shape, jnp.bool_)
    for j in range(epg):
        cand = i1 != j
        upd = cand & (jnp.logical_not(have) | (vin[j] > b2))
        b2 = jnp.where(upd, vin[j], b2)
        i2 = jnp.where(upd, j, i2)
        have = have | cand
    s1 = sin[0]
    s2 = sin[0]
    for j in range(1, epg):
        s1 = jnp.where(i1 == j, sin[j], s1)
        s2 = jnp.where(i2 == j, sin[j], s2)
    tot = s1 + s2
    return (gidx * epg + i1, gidx * epg + i2), (s1 / tot, s2 / tot)


def _merge_kernel(t_ref, yf_ref, ys_ref, wo_ref, g1_ref, n2_ref, sc2_ref, sh2_ref, rwh_ref, rwl_ref, rb_ref,
                  tn_ref, h3_ref, eidx_ref, ew_ref):
    df = yf_ref.shape[1]
    tm, d = t_ref.shape
    o = (jnp.dot(yf_ref[...], wo_ref[0:df, :], preferred_element_type=F32)
         + jnp.dot(ys_ref[...], wo_ref[df:, :], preferred_element_type=F32))
    tn = t_ref[...] + g1_ref[0] * o
    tn_ref[...] = tn
    h2 = tn * lax.rsqrt(jnp.mean(tn * tn, axis=-1, keepdims=True) + EPS) * n2_ref[...]
    h2 = h2 * (1.0 + sc2_ref[0]) + sh2_ref[0]
    _store_token_rows(h3_ref, h2)
    h_hi = h2.astype(BF16)
    h_lo = (h2 - h_hi.astype(F32)).astype(BF16)
    lg = (jnp.dot(h_hi, rwh_ref[...], preferred_element_type=F32)
          + jnp.dot(h_lo, rwh_ref[...], preferred_element_type=F32)
          + jnp.dot(h_hi, rwl_ref[...], preferred_element_type=F32))
    ne = rb_ref.shape[0]
    logits = lg.T[0:ne, :]
    scores = jax.nn.sigmoid(logits)
    sel = scores + rb_ref[...]
    srows = [scores[e:e + 1, :] for e in range(ne)]
    vrows = [sel[e:e + 1, :] for e in range(ne)]
    (e1, e2), (w1, w2) = _route(srows, vrows, N_EXPERT_GROUPS)
    eidx_ref[...] = jnp.zeros_like(eidx_ref)
    ew_ref[...] = jnp.zeros_like(ew_ref)
    eidx_ref[0:1, :] = e1
    eidx_ref[1:2, :] = e2
    ew_ref[0:1, :] = w1
    ew_ref[1:2, :] = w2


def _merge(t, yf, ys, wo, g1, n2, sc2, sh2, rwh, rwl, rb, n_x_rows, seq):
    nt, d = t.shape
    df = yf.shape[1]
    ne = rb.shape[0]
    nsub = d // LANES
    tm = ROW_TILE
    mrow = functools.partial(_mod_row, tm=tm, n_x_rows=n_x_rows, seq=seq)
    mspec = pl.BlockSpec((1, 1, d), lambda i: (mrow(i), 0, 0))
    return pl.pallas_call(
        _merge_kernel,
        grid=(nt // tm,),
        in_specs=[pl.BlockSpec((tm, d), lambda i: (i, 0)),
                  pl.BlockSpec((tm, df), lambda i: (i, 0)),
                  pl.BlockSpec((tm, d - df), lambda i: (i, 0)),
                  pl.BlockSpec((d, d), lambda i: (0, 0)),
                  mspec,
                  pl.BlockSpec((1, d), lambda i: (0, 0)),
                  mspec, mspec,
                  pl.BlockSpec((d, LANES), lambda i: (0, 0)),
                  pl.BlockSpec((d, LANES), lambda i: (0, 0)),
                  pl.BlockSpec((ne, 1), lambda i: (0, 0))],
        out_specs=[pl.BlockSpec((tm, d), lambda i: (i, 0)),
                   pl.BlockSpec((tm * GATHER_PITCH, LANES), lambda i: (i, 0)),
                   pl.BlockSpec((8, tm), lambda i: (0, i)),
                   pl.BlockSpec((8, tm), lambda i: (0, i))],
        out_shape=[jax.ShapeDtypeStruct((nt, d), F32),
                   jax.ShapeDtypeStruct((nt * GATHER_PITCH, LANES), F32),
                   jax.ShapeDtypeStruct((8, nt), jnp.int32),
                   jax.ShapeDtypeStruct((8, nt), F32)],
        compiler_params=_cp(("parallel",)),
        name="merge_route",
    )(t, yf, ys, wo, g1, n2, sc2, sh2, rwh, rwl, rb)


def _dispatch(eidx, ew, n_experts, rows, n_blocks):
    nt = eidx.shape[1]
    a = nt * TOP_K
    flat_e = eidx[:TOP_K].T.reshape(a)
    onehot = (flat_e[:, None] == jnp.arange(n_experts, dtype=jnp.int32)[None, :]).astype(jnp.int32)
    csum = jnp.cumsum(onehot, axis=0)
    counts = csum[-1]
    padded = (counts + rows - 1) // rows * rows
    pad_end = jnp.cumsum(padded)
    pad_start = pad_end - padded
    dest = jnp.sum(onehot * (csum - 1 + pad_start[None, :]), axis=1)
    tok = jnp.arange(a, dtype=jnp.int32) // TOP_K
    buf_tok = jnp.zeros((n_blocks * rows,), jnp.int32).at[dest].set(tok)
    n_valid = (pad_end[-1] // rows).astype(jnp.int32)
    blk_start = jnp.arange(n_blocks, dtype=jnp.int32) * rows
    blk_exp = jnp.sum((pad_end[None, :] <= blk_start[:, None]).astype(jnp.int32), axis=1)
    blk_exp = jnp.minimum(blk_exp, n_experts - 1)
    last_exp = jnp.sum(jnp.where(jnp.arange(n_blocks) == n_valid - 1, blk_exp, 0))
    blk_exp = jnp.where(jnp.arange(n_blocks) < n_valid, blk_exp, last_exp).astype(jnp.int32)
    return buf_tok, blk_exp, n_valid.reshape(1), dest.reshape(nt, TOP_K), ew[:TOP_K].T


def _store_token_rows(o_ref, x):
    rows, d = x.shape
    nsub = d // LANES
    for c in range(nsub):
        o_ref[pl.ds(c, rows, stride=GATHER_PITCH), :] = x[:, c * LANES:(c + 1) * LANES]
    for c in range(nsub, GATHER_PITCH):
        o_ref[pl.ds(c, rows, stride=GATHER_PITCH), :] = jnp.zeros((rows, LANES), o_ref.dtype)


def _start_row_copies(n_rows, src_hbm, src_row, dst, sem, nsub):
    def body(q, c):
        for k in range(ROW_COPY_UNROLL):
            r = q * ROW_COPY_UNROLL + k
            src = src_hbm.at[pl.ds(pl.multiple_of(src_row(r) * GATHER_PITCH, 4), nsub)]
            pltpu.make_async_copy(src, dst.at[pl.ds(pl.multiple_of(r * GATHER_PITCH, 4), nsub)], sem).start()
        return c

    lax.fori_loop(0, n_rows // ROW_COPY_UNROLL, body, 0)


def _wait_row_copies(n_rows, src_hbm, dst, sem, nsub):
    pltpu.make_async_copy(src_hbm.at[pl.ds(0, n_rows * nsub)], dst.at[pl.ds(0, n_rows * nsub)], sem).wait()


def _expert_kernel(be_ref, nv_ref, tok_ref, tokn_ref, h_hbm, wg_hbm, wu_hbm, wd_hbm, o_ref,
                   wgb_ref, wub_ref, wdb_ref, stg_ref, std_ref, xg_ref, xb_ref, sem, wsem, *, nsub, layer):
    i = pl.program_id(0)
    rows = xb_ref.shape[0]
    valid = i < nv_ref[0]
    slot = i % 2
    e = be_ref[i]
    first = (i == 0) | (be_ref[jnp.maximum(i - 1, 0)] != e)

    def gather(toks, s):
        _start_row_copies(rows, h_hbm, lambda r: toks[0, r], xg_ref.at[s], sem.at[s], nsub)

    @pl.when(i == 0)
    def _():
        gather(tok_ref, 0)

    next_valid = (i + 1 < nv_ref[0]) & (i + 1 < pl.num_programs(0))
    for s in range(2):
        @pl.when(next_valid & (slot == 1 - s))
        def _():
            gather(tokn_ref, s)

    @pl.when(valid & first)
    def _():
        chunks = []
        for src, dst, stg in ((wg_hbm, wgb_ref, stg_ref), (wu_hbm, wub_ref, stg_ref), (wd_hbm, wdb_ref, std_ref)):
            nr = stg.shape[1]
            chunks += [(src, dst, stg, k * nr, nr) for k in range(dst.shape[0] // nr)]

        def copy(k):
            src, _, stg, r0, nr = chunks[k]
            return pltpu.make_async_copy(src.at[layer, e, pl.ds(r0, nr), :], stg.at[k % 2], wsem.at[k % 2])

        copy(0).start()
        for k, (_, dst, stg, r0, nr) in enumerate(chunks):
            if k + 1 < len(chunks):
                copy(k + 1).start()
            copy(k).wait()
            dst[r0:r0 + nr, :] = stg[k % 2].astype(BF16)

    @pl.when(valid)
    def _():
        for s in range(2):
            @pl.when(slot == s)
            def _():
                _wait_row_copies(rows, h_hbm, xg_ref.at[s], sem.at[s], nsub)
                for c in range(nsub):
                    xb_ref[:, c * LANES:(c + 1) * LANES] = (
                        xg_ref[s, pl.ds(c, rows, stride=GATHER_PITCH), :].astype(BF16))
        x = xb_ref[...]
        g = jnp.dot(x, wgb_ref[...], preferred_element_type=F32)
        u = jnp.dot(x, wub_ref[...], preferred_element_type=F32)
        hmid = (g * jax.nn.sigmoid(g)) * u
        y = jnp.dot(hmid.astype(BF16), wdb_ref[...], preferred_element_type=F32)
        _store_token_rows(o_ref, y)

    @pl.when(jnp.logical_not(valid))
    def _():
        o_ref[...] = jnp.zeros_like(o_ref)


def _experts(h3, buf_tok, blk_exp, n_valid, w_gate, w_up, w_down, layer):
    _, ne, d, de = w_gate.shape
    nsub = d // LANES
    rows = MOE_ROWS
    n_blocks = buf_tok.shape[0] // rows
    any_spec = pl.BlockSpec(memory_space=pl.ANY)
    grid_spec = pltpu.PrefetchScalarGridSpec(
        num_scalar_prefetch=2,
        grid=(n_blocks,),
        in_specs=[pl.BlockSpec((None, 1, rows), lambda i, be, nv: (i, 0, 0), memory_space=pltpu.SMEM),
                  pl.BlockSpec((None, 1, rows), lambda i, be, nv: (jnp.minimum(i + 1, n_blocks - 1), 0, 0),
                               memory_space=pltpu.SMEM),
                  any_spec, any_spec, any_spec, any_spec],
        out_specs=pl.BlockSpec((rows * GATHER_PITCH, LANES), lambda i, be, nv: (i, 0)),
        scratch_shapes=[pltpu.VMEM((d, de), BF16),
                        pltpu.VMEM((d, de), BF16),
                        pltpu.VMEM((de, d), BF16),
                        pltpu.VMEM((2, d // MOE_WCHUNKS, de), F32),
                        pltpu.VMEM((2, de // MOE_WCHUNKS, d), F32),
                        pltpu.VMEM((2, rows * GATHER_PITCH, LANES), F32),
                        pltpu.VMEM((rows, d), BF16),
                        pltpu.SemaphoreType.DMA((2,)),
                        pltpu.SemaphoreType.DMA((2,))],
    )
    toks = buf_tok.reshape(n_blocks, 1, rows)
    return pl.pallas_call(
        functools.partial(_expert_kernel, nsub=nsub, layer=layer),
        grid_spec=grid_spec,
        out_shape=jax.ShapeDtypeStruct((n_blocks * rows * GATHER_PITCH, LANES), F32),
        compiler_params=_cp(("arbitrary",)),
        name="experts",
    )(blk_exp, n_valid, toks, toks, h3, w_gate, w_up, w_down)


def _combine_kernel(pos_ref, posn_ref, t_ref, w_ref, yb_hbm, g2_ref, fg_ref, o_ref, gk_ref, sem, *, nsub,
                    final_norm):
    i = pl.program_id(0)
    tm = t_ref.shape[0]
    slot = i % 2

    def gather(pos, s):
        for k in range(TOP_K):
            _start_row_copies(tm, yb_hbm, lambda r, k=k: pos[0, TOP_K * r + k], gk_ref.at[s, k], sem.at[s], nsub)

    @pl.when(i == 0)
    def _():
        gather(pos_ref, 0)

    for s in range(2):
        @pl.when((i + 1 < pl.num_programs(0)) & (slot == 1 - s))
        def _():
            gather(posn_ref, s)

    w0 = w_ref[:, 0:1]
    w1 = w_ref[:, 1:2]
    for s in range(2):
        @pl.when(slot == s)
        def _():
            for k in range(TOP_K):
                _wait_row_copies(tm, yb_hbm, gk_ref.at[s, k], sem.at[s], nsub)
            for c in range(nsub):
                sl = slice(c * LANES, (c + 1) * LANES)
                y = (gk_ref[s, 0, pl.ds(c, tm, stride=GATHER_PITCH), :] * w0
                     + gk_ref[s, 1, pl.ds(c, tm, stride=GATHER_PITCH), :] * w1)
                o_ref[:, sl] = t_ref[:, sl] + g2_ref[0][:, sl] * y

    if final_norm:
        x = o_ref[...]
        o_ref[...] = x * lax.rsqrt(jnp.mean(x * x, axis=-1, keepdims=True) + EPS) * fg_ref[...]


def _combine(t, yb3, pos, w, g2, final_g, n_x_rows, seq, n_out_rows, final_norm):
    nt, d = t.shape
    nsub = d // LANES
    tm = ROW_TILE
    mrow = functools.partial(_mod_row, tm=tm, n_x_rows=n_x_rows, seq=seq)
    nb = n_out_rows // tm
    posb = pos.reshape(nt // tm, 1, TOP_K * tm)
    return pl.pallas_call(
        functools.partial(_combine_kernel, nsub=nsub, final_norm=final_norm),
        grid=(nb,),
        in_specs=[pl.BlockSpec((None, 1, TOP_K * tm), lambda i: (i, 0, 0), memory_space=pltpu.SMEM),
                  pl.BlockSpec((None, 1, TOP_K * tm), lambda i: (jnp.minimum(i + 1, nb - 1), 0, 0),
                               memory_space=pltpu.SMEM),
                  pl.BlockSpec((tm, d), lambda i: (i, 0)),
                  pl.BlockSpec((tm, TOP_K), lambda i: (i, 0)),
                  pl.BlockSpec(memory_space=pl.ANY),
                  pl.BlockSpec((1, 1, d), lambda i: (mrow(i), 0, 0)),
                  pl.BlockSpec((1, d), lambda i: (0, 0))],
        out_specs=pl.BlockSpec((tm, d), lambda i: (i, 0)),
        out_shape=jax.ShapeDtypeStruct((n_out_rows, d), F32),
        scratch_shapes=[pltpu.VMEM((2, TOP_K, tm * GATHER_PITCH, LANES), F32),
                        pltpu.SemaphoreType.DMA((2,))],
        compiler_params=_cp(("arbitrary",)),
        name="combine",
    )(posb, posb, t, w, yb3, g2, final_g)


def kernel(x, c, ctx, c_ctx, w_mod, b_mod, norm1_g, norm2_g, w_in, w_out, fourier_w, mix_norm_g,
           lam_re, lam_im, log_dt, b_re, b_im, c_re, c_im, d_skip, glu_w, glu_b,
           router_w, router_b, w_gate, w_up, w_down, final_g):
    bsz, seq, d = x.shape
    n_ctx = ctx.shape[1]
    depth = w_mod.shape[0]
    df = fourier_w.shape[1] * fourier_w.shape[2]
    ds = d_skip.shape[1]
    ne = router_w.shape[1]
    nx_rows = bsz * seq
    nt = nx_rows + bsz * n_ctx

    t = jnp.concatenate([x.reshape(nx_rows, d), ctx.reshape(bsz * n_ctx, d)], axis=0).astype(F32)

    cvec = jnp.concatenate([c_ctx[None, :], c, jnp.zeros((8 - 1 - bsz, d), c.dtype)], axis=0).astype(F32)
    mod = _adaln(cvec, w_mod, b_mod).reshape(depth, 8, N_MOD, 1, d)

    cc, sc_ = _dft_tables(df)
    w_chan = jnp.concatenate([cc, sc_], axis=1).astype(BF16)
    tabs_x = _half_tables(seq)
    tabs_c = _half_tables(n_ctx)
    s5_tabs = jax.vmap(_s5_tables)(lam_re, lam_im, log_dt, b_re, b_im, c_re, c_im, d_skip)

    rw = jnp.pad(router_w.astype(F32), ((0, 0), (0, LANES - ne)))
    rwh = rw.astype(BF16)
    rwl = (rw - rwh.astype(F32)).astype(BF16)
    rb = router_b.astype(F32).reshape(ne, 1)
    fg = final_g.reshape(1, d).astype(F32)
    n_blocks = -(-(nt * TOP_K) // MOE_ROWS) + ne

    for l in range(depth):
        sh1, sc1, g1, sh2, sc2, g2 = [mod[l, :, k] for k in range(N_MOD)]
        zf, zs = _inproj(t, norm1_g[l].reshape(1, d).astype(F32), sc1, sh1, w_in[l].astype(BF16),
                         df, nx_rows, seq)

        hw = fourier_w[l].astype(BF16)
        gain = mix_norm_g[l].astype(F32).reshape(1, -1)
        yf = _fourier_mix(zf, w_chan, tabs_x, hw, gain[:, :df], 0, bsz, seq, jnp.zeros((nt, df), BF16))
        yf = _fourier_mix(zf, w_chan, tabs_c, hw, gain[:, :df], nx_rows, bsz, n_ctx, yf)

        tables = [tab[l] for tab in s5_tabs]
        ysx, ysc = _s5_scan(zs, tables, bsz, seq, n_ctx)
        ys = _glu(ysx, ysc, glu_w[l].astype(BF16), glu_b[l].astype(F32).reshape(1, ds), gain[:, df:])

        t, h3, eidx, ew = _merge(t, yf, ys, w_out[l].astype(BF16), g1,
                                 norm2_g[l].reshape(1, d).astype(F32), sc2, sh2, rwh, rwl, rb, nx_rows, seq)

        buf_tok, blk_exp, n_valid, pos, wtok = _dispatch(eidx, ew, ne, MOE_ROWS, n_blocks)
        yb = _experts(h3, buf_tok, blk_exp, n_valid, w_gate, w_up, w_down, l)
        last = l == depth - 1
        t = _combine(t, yb, pos, wtok, g2, fg, nx_rows, seq, nx_rows if last else nt, last)

    return t.reshape(bsz, seq, d).astype(x.dtype)
```

```python
import functools
import math

import jax
import jax.numpy as jnp
from jax import lax
from jax.experimental import pallas as pl
from jax.experimental.pallas import tpu as pltpu

F32 = jnp.float32
BF16 = jnp.bfloat16
EPS = 1e-6

FOURIER_HEADS = 4
SSM_GROUP = 16
N_EXPERT_GROUPS = 4
TOP_K = 2
N_MOD = 6

SSM_CHUNK = 16
ROW_TILE = 256
MOE_ROWS = 256
MOE_WCHUNKS = 8
ROW_COPY_UNROLL = 8
LANES = 128
GATHER_PITCH = 20
S5_SET = LANES
VMEM_LIMIT_BYTES = 56 * 1024 * 1024


def _cp(sems):
    return pltpu.CompilerParams(dimension_semantics=sems, vmem_limit_bytes=VMEM_LIMIT_BYTES)


def _mod_row(i, tm, n_x_rows, seq):
    r = i * tm
    return jnp.where(r < n_x_rows, 1 + r // seq, 0)


def _mod_kernel(c_ref, w_ref, b_ref, o_ref):
    c = c_ref[...]
    s = c * jax.nn.sigmoid(c)
    o_ref[...] = jnp.dot(s.astype(BF16), w_ref[...].astype(BF16),
                         preferred_element_type=F32) + b_ref[...]


def _adaln(cvec, w_mod, b_mod):
    depth, d, n = w_mod.shape
    tn = 1024
    return pl.pallas_call(
        _mod_kernel,
        grid=(depth, n // tn),
        in_specs=[pl.BlockSpec((8, d), lambda l, j: (0, 0)),
                  pl.BlockSpec((None, d, tn), lambda l, j: (l, 0, j)),
                  pl.BlockSpec((None, 1, tn), lambda l, j: (l, 0, j))],
        out_specs=pl.BlockSpec((None, 8, tn), lambda l, j: (l, 0, j)),
        out_shape=jax.ShapeDtypeStruct((depth, 8, n), F32),
        compiler_params=_cp(("parallel", "parallel")),
        name="adaln",
    )(cvec, w_mod, b_mod.reshape(depth, 1, n))


def _inproj_kernel(t_ref, g_ref, sc_ref, sh_ref, w_ref, zf_ref, zs_ref):
    x = t_ref[...]
    h = x * lax.rsqrt(jnp.mean(x * x, axis=-1, keepdims=True) + EPS) * g_ref[...]
    h = h * (1.0 + sc_ref[0]) + sh_ref[0]
    z = jnp.dot(h.astype(BF16), w_ref[...], preferred_element_type=F32)
    df = zf_ref.shape[1]
    zf_ref[...] = z[:, :df].astype(zf_ref.dtype)
    zs_ref[...] = z[:, df:]


def _inproj(t, g, sc, sh, w, df, n_x_rows, seq):
    nt, d = t.shape
    n = w.shape[1]
    tm = ROW_TILE
    mrow = functools.partial(_mod_row, tm=tm, n_x_rows=n_x_rows, seq=seq)
    return pl.pallas_call(
        _inproj_kernel,
        grid=(nt // tm,),
        in_specs=[pl.BlockSpec((tm, d), lambda i: (i, 0)),
                  pl.BlockSpec((1, d), lambda i: (0, 0)),
                  pl.BlockSpec((1, 1, d), lambda i: (mrow(i), 0, 0)),
                  pl.BlockSpec((1, 1, d), lambda i: (mrow(i), 0, 0)),
                  pl.BlockSpec((d, n), lambda i: (0, 0))],
        out_specs=[pl.BlockSpec((tm, df), lambda i: (i, 0)),
                   pl.BlockSpec((tm, n - df), lambda i: (i, 0))],
        out_shape=[jax.ShapeDtypeStruct((nt, df), BF16),
                   jax.ShapeDtypeStruct((nt, n - df), F32)],
        compiler_params=_cp(("parallel",)),
        name="inproj",
    )(t, g, sc, sh, w)


def _dft_tables(n):
    r = 1
    while r * r < n:
        r *= 2
    q = n // r
    k = jnp.arange(n, dtype=jnp.int32)[:, None]
    step = 2.0 * math.pi / n
    pa = ((k * (jnp.arange(q, dtype=jnp.int32)[None, :] * r)) % n).astype(F32) * step
    pb = ((k * jnp.arange(r, dtype=jnp.int32)[None, :]) % n).astype(F32) * step
    ca, sa = jnp.cos(pa)[:, :, None], jnp.sin(pa)[:, :, None]
    cb, sb = jnp.cos(pb)[:, None, :], jnp.sin(pb)[:, None, :]
    c = (ca * cb - sa * sb).reshape(n, n)
    s = (sa * cb + ca * sb).reshape(n, n)
    return c, s


def _chan_dft_kernel(z_ref, w_ref, o_ref):
    o_ref[...] = jnp.dot(z_ref[...], w_ref[...], preferred_element_type=F32).astype(o_ref.dtype)


def _chan_dft(z, w):
    nt = z.shape[0]
    df, n = w.shape
    tm = min(512, nt)
    return pl.pallas_call(
        _chan_dft_kernel,
        grid=(nt // tm,),
        in_specs=[pl.BlockSpec((tm, df), lambda i: (i, 0)),
                  pl.BlockSpec((df, n), lambda i: (0, 0))],
        out_specs=pl.BlockSpec((tm, n), lambda i: (i, 0)),
        out_shape=jax.ShapeDtypeStruct((nt, n), BF16),
        compiler_params=_cp(("parallel",)),
        name="chan_dft",
    )(z, w)


def _chan_sym_kernel(z_ref, zr_ref, w_ref, o_ref):
    z = z_ref[...].astype(F32)
    zr = zr_ref[...].astype(F32)
    c = w_ref.shape[0]
    o_ref[:, :c] = jnp.dot((z + zr).astype(BF16), w_ref[:, :c], preferred_element_type=F32).astype(o_ref.dtype)
    o_ref[:, c:] = jnp.dot((z - zr).astype(BF16), w_ref[:, c:], preferred_element_type=F32).astype(o_ref.dtype)


def _chan_sym(z, zr, w, row0, bsz, length):
    df, n = w.shape
    half = length // 2
    tm = min(512, half)
    hb = half // tm
    off = row0 // tm
    return pl.pallas_call(
        _chan_sym_kernel,
        grid=(bsz, hb),
        in_specs=[pl.BlockSpec((tm, df), lambda b, i: (off + b * 2 * hb + i, 0)),
                  pl.BlockSpec((tm, df), lambda b, i: (b * hb + i, 0)),
                  pl.BlockSpec((df, n), lambda b, i: (0, 0))],
        out_specs=pl.BlockSpec((tm, n), lambda b, i: (b * hb + i, 0)),
        out_shape=jax.ShapeDtypeStruct((bsz * half, n), BF16),
        compiler_params=_cp(("parallel", "parallel")),
        name="chan_sym",
    )(z, zr, w)


def _fourier_epilogue(f, hw_ref, gain_ref, o_ref):
    nh, hd, _ = hw_ref.shape
    ys = [jnp.dot(f[:, h * hd:(h + 1) * hd].astype(BF16), hw_ref[h],
                  preferred_element_type=F32) for h in range(nh)]
    ssum = jnp.sum(ys[0] * ys[0], axis=-1, keepdims=True)
    for h in range(1, nh):
        ssum = ssum + jnp.sum(ys[h] * ys[h], axis=-1, keepdims=True)
    r = lax.rsqrt(ssum / (nh * hd) + EPS)
    for h in range(nh):
        o_ref[:, h * hd:(h + 1) * hd] = (ys[h] * r * gain_ref[:, h * hd:(h + 1) * hd]).astype(o_ref.dtype)


def _row_sym_kernel(ch_ref, sh_ref, alt_ref, ae_ref, bo_ref, amid_ref, hw_ref, gain_ref, lo_ref, hi_ref,
                    p_ref, q_ref, r_ref, *, scale, mid_sign):
    m = pl.program_id(1)
    k = pl.program_id(2)

    @pl.when(k == 0)
    def _():
        p_ref[...] = jnp.zeros_like(p_ref)
        q_ref[...] = jnp.zeros_like(q_ref)
        r_ref[...] = jnp.zeros_like(r_ref)

    ae = ae_ref[...]
    p_ref[...] += jnp.dot(ch_ref[...], ae, preferred_element_type=F32)
    q_ref[...] += jnp.dot(sh_ref[...], bo_ref[...], preferred_element_type=F32)

    @pl.when(m == 0)
    def _():
        r_ref[...] += jnp.dot(alt_ref[...], ae, preferred_element_type=F32)

    @pl.when(k == pl.num_programs(2) - 1)
    def _():
        tm = p_ref.shape[0]
        row = lax.broadcasted_iota(jnp.int32, (tm, 1), 0) + m * tm
        amid = amid_ref[0:1, :]
        pf = p_ref[...] + jnp.where((row & 1) == 0, 1.0, -1.0) * amid
        q = q_ref[...]
        mid = r_ref[0:1, :] + mid_sign * amid
        hi = jnp.where(row == 0, mid, pf + q)
        _fourier_epilogue((pf - q) * scale, hw_ref, gain_ref, lo_ref)
        _fourier_epilogue(hi * scale, hw_ref, gain_ref, hi_ref)


def _without_alias_ref(kernel, pos):
    def wrapped(*refs):
        return kernel(*refs[:pos], *refs[pos + 1:])
    return wrapped


def _row_sym(aebo, amid, ch, sh, alt, hw, gain, bsz, length, row0, ybuf, n_rows):
    df = aebo.shape[1] // 2
    half = length // 2
    tm = tk = min(512, half)
    mb, kb = half // tm, half // tk
    scale = 1.0 / math.sqrt(length * df)
    mid_sign = 1.0 if half % 2 == 0 else -1.0
    off = row0 // tm
    kern = functools.partial(_row_sym_kernel, scale=scale, mid_sign=mid_sign)
    in_specs = [pl.BlockSpec((tm, tk), lambda b, m, k: (m, k)),
                pl.BlockSpec((tm, tk), lambda b, m, k: (m, k)),
                pl.BlockSpec((8, tk), lambda b, m, k: (0, k)),
                pl.BlockSpec((tk, df), lambda b, m, k: (b * kb + k, 0)),
                pl.BlockSpec((tk, df), lambda b, m, k: (b * kb + k, 1)),
                pl.BlockSpec((8, df), lambda b, m, k: (b, 0)),
                pl.BlockSpec(hw.shape, lambda b, m, k: (0, 0, 0)),
                pl.BlockSpec((1, df), lambda b, m, k: (0, 0))]
    args = [ch, sh, alt, aebo, aebo, amid, hw, gain]
    aliases = {}
    if ybuf is not None:
        kern = _without_alias_ref(kern, len(args))
        in_specs.append(pl.BlockSpec(memory_space=pl.ANY))
        aliases = {len(args): 0}
        args.append(ybuf)
    return pl.pallas_call(
        kern,
        grid=(bsz, mb, kb),
        in_specs=in_specs,
        out_specs=[pl.BlockSpec((tm, df), lambda b, m, k: (off + b * 2 * mb + m, 0)),
                   pl.BlockSpec((tm, df), lambda b, m, k: (b * mb + m, 0))],
        out_shape=[jax.ShapeDtypeStruct((n_rows, df), BF16),
                   jax.ShapeDtypeStruct((bsz * half, df), BF16)],
        scratch_shapes=[pltpu.VMEM((tm, df), F32), pltpu.VMEM((tm, df), F32), pltpu.VMEM((8, df), F32)],
        input_output_aliases=aliases,
        compiler_params=_cp(("parallel", "parallel", "arbitrary")),
        name="row_sym",
    )(*args)


def _mirror_kernel(j_ref, a_ref, b_ref, o_ref, *, keep_first):
    rev = jnp.dot(j_ref[...], a_ref[...], preferred_element_type=F32)
    first = b_ref[0:1, :].astype(F32)
    if not keep_first:
        first = jnp.where(pl.program_id(1) == 0, 0.0, first)
    row = lax.broadcasted_iota(jnp.int32, (o_ref.shape[0], 1), 0)
    o_ref[...] = jnp.where(row == 0, first, rev).astype(o_ref.dtype)


def _mirror(src, src_row0, src_stride, half, bsz, keep_first, dst_row0=0, dst_stride=None, ybuf=None):
    df = src.shape[1]
    tm = min(512, half)
    nb = half // tm
    dst_stride = half if dst_stride is None else dst_stride
    soff, sstr, doff, dstr = src_row0 // tm, src_stride // tm, dst_row0 // tm, dst_stride // tm
    r = jnp.arange(tm, dtype=jnp.int32)
    jmat = ((r[:, None] + r[None, :] == tm) & (r[:, None] > 0)).astype(BF16)
    kern = functools.partial(_mirror_kernel, keep_first=keep_first)
    in_specs = [pl.BlockSpec((tm, tm), lambda b, i: (0, 0)),
                pl.BlockSpec((tm, df), lambda b, i: (soff + b * sstr + nb - 1 - i, 0)),
                pl.BlockSpec((tm, df), lambda b, i: (soff + b * sstr + (nb - i) % nb, 0))]
    args = [jmat, src, src]
    aliases = {}
    n_rows = bsz * half
    if ybuf is not None:
        kern = _without_alias_ref(kern, len(args))
        in_specs.append(pl.BlockSpec(memory_space=pl.ANY))
        aliases = {len(args): 0}
        args.append(ybuf)
        n_rows = ybuf.shape[0]
    return pl.pallas_call(
        kern,
        grid=(bsz, nb),
        in_specs=in_specs,
        out_specs=pl.BlockSpec((tm, df), lambda b, i: (doff + b * dstr + i, 0)),
        out_shape=jax.ShapeDtypeStruct((n_rows, df), BF16),
        input_output_aliases=aliases,
        compiler_params=_cp(("parallel", "parallel")),
        name="mirror",
    )(*args)


def _fourier_mix(zf, w_chan, tabs, hw, gain, row0, bsz, length, ybuf):
    ch, sh, alt = tabs
    df = zf.shape[1]
    half = length // 2
    zr = _mirror(zf, row0 + half, length, half, bsz, keep_first=False)
    aebo = _chan_sym(zf, zr, w_chan, row0, bsz, length)
    zmid = zf[row0:row0 + bsz * length].reshape(bsz, length, df)[:, half]
    zmid = jnp.pad(zmid[:, None, :], ((0, 0), (0, 7), (0, 0)))
    amid = _chan_dft(zmid.reshape(bsz * 8, df), w_chan)[:, :df].astype(F32)
    y, hi = _row_sym(aebo, amid, ch, sh, alt, hw, gain, bsz, length, row0, ybuf, zf.shape[0])
    return _mirror(hi, 0, half, half, bsz, keep_first=True, dst_row0=row0 + half, dst_stride=length, ybuf=y)


def _half_tables(length):
    half = length // 2
    c, s = _dft_tables(length)
    n = jnp.arange(half, dtype=jnp.int32)
    alt = jnp.where((lax.broadcasted_iota(jnp.int32, (8, half), 0) == 0),
                    jnp.where(n % 2 == 0, 1.0, -1.0)[None, :], 0.0)
    return c[:half, :half].astype(BF16), s[:half, :half].astype(BF16), alt.astype(BF16)


def _s5_tables(lam_re, lam_im, log_dt, b_re, b_im, c_re, c_im, d_skip):
    tc = SSM_CHUNK
    g, p = lam_re.shape[1:]
    hch = b_re.shape[-1]
    lr, li = lam_re.astype(F32), lam_im.astype(F32)
    dt = jnp.exp(log_dt.astype(F32))[..., None]
    er, ei = lr * dt, li * dt

    def lpow(k):
        m = jnp.exp(er * k)
        return m * jnp.cos(ei * k), m * jnp.sin(ei * k)

    l1r, l1i = lpow(1.0)
    den = lr * lr + li * li
    qr = ((l1r - 1.0) * lr + l1i * li) / den
    qi = (l1i * lr - (l1r - 1.0) * li) / den
    br, bi = b_re.astype(F32), b_im.astype(F32)
    bbr = qr[..., None] * br - qi[..., None] * bi
    bbi = qr[..., None] * bi + qi[..., None] * br
    cr, ci = c_re.astype(F32), c_im.astype(F32)

    ks = jnp.arange(tc + 1, dtype=F32)
    pwr = jnp.stack([lpow(k)[0] for k in range(tc + 1)], axis=-1)
    pwi = jnp.stack([lpow(k)[1] for k in range(tc + 1)], axis=-1)
    del ks

    cbr = jnp.einsum("dgip,dgpj->dgpij", cr, bbr) - jnp.einsum("dgip,dgpj->dgpij", ci, bbi)
    cbi = jnp.einsum("dgip,dgpj->dgpij", cr, bbi) + jnp.einsum("dgip,dgpj->dgpij", ci, bbr)
    klag = (jnp.einsum("dgpij,dgpl->dgjli", cbr, pwr[..., :tc])
            - jnp.einsum("dgpij,dgpl->dgjli", cbi, pwi[..., :tc]))

    dg = d_skip.astype(F32).reshape(g, hch)
    lag0 = klag[0][:, :, 0] + klag[1][:, :, 0] + dg[:, :, None] * jnp.eye(hch, dtype=F32)[None]
    seq = jnp.concatenate([jnp.flip(klag[1][:, :, 1:], axis=2), lag0[:, :, None, :], klag[0][:, :, 1:]], axis=2)
    seq = seq.reshape(g, hch, (2 * tc - 1) * hch)
    m = jnp.stack([seq[:, :, (tc - 1 - t) * hch:(2 * tc - 1 - t) * hch] for t in range(tc)], axis=1)
    m = m.reshape(g, tc * hch, tc * hch)

    idx_f = jnp.arange(tc - 1, -1, -1)
    idx_b = jnp.arange(tc)

    def st(d, idx):
        wr = pwr[d][..., idx][:, :, :, None] * bbr[d][:, :, None, :] - pwi[d][..., idx][:, :, :, None] * bbi[d][:, :, None, :]
        wi = pwr[d][..., idx][:, :, :, None] * bbi[d][:, :, None, :] + pwi[d][..., idx][:, :, :, None] * bbr[d][:, :, None, :]
        return (wr.transpose(0, 2, 3, 1).reshape(g, tc * hch, p),
                wi.transpose(0, 2, 3, 1).reshape(g, tc * hch, p))

    sfr, sfi = st(0, idx_f)
    sbr, sbi = st(1, idx_b)
    w_st = jnp.concatenate([sfr, sbr, sfi, sbi], axis=-1)

    def so(d, idx):
        wr = cr[d][:, None, :, :] * pwr[d][..., idx].transpose(0, 2, 1)[:, :, None, :] \
            - ci[d][:, None, :, :] * pwi[d][..., idx].transpose(0, 2, 1)[:, :, None, :]
        wi = cr[d][:, None, :, :] * pwi[d][..., idx].transpose(0, 2, 1)[:, :, None, :] \
            + ci[d][:, None, :, :] * pwr[d][..., idx].transpose(0, 2, 1)[:, :, None, :]
        return (wr.reshape(g, tc * hch, p).transpose(0, 2, 1),
                wi.reshape(g, tc * hch, p).transpose(0, 2, 1))

    ofr, ofi = so(0, jnp.arange(1, tc + 1))
    obr, obi = so(1, jnp.arange(tc, 0, -1))
    zero = jnp.zeros_like(ofr)
    w_of = jnp.concatenate([ofr, zero, -ofi, zero], axis=1)
    w_ob = jnp.concatenate([zero, obr, zero, -obi], axis=1)

    a_re = jnp.concatenate([pwr[0][..., tc], pwr[1][..., tc]], axis=-1)
    a_im = jnp.concatenate([pwi[0][..., tc], pwi[1][..., tc]], axis=-1)

    gs = S5_SET // hch
    ns = g // gs
    kd = tc * hch
    w_o = w_of + w_ob
    return (m.astype(BF16).reshape(ns, gs, kd, kd),
            w_st.astype(BF16).reshape(ns, gs, kd, 4 * p),
            w_o.astype(BF16).reshape(ns, gs, 4 * p, kd),
            a_re.reshape(ns, gs, 2 * p), a_im.reshape(ns, gs, 2 * p))


def _chunk_perm(tc, gs, hch):
    n = tc * gs * hch
    r = jnp.arange(n, dtype=jnp.int32)
    t, g, j = r // (gs * hch), (r // hch) % gs, r % hch
    dst = g * (tc * hch) + t * hch + j
    fwd = (dst[:, None] == r[None, :]).astype(BF16)
    return fwd, fwd.T


def _s5_kernel(zx_ref, zc_ref, pf_ref, pb_ref, m_ref, wst_ref, wo_ref, are_ref, aim_ref, ox_ref, oc_ref,
               sh_ref, xb_ref, yb_ref, *, ctx_chunks, x_chunks, pitch):
    tc = SSM_CHUNK
    nch = ctx_chunks + x_chunks
    lanes = zx_ref.shape[1]
    gs, kd, nst = wst_ref.shape
    for t in range(tc):
        xb_ref[0:ctx_chunks, t * lanes:(t + 1) * lanes] = zc_ref[pl.ds(t, ctx_chunks, stride=tc), :].astype(BF16)
        xb_ref[ctx_chunks:nch, t * lanes:(t + 1) * lanes] = zx_ref[pl.ds(t, x_chunks, stride=tc), :].astype(BF16)
    xb_ref[...] = jnp.dot(xb_ref[...], pf_ref[...], preferred_element_type=F32).astype(BF16)

    nslab = gs * nst // lanes
    half = nslab // 2
    for j in range(gs):
        s = jnp.dot(xb_ref[:, j * kd:(j + 1) * kd], wst_ref[j], preferred_element_type=F32)
        sh_ref[j * pitch:j * pitch + nch, :] = s[:, 0:lanes]
        sh_ref[(half + j) * pitch:(half + j) * pitch + nch, :] = s[:, lanes:2 * lanes]

    fwd = lax.broadcasted_iota(jnp.int32, (half, lanes), 1) < (lanes // 2)
    a_re = are_ref[...]
    a_im = aim_ref[...]

    def rows(c, part):
        return pl.ds(part * half * pitch + c, half, stride=pitch)

    def step(i, carry):
        hr, hi = carry
        cf = i
        cb = jnp.where(i < ctx_chunks, ctx_chunks - 1 - i, nch - 1 + ctx_chunks - i)
        same = cf == cb
        sfr, sfi = sh_ref[rows(cf, 0), :], sh_ref[rows(cf, 1), :]
        sbr, sbi = sh_ref[rows(cb, 0), :], sh_ref[rows(cb, 1), :]
        sh_ref[rows(cf, 0), :] = jnp.where(fwd, hr, sfr)
        sh_ref[rows(cf, 1), :] = jnp.where(fwd, hi, sfi)
        sh_ref[rows(cb, 0), :] = jnp.where(fwd, jnp.where(same, hr, sbr), hr)
        sh_ref[rows(cb, 1), :] = jnp.where(fwd, jnp.where(same, hi, sbi), hi)
        sr = jnp.where(fwd, sfr, sbr)
        si = jnp.where(fwd, sfi, sbi)
        return a_re * hr - a_im * hi + sr, a_re * hi + a_im * hr + si

    zero = jnp.zeros((half, lanes), F32)
    lax.fori_loop(0, nch, step, (zero, zero))

    for j in range(gs):
        hb = jnp.concatenate([sh_ref[j * pitch:j * pitch + nch, :],
                              sh_ref[(half + j) * pitch:(half + j) * pitch + nch, :]], axis=1).astype(BF16)
        yj = (jnp.dot(xb_ref[:, j * kd:(j + 1) * kd], m_ref[j], preferred_element_type=F32)
              + jnp.dot(hb, wo_ref[j], preferred_element_type=F32))
        yb_ref[:, j * kd:(j + 1) * kd] = yj.astype(BF16)
    y = jnp.dot(yb_ref[...], pb_ref[...], preferred_element_type=F32)
    for t in range(tc):
        oc_ref[pl.ds(t, ctx_chunks, stride=tc), :] = y[0:ctx_chunks, t * lanes:(t + 1) * lanes]
        ox_ref[pl.ds(t, x_chunks, stride=tc), :] = y[ctx_chunks:nch, t * lanes:(t + 1) * lanes]


def _s5_scan(zs, tables, bsz, seq, n_ctx):
    m, w_st, w_o, a_re, a_im = tables
    ns, gs, kd, nst = w_st.shape
    tc = SSM_CHUNK
    lanes = S5_SET
    kset = gs * kd
    ctx_chunks, x_chunks = n_ctx // tc, seq // tc
    nch = ctx_chunks + x_chunks
    pitch = -(-nch // 8) * 8
    if (pitch // 8) % 2 == 0:
        pitch += 8
    ctx0 = bsz * seq // n_ctx
    perm_f, perm_b = _chunk_perm(tc, gs, kd // tc)
    kern = functools.partial(_s5_kernel, ctx_chunks=ctx_chunks, x_chunks=x_chunks, pitch=pitch)
    per_set = lambda arr: pl.BlockSpec((None,) + arr.shape[1:], lambda s, b: (s,) + (0,) * (arr.ndim - 1))
    return pl.pallas_call(
        kern,
        grid=(ns, bsz),
        in_specs=[pl.BlockSpec((seq, lanes), lambda s, b: (b, s)),
                  pl.BlockSpec((n_ctx, lanes), lambda s, b: (ctx0 + b, s)),
                  pl.BlockSpec((kset, kset), lambda s, b: (0, 0)),
                  pl.BlockSpec((kset, kset), lambda s, b: (0, 0)),
                  per_set(m), per_set(w_st), per_set(w_o), per_set(a_re), per_set(a_im)],
        out_specs=[pl.BlockSpec((seq, lanes), lambda s, b: (b, s)),
                   pl.BlockSpec((n_ctx, lanes), lambda s, b: (b, s))],
        out_shape=[jax.ShapeDtypeStruct((bsz * seq, zs.shape[1]), F32),
                   jax.ShapeDtypeStruct((bsz * n_ctx, zs.shape[1]), F32)],
        scratch_shapes=[pltpu.VMEM((gs * nst // lanes * pitch, lanes), F32),
                        pltpu.VMEM((nch, kset), BF16),
                        pltpu.VMEM((nch, kset), BF16)],
        compiler_params=_cp(("parallel", "parallel")),
        name="s5_scan",
    )(zs, zs, perm_f, perm_b, m, w_st, w_o, a_re, a_im)


def _glu_kernel(yx_ref, yc_ref, w_ref, b_ref, gain_ref, o_ref, *, nx_tiles):
    y = jnp.where(pl.program_id(0) < nx_tiles, yx_ref[...], yc_ref[...])
    g = jax.nn.gelu(y)
    v = g * jax.nn.sigmoid(jnp.dot(g.astype(BF16), w_ref[...], preferred_element_type=F32) + b_ref[...])
    r = lax.rsqrt(jnp.mean(v * v, axis=-1, keepdims=True) + EPS)
    o_ref[...] = (v * r * gain_ref[...]).astype(o_ref.dtype)


def _glu(yx, yc, w, b, gain):
    ds = yx.shape[1]
    tm = 512
    nx_tiles, nc_tiles = yx.shape[0] // tm, yc.shape[0] // tm
    return pl.pallas_call(
        functools.partial(_glu_kernel, nx_tiles=nx_tiles),
        grid=(nx_tiles + nc_tiles,),
        in_specs=[pl.BlockSpec((tm, ds), lambda i: (jnp.minimum(i, nx_tiles - 1), 0)),
                  pl.BlockSpec((tm, ds), lambda i: (jnp.maximum(i - nx_tiles, 0), 0)),
                  pl.BlockSpec((ds, ds), lambda i: (0, 0)),
                  pl.BlockSpec((1, ds), lambda i: (0, 0)),
                  pl.BlockSpec((1, ds), lambda i: (0, 0))],
        out_specs=pl.BlockSpec((tm, ds), lambda i: (i, 0)),
        out_shape=jax.ShapeDtypeStruct((yx.shape[0] + yc.shape[0], ds), BF16),
        compiler_params=_cp(("parallel",)),
        name="glu",
    )(yx, yc, w, b, gain)


def _route(scores, sel, n_groups):
    epg = len(sel) // n_groups
    gscore = []
    for q in range(n_groups):
        v = sel[q * epg:(q + 1) * epg]
        best = None
        for a in range(epg):
            for b in range(a + 1, epg):
                s = v[a] + v[b]
                best = s if best is None else jnp.maximum(best, s)
        gscore.append(best)
    gbest = gscore[0]
    gidx = jnp.zeros(gbest.shape, jnp.int32)
    for q in range(1, n_groups):
        upd = gscore[q] > gbest
        gbest = jnp.where(upd, gscore[q], gbest)
        gidx = jnp.where(upd, q, gidx)
    vin = list(sel[:epg])
    sin = list(scores[:epg])
    for q in range(1, n_groups):
        pick = gidx == q
        for j in range(epg):
            vin[j] = jnp.where(pick, sel[q * epg + j], vin[j])
            sin[j] = jnp.where(pick, scores[q * epg + j], sin[j])
    b1 = vin[0]
    i1 = jnp.zeros(gbest.shape, jnp.int32)
    for j in range(1, epg):
        upd = vin[j] > b1
        b1 = jnp.where(upd, vin[j], b1)
        i1 = jnp.where(upd, j, i1)
    b2 = vin[0]
    i2 = jnp.zeros(gbest.shape, jnp.int32)
    have = jnp.zeros(gbest.shape, jnp.bool_)
    for j in range(epg):
        cand = i1 != j
        upd = cand & (jnp.logical_not(have) | (vin[j] > b2))
        b2 = jnp.where(upd, vin[j], b2)
        i2 = jnp.where(upd, j, i2)
        have = have | cand
    s1 = sin[0]
    s2 = sin[0]
    for j in range(1, epg):
        s1 = jnp.where(i1 == j, sin[j], s1)
        s2 = jnp.where(i2 == j, sin[j], s2)
    tot = s1 + s2
    return (gidx * epg + i1, gidx * epg + i2), (s1 / tot, s2 / tot)


def _merge_kernel(t_ref, yf_ref, ys_ref, wo_ref, g1_ref, n2_ref, sc2_ref, sh2_ref, rwh_ref, rwl_ref, rb_ref,
                  tn_ref, h3_ref, eidx_ref, ew_ref):
    df = yf_ref.shape[1]
    tm, d = t_ref.shape
    o = (jnp.dot(yf_ref[...], wo_ref[0:df, :], preferred_element_type=F32)
         + jnp.dot(ys_ref[...], wo_ref[df:, :], preferred_element_type=F32))
    tn = t_ref[...] + g1_ref[0] * o
    tn_ref[...] = tn
    h2 = tn * lax.rsqrt(jnp.mean(tn * tn, axis=-1, keepdims=True) + EPS) * n2_ref[...]
    h2 = h2 * (1.0 + sc2_ref[0]) + sh2_ref[0]
    _store_token_rows(h3_ref, h2)
    h_hi = h2.astype(BF16)
    h_lo = (h2 - h_hi.astype(F32)).astype(BF16)
    lg = (jnp.dot(h_hi, rwh_ref[...], preferred_element_type=F32)
          + jnp.dot(h_lo, rwh_ref[...], preferred_element_type=F32)
          + jnp.dot(h_hi, rwl_ref[...], preferred_element_type=F32))
    ne = rb_ref.shape[0]
    logits = lg.T[0:ne, :]
    scores = jax.nn.sigmoid(logits)
    sel = scores + rb_ref[...]
    srows = [scores[e:e + 1, :] for e in range(ne)]
    vrows = [sel[e:e + 1, :] for e in range(ne)]
    (e1, e2), (w1, w2) = _route(srows, vrows, N_EXPERT_GROUPS)
    eidx_ref[...] = jnp.zeros_like(eidx_ref)
    ew_ref[...] = jnp.zeros_like(ew_ref)
    eidx_ref[0:1, :] = e1
    eidx_ref[1:2, :] = e2
    ew_ref[0:1, :] = w1
    ew_ref[1:2, :] = w2


def _merge(t, yf, ys, wo, g1, n2, sc2, sh2, rwh, rwl, rb, n_x_rows, seq):
    nt, d = t.shape
    df = yf.shape[1]
    ne = rb.shape[0]
    nsub = d // LANES
    tm = ROW_TILE
    mrow = functools.partial(_mod_row, tm=tm, n_x_rows=n_x_rows, seq=seq)
    mspec = pl.BlockSpec((1, 1, d), lambda i: (mrow(i), 0, 0))
    return pl.pallas_call(
        _merge_kernel,
        grid=(nt // tm,),
        in_specs=[pl.BlockSpec((tm, d), lambda i: (i, 0)),
                  pl.BlockSpec((tm, df), lambda i: (i, 0)),
                  pl.BlockSpec((tm, d - df), lambda i: (i, 0)),
                  pl.BlockSpec((d, d), lambda i: (0, 0)),
                  mspec,
                  pl.BlockSpec((1, d), lambda i: (0, 0)),
                  mspec, mspec,
                  pl.BlockSpec((d, LANES), lambda i: (0, 0)),
                  pl.BlockSpec((d, LANES), lambda i: (0, 0)),
                  pl.BlockSpec((ne, 1), lambda i: (0, 0))],
        out_specs=[pl.BlockSpec((tm, d), lambda i: (i, 0)),
                   pl.BlockSpec((tm * GATHER_PITCH, LANES), lambda i: (i, 0)),
                   pl.BlockSpec((8, tm), lambda i: (0, i)),
                   pl.BlockSpec((8, tm), lambda i: (0, i))],
        out_shape=[jax.ShapeDtypeStruct((nt, d), F32),
                   jax.ShapeDtypeStruct((nt * GATHER_PITCH, LANES), F32),
                   jax.ShapeDtypeStruct((8, nt), jnp.int32),
                   jax.ShapeDtypeStruct((8, nt), F32)],
        compiler_params=_cp(("parallel",)),
        name="merge_route",
    )(t, yf, ys, wo, g1, n2, sc2, sh2, rwh, rwl, rb)


def _dispatch(eidx, ew, n_experts, rows, n_blocks):
    nt = eidx.shape[1]
    a = nt * TOP_K
    flat_e = eidx[:TOP_K].T.reshape(a)
    onehot = (flat_e[:, None] == jnp.arange(n_experts, dtype=jnp.int32)[None, :]).astype(jnp.int32)
    csum = jnp.cumsum(onehot, axis=0)
    counts = csum[-1]
    padded = (counts + rows - 1) // rows * rows
    pad_end = jnp.cumsum(padded)
    pad_start = pad_end - padded
    dest = jnp.sum(onehot * (csum - 1 + pad_start[None, :]), axis=1)
    tok = jnp.arange(a, dtype=jnp.int32) // TOP_K
    buf_tok = jnp.zeros((n_blocks * rows,), jnp.int32).at[dest].set(tok)
    n_valid = (pad_end[-1] // rows).astype(jnp.int32)
    blk_start = jnp.arange(n_blocks, dtype=jnp.int32) * rows
    blk_exp = jnp.sum((pad_end[None, :] <= blk_start[:, None]).astype(jnp.int32), axis=1)
    blk_exp = jnp.minimum(blk_exp, n_experts - 1)
    last_exp = jnp.sum(jnp.where(jnp.arange(n_blocks) == n_valid - 1, blk_exp, 0))
    blk_exp = jnp.where(jnp.arange(n_blocks) < n_valid, blk_exp, last_exp).astype(jnp.int32)
    return buf_tok, blk_exp, n_valid.reshape(1), dest.reshape(nt, TOP_K), ew[:TOP_K].T


def _store_token_rows(o_ref, x):
    rows, d = x.shape
    nsub = d // LANES
    for c in range(nsub):
        o_ref[pl.ds(c, rows, stride=GATHER_PITCH), :] = x[:, c * LANES:(c + 1) * LANES]
    for c in range(nsub, GATHER_PITCH):
        o_ref[pl.ds(c, rows, stride=GATHER_PITCH), :] = jnp.zeros((rows, LANES), o_ref.dtype)


def _start_row_copies(n_rows, src_hbm, src_row, dst, sem, nsub):
    def body(q, c):
        for k in range(ROW_COPY_UNROLL):
            r = q * ROW_COPY_UNROLL + k
            src = src_hbm.at[pl.ds(pl.multiple_of(src_row(r) * GATHER_PITCH, 4), nsub)]
            pltpu.make_async_copy(src, dst.at[pl.ds(pl.multiple_of(r * GATHER_PITCH, 4), nsub)], sem).start()
        return c

    lax.fori_loop(0, n_rows // ROW_COPY_UNROLL, body, 0)


def _wait_row_copies(n_rows, src_hbm, dst, sem, nsub):
    pltpu.make_async_copy(src_hbm.at[pl.ds(0, n_rows * nsub)], dst.at[pl.ds(0, n_rows * nsub)], sem).wait()


def _expert_kernel(be_ref, nv_ref, tok_ref, tokn_ref, h_hbm, wg_hbm, wu_hbm, wd_hbm, o_ref,
                   wgb_ref, wub_ref, wdb_ref, stg_ref, std_ref, xg_ref, xb_ref, sem, wsem, *, nsub, layer):
    i = pl.program_id(0)
    rows = xb_ref.shape[0]
    valid = i < nv_ref[0]
    slot = i % 2
    e = be_ref[i]
    first = (i == 0) | (be_ref[jnp.maximum(i - 1, 0)] != e)

    def gather(toks, s):
        _start_row_copies(rows, h_hbm, lambda r: toks[0, r], xg_ref.at[s], sem.at[s], nsub)

    @pl.when(i == 0)
    def _():
        gather(tok_ref, 0)

    next_valid = (i + 1 < nv_ref[0]) & (i + 1 < pl.num_programs(0))
    for s in range(2):
        @pl.when(next_valid & (slot == 1 - s))
        def _():
            gather(tokn_ref, s)

    @pl.when(valid & first)
    def _():
        chunks = []
        for src, dst, stg in ((wg_hbm, wgb_ref, stg_ref), (wu_hbm, wub_ref, stg_ref), (wd_hbm, wdb_ref, std_ref)):
            nr = stg.shape[1]
            chunks += [(src, dst, stg, k * nr, nr) for k in range(dst.shape[0] // nr)]

        def copy(k):
            src, _, stg, r0, nr = chunks[k]
            return pltpu.make_async_copy(src.at[layer, e, pl.ds(r0, nr), :], stg.at[k % 2], wsem.at[k % 2])

        copy(0).start()
        for k, (_, dst, stg, r0, nr) in enumerate(chunks):
            if k + 1 < len(chunks):
                copy(k + 1).start()
            copy(k).wait()
            dst[r0:r0 + nr, :] = stg[k % 2].astype(BF16)

    @pl.when(valid)
    def _():
        for s in range(2):
            @pl.when(slot == s)
            def _():
                _wait_row_copies(rows, h_hbm, xg_ref.at[s], sem.at[s], nsub)
                for c in range(nsub):
                    xb_ref[:, c * LANES:(c + 1) * LANES] = (
                        xg_ref[s, pl.ds(c, rows, stride=GATHER_PITCH), :].astype(BF16))
        x = xb_ref[...]
        g = jnp.dot(x, wgb_ref[...], preferred_element_type=F32)
        u = jnp.dot(x, wub_ref[...], preferred_element_type=F32)
        hmid = (g * jax.nn.sigmoid(g)) * u
        y = jnp.dot(hmid.astype(BF16), wdb_ref[...], preferred_element_type=F32)
        _store_token_rows(o_ref, y)

    @pl.when(jnp.logical_not(valid))
    def _():
        o_ref[...] = jnp.zeros_like(o_ref)


def _experts(h3, buf_tok, blk_exp, n_valid, w_gate, w_up, w_down, layer):
    _, ne, d, de = w_gate.shape
    nsub = d // LANES
    rows = MOE_ROWS
    n_blocks = buf_tok.shape[0] // rows
    any_spec = pl.BlockSpec(memory_space=pl.ANY)
    grid_spec = pltpu.PrefetchScalarGridSpec(
        num_scalar_prefetch=2,
        grid=(n_blocks,),
        in_specs=[pl.BlockSpec((None, 1, rows), lambda i, be, nv: (i, 0, 0), memory_space=pltpu.SMEM),
                  pl.BlockSpec((None, 1, rows), lambda i, be, nv: (jnp.minimum(i + 1, n_blocks - 1), 0, 0),
                               memory_space=pltpu.SMEM),
                  any_spec, any_spec, any_spec, any_spec],
        out_specs=pl.BlockSpec((rows * GATHER_PITCH, LANES), lambda i, be, nv: (i, 0)),
        scratch_shapes=[pltpu.VMEM((d, de), BF16),
                        pltpu.VMEM((d, de), BF16),
                        pltpu.VMEM((de, d), BF16),
                        pltpu.VMEM((2, d // MOE_WCHUNKS, de), F32),
                        pltpu.VMEM((2, de // MOE_WCHUNKS, d), F32),
                        pltpu.VMEM((2, rows * GATHER_PITCH, LANES), F32),
                        pltpu.VMEM((rows, d), BF16),
                        pltpu.SemaphoreType.DMA((2,)),
                        pltpu.SemaphoreType.DMA((2,))],
    )
    toks = buf_tok.reshape(n_blocks, 1, rows)
    return pl.pallas_call(
        functools.partial(_expert_kernel, nsub=nsub, layer=layer),
        grid_spec=grid_spec,
        out_shape=jax.ShapeDtypeStruct((n_blocks * rows * GATHER_PITCH, LANES), F32),
        compiler_params=_cp(("arbitrary",)),
        name="experts",
    )(blk_exp, n_valid, toks, toks, h3, w_gate, w_up, w_down)


def _combine_kernel(pos_ref, posn_ref, t_ref, w_ref, yb_hbm, g2_ref, fg_ref, o_ref, gk_ref, sem, *, nsub,
                    final_norm):
    i = pl.program_id(0)
    tm = t_ref.shape[0]
    slot = i % 2

    def gather(pos, s):
        for k in range(TOP_K):
            _start_row_copies(tm, yb_hbm, lambda r, k=k: pos[0, TOP_K * r + k], gk_ref.at[s, k], sem.at[s], nsub)

    @pl.when(i == 0)
    def _():
        gather(pos_ref, 0)

    for s in range(2):
        @pl.when((i + 1 < pl.num_programs(0)) & (slot == 1 - s))
        def _():
            gather(posn_ref, s)

    w0 = w_ref[:, 0:1]
    w1 = w_ref[:, 1:2]
    for s in range(2):
        @pl.when(slot == s)
        def _():
            for k in range(TOP_K):
                _wait_row_copies(tm, yb_hbm, gk_ref.at[s, k], sem.at[s], nsub)
            for c in range(nsub):
                sl = slice(c * LANES, (c + 1) * LANES)
                y = (gk_ref[s, 0, pl.ds(c, tm, stride=GATHER_PITCH), :] * w0
                     + gk_ref[s, 1, pl.ds(c, tm, stride=GATHER_PITCH), :] * w1)
                o_ref[:, sl] = t_ref[:, sl] + g2_ref[0][:, sl] * y

    if final_norm:
        x = o_ref[...]
        o_ref[...] = x * lax.rsqrt(jnp.mean(x * x, axis=-1, keepdims=True) + EPS) * fg_ref[...]


def _combine(t, yb3, pos, w, g2, final_g, n_x_rows, seq, n_out_rows, final_norm):
    nt, d = t.shape
    nsub = d // LANES
    tm = ROW_TILE
    mrow = functools.partial(_mod_row, tm=tm, n_x_rows=n_x_rows, seq=seq)
    nb = n_out_rows // tm
    posb = pos.reshape(pos.shape[0] // tm, 1, TOP_K * tm)
    return pl.pallas_call(
        functools.partial(_combine_kernel, nsub=nsub, final_norm=final_norm),
        grid=(nb,),
        in_specs=[pl.BlockSpec((None, 1, TOP_K * tm), lambda i: (i, 0, 0), memory_space=pltpu.SMEM),
                  pl.BlockSpec((None, 1, TOP_K * tm), lambda i: (jnp.minimum(i + 1, nb - 1), 0, 0),
                               memory_space=pltpu.SMEM),
                  pl.BlockSpec((tm, d), lambda i: (i, 0)),
                  pl.BlockSpec((tm, TOP_K), lambda i: (i, 0)),
                  pl.BlockSpec(memory_space=pl.ANY),
                  pl.BlockSpec((1, 1, d), lambda i: (mrow(i), 0, 0)),
                  pl.BlockSpec((1, d), lambda i: (0, 0))],
        out_specs=pl.BlockSpec((tm, d), lambda i: (i, 0)),
        out_shape=jax.ShapeDtypeStruct((n_out_rows, d), F32),
        scratch_shapes=[pltpu.VMEM((2, TOP_K, tm * GATHER_PITCH, LANES), F32),
                        pltpu.SemaphoreType.DMA((2,))],
        compiler_params=_cp(("arbitrary",)),
        name="combine",
    )(posb, posb, t, w, yb3, g2, final_g)


def kernel(x, c, ctx, c_ctx, w_mod, b_mod, norm1_g, norm2_g, w_in, w_out, fourier_w, mix_norm_g,
           lam_re, lam_im, log_dt, b_re, b_im, c_re, c_im, d_skip, glu_w, glu_b,
           router_w, router_b, w_gate, w_up, w_down, final_g):
    bsz, seq, d = x.shape
    n_ctx = ctx.shape[1]
    depth = w_mod.shape[0]
    df = fourier_w.shape[1] * fourier_w.shape[2]
    ds = d_skip.shape[1]
    ne = router_w.shape[1]
    nx_rows = bsz * seq
    nt = nx_rows + bsz * n_ctx

    t = jnp.concatenate([x.reshape(nx_rows, d), ctx.reshape(bsz * n_ctx, d)], axis=0).astype(F32)

    cvec = jnp.concatenate([c_ctx[None, :], c, jnp.zeros((8 - 1 - bsz, d), c.dtype)], axis=0).astype(F32)
    mod = _adaln(cvec, w_mod, b_mod).reshape(depth, 8, N_MOD, 1, d)

    cc, sc_ = _dft_tables(df)
    w_chan = jnp.concatenate([cc, sc_], axis=1).astype(BF16)
    tabs_x = _half_tables(seq)
    tabs_c = _half_tables(n_ctx)
    s5_tabs = jax.vmap(_s5_tables)(lam_re, lam_im, log_dt, b_re, b_im, c_re, c_im, d_skip)

    rw = jnp.pad(router_w.astype(F32), ((0, 0), (0, LANES - ne)))
    rwh = rw.astype(BF16)
    rwl = (rw - rwh.astype(F32)).astype(BF16)
    rb = router_b.astype(F32).reshape(ne, 1)
    fg = final_g.reshape(1, d).astype(F32)
    n_blocks = -(-(nt * TOP_K) // MOE_ROWS) + ne

    for l in range(depth):
        sh1, sc1, g1, sh2, sc2, g2 = [mod[l, :, k] for k in range(N_MOD)]
        zf, zs = _inproj(t, norm1_g[l].reshape(1, d).astype(F32), sc1, sh1, w_in[l].astype(BF16),
                         df, nx_rows, seq)

        hw = fourier_w[l].astype(BF16)
        gain = mix_norm_g[l].astype(F32).reshape(1, -1)
        yf = _fourier_mix(zf, w_chan, tabs_x, hw, gain[:, :df], 0, bsz, seq, jnp.zeros((nt, df), BF16))
        yf = _fourier_mix(zf, w_chan, tabs_c, hw, gain[:, :df], nx_rows, bsz, n_ctx, yf)

        tables = [tab[l] for tab in s5_tabs]
        ysx, ysc = _s5_scan(zs, tables, bsz, seq, n_ctx)
        ys = _glu(ysx, ysc, glu_w[l].astype(BF16), glu_b[l].astype(F32).reshape(1, ds), gain[:, df:])

        t, h3, eidx, ew = _merge(t, yf, ys, w_out[l].astype(BF16), g1,
                                 norm2_g[l].reshape(1, d).astype(F32), sc2, sh2, rwh, rwl, rb, nx_rows, seq)

        n_moe = nx_rows if l == depth - 1 else nt
        n_blocks = -(-(n_moe * TOP_K) // MOE_ROWS) + ne
        buf_tok, blk_exp, n_valid, pos, wtok = _dispatch(eidx[:, :n_moe], ew[:, :n_moe], ne, MOE_ROWS, n_blocks)
        yb = _experts(h3, buf_tok, blk_exp, n_valid, w_gate, w_up, w_down, l)
        last = l == depth - 1
        t = _combine(t, yb, pos, wtok, g2, fg, nx_rows, seq, nx_rows if last else nt, last)

    return t.reshape(bsz, seq, d).astype(x.dtype)
```
